```python
import math
import jax, jax.numpy as jnp
from jax import lax
import numpy as np

D_MODEL = 2048
BATCH = 2
SEQ = 16384
DEPTH = 2

N_A_LAYERS = DEPTH // 2
N_B_LAYERS = DEPTH - N_A_LAYERS

M_HEADS = 8
M_QK_DIM = D_MODEL // 2 // M_HEADS
M_V_DIM = D_MODEL // M_HEADS
M_IN_WIDTH = 2 * M_HEADS * M_QK_DIM + 2 * M_HEADS * M_V_DIM + 2 * M_HEADS
CHUNK = 128

D_HEADS = 8
D_HEAD_DIM = D_MODEL // (2 * D_HEADS)
D_V_DIM = 2 * D_HEAD_DIM
Q_BLOCK = 128

D_FF = 4 * D_MODEL
EPS = 1e-6

kernel_name = "yoco_mlstm_diffattn_alibi_adaln"


def rmsnorm(x, g):
    xf = x.astype(jnp.float32)
    y = xf * lax.rsqrt(jnp.mean(xf * xf, axis=-1, keepdims=True) + EPS)
    return (y * g.astype(jnp.float32)).astype(x.dtype)


def modulate(h, shift, scale):
    return h * (1.0 + scale[:, None, :]) + shift[:, None, :]


def alibi_slopes(n_heads):
    return jnp.asarray([2.0 ** (-8.0 * (h + 1) / n_heads) for h in range(n_heads)], jnp.float32)


def mlstm_chunkwise(q, k, v, logi, logf):
    B_, H_, S_, dk = q.shape
    dv = v.shape[-1]
    nc = S_ // CHUNK

    def to_chunks(t):
        t = t.reshape(t.shape[:2] + (nc, CHUNK) + t.shape[3:])
        return jnp.moveaxis(t, 2, 0)

    xs = (to_chunks(q), to_chunks(k), to_chunks(v), to_chunks(logi), to_chunks(logf))
    causal = jnp.tril(jnp.ones((CHUNK, CHUNK), bool))

    def body(carry, inp):
        C, n, m = carry
        qc, kc, vc, li, lf = inp
        b = jnp.cumsum(lf, axis=-1)
        dmat = b[..., :, None] - b[..., None, :] + li[..., None, :]
        dmat = jnp.where(causal, dmat, -jnp.inf)
        inter = b + m[..., None]
        m_j = jnp.maximum(inter, jnp.max(dmat, axis=-1))
        w_intra = jnp.exp(dmat - m_j[..., None])
        w_inter = jnp.exp(inter - m_j)
        s = jnp.einsum('bhjd,bhsd->bhjs', qc, kc) * w_intra
        num = w_inter[..., None] * jnp.einsum('bhvd,bhjd->bhjv', C, qc) + jnp.einsum('bhjs,bhsv->bhjv', s, vc)
        den = w_inter * jnp.einsum('bhd,bhjd->bhj', n, qc) + jnp.sum(s, axis=-1)
        h = num / jnp.maximum(jnp.abs(den), jnp.exp(-m_j))[..., None]
        b_last = b[..., -1]
        dec = b_last[..., None] - b + li
        m_new = jnp.maximum(b_last + m, jnp.max(dec, axis=-1))
        w_k = jnp.exp(dec - m_new[..., None])
        carry_scale = jnp.exp(b_last + m - m_new)
        C_new = carry_scale[..., None, None] * C + jnp.einsum('bhs,bhsv,bhsd->bhvd', w_k, vc, kc)
        n_new = carry_scale[..., None] * n + jnp.einsum('bhs,bhsd->bhd', w_k, kc)
        return (C_new, n_new, m_new), h

    init = (jnp.zeros((B_, H_, dv, dk), jnp.float32), jnp.zeros((B_, H_, dk), jnp.float32),
            jnp.zeros((B_, H_), jnp.float32))
    _, hs = lax.scan(body, init, xs)
    return jnp.moveaxis(hs, 0, 2).reshape(B_, H_, S_, dv)


def mlstm_mixer(h, w_in, b_gate, g_hnorm, w_out):
    B_, S_, _ = h.shape
    qk_w = M_HEADS * M_QK_DIM
    v_w = M_HEADS * M_V_DIM
    proj = h @ w_in
    q, k, v, o, g = jnp.split(proj, [qk_w, 2 * qk_w, 2 * qk_w + v_w, 2 * qk_w + 2 * v_w], axis=-1)

    def heads(t, d):
        return t.reshape(B_, S_, M_HEADS, d).transpose(0, 2, 1, 3).astype(jnp.float32)

    q = heads(q, M_QK_DIM)
    k = heads(k, M_QK_DIM) * (M_QK_DIM ** -0.5)
    v = heads(v, M_V_DIM)
    g = (g + b_gate).astype(jnp.float32).transpose(0, 2, 1)
    logi = g[:, :M_HEADS]
    logf = jax.nn.log_sigmoid(g[:, M_HEADS:])
    hm = mlstm_chunkwise(q, k, v, logi, logf)
    hm = rmsnorm(hm, g_hnorm.reshape(M_HEADS, 1, M_V_DIM))
    hm = hm.transpose(0, 2, 1, 3).reshape(B_, S_, v_w).astype(h.dtype)
    return (hm * jax.nn.sigmoid(o)) @ w_out


def diff_attention(q, k, v, lam):
    B_, H_, _, S_, _ = q.shape
    dv = v.shape[-1]
    nb = S_ // Q_BLOCK
    slopes = alibi_slopes(H_)
    ar = jnp.arange(Q_BLOCK)

    def one_block(i):
        q_i = lax.dynamic_slice_in_dim(q, i * Q_BLOCK, Q_BLOCK, axis=3)
        q_pos = i * Q_BLOCK + ar

        def step(j, carry):
            m, l, acc = carry
            k_j = lax.dynamic_slice_in_dim(k, j * Q_BLOCK, Q_BLOCK, axis=3)
            v_j = lax.dynamic_slice_in_dim(v, j * Q_BLOCK, Q_BLOCK, axis=2)
            dist = (q_pos[:, None] - (j * Q_BLOCK + ar)[None, :]).astype(jnp.float32)
            s = jnp.einsum('bhmqd,bhmkd->bhmqk', q_i, k_j) - slopes[:, None, None, None] * dist
            s = jnp.where(dist >= 0, s, -jnp.inf)
            m_new = jnp.maximum(m, jnp.max(s, axis=-1))
            p = jnp.exp(s - m_new[..., None])
            corr = jnp.exp(m - m_new)
            l_new = l * corr + jnp.sum(p, axis=-1)
            acc_new = acc * corr[..., None] + jnp.einsum('bhmqk,bhkv->bhmqv', p, v_j)
            return m_new, l_new, acc_new

        init = (jnp.full((B_, H_, 2, Q_BLOCK), -jnp.inf, jnp.float32),
                jnp.zeros((B_, H_, 2, Q_BLOCK), jnp.float32),
                jnp.zeros((B_, H_, 2, Q_BLOCK, dv), jnp.float32))
        _, l, acc = lax.fori_loop(0, i + 1, step, init)
        o = acc / l[..., None]
        return o[:, :, 0] - lam * o[:, :, 1]

    out = lax.map(one_block, jnp.arange(nb))
    return out.transpose(1, 2, 0, 3, 4).reshape(B_, H_, S_, dv)


def diff_attn_mixer(h, k_sh, v_sh, w_q, lam_p, g_subln, w_out, lam_init):
    B_, S_, _ = h.shape
    q = (h @ w_q).reshape(B_, S_, D_HEADS, 2, D_HEAD_DIM).transpose(0, 2, 3, 1, 4)
    q = q.astype(jnp.float32) * (D_HEAD_DIM ** -0.5)
    lp = lam_p.astype(jnp.float32)
    lam = jnp.exp(jnp.sum(lp[0] * lp[1])) - jnp.exp(jnp.sum(lp[2] * lp[3])) + lam_init
    o = diff_attention(q, k_sh, v_sh, lam)
    o = rmsnorm(o, g_subln.reshape(D_HEADS, 1, D_V_DIM)) * (1.0 - lam_init)
    o = o.transpose(0, 2, 1, 3).reshape(B_, S_, D_HEADS * D_V_DIM).astype(h.dtype)
    return o @ w_out


def shared_kv(x, c, g_kv, w_kv_mod, b_kv_mod, w_kv):
    B_, S_, _ = x.shape
    shift, scale = jnp.split(c @ w_kv_mod + b_kv_mod, 2, axis=-1)
    hk = modulate(rmsnorm(x, g_kv), shift, scale)
    kv = hk @ w_kv
    k_w = 2 * D_HEADS * D_HEAD_DIM
    k = kv[..., :k_w].reshape(B_, S_, D_HEADS, 2, D_HEAD_DIM).transpose(0, 2, 3, 1, 4).astype(jnp.float32)
    v = kv[..., k_w:].reshape(B_, S_, D_HEADS, D_V_DIM).transpose(0, 2, 1, 3).astype(jnp.float32)
    return k, v


def setup_inputs(seed: int = 0) -> dict:
    key = jax.random.key(seed)
    ks = jax.random.split(key, 24)

    def nrm(k, shape, scale):
        return jax.random.normal(k, shape, jnp.float32) * scale

    def gain(k, shape):
        return 1.0 + nrm(k, shape, 0.02)

    f_bias = jnp.linspace(3.0, 6.0, M_HEADS, dtype=jnp.float32)[None, :] + nrm(ks[8], (N_A_LAYERS, M_HEADS), 0.1)
    i_bias = nrm(ks[9], (N_A_LAYERS, M_HEADS), 0.1)
    return {
        "x": nrm(ks[0], (BATCH, SEQ, D_MODEL), 1.0),
        "c": nrm(ks[1], (BATCH, D_MODEL), 1.0),
        "w_mod": nrm(ks[2], (DEPTH, D_MODEL, 6 * D_MODEL), 0.5 * D_MODEL ** -0.5),
        "b_mod": nrm(ks[3], (DEPTH, 6 * D_MODEL), 0.01),
        "g_norm": gain(ks[4], (DEPTH, 2, D_MODEL)),
        "w_up": nrm(ks[5], (DEPTH, D_MODEL, D_FF), D_MODEL ** -0.5),
        "w_down": nrm(ks[6], (DEPTH, D_FF, D_MODEL), D_FF ** -0.5),
        "w_a_in": nrm(ks[7], (N_A_LAYERS, D_MODEL, M_IN_WIDTH), D_MODEL ** -0.5),
        "b_a_gate": jnp.concatenate([i_bias, f_bias], axis=-1),
        "g_a_hnorm": gain(ks[10], (N_A_LAYERS, M_HEADS * M_V_DIM)),
        "w_a_out": nrm(ks[11], (N_A_LAYERS, M_HEADS * M_V_DIM, D_MODEL), (M_HEADS * M_V_DIM) ** -0.5),
        "g_kv": gain(ks[12], (D_MODEL,)),
        "w_kv_mod": nrm(ks[13], (D_MODEL, 2 * D_MODEL), 0.5 * D_MODEL ** -0.5),
        "b_kv_mod": nrm(ks[14], (2 * D_MODEL,), 0.01),
        "w_kv": nrm(ks[15], (D_MODEL, 2 * D_HEADS * D_HEAD_DIM + D_HEADS * D_V_DIM), D_MODEL ** -0.5),
        "w_b_q": nrm(ks[16], (N_B_LAYERS, D_MODEL, 2 * D_HEADS * D_HEAD_DIM), D_MODEL ** -0.5),
        "b_lambda": nrm(ks[17], (N_B_LAYERS, 4, D_HEAD_DIM), 0.1),
        "g_b_subln": gain(ks[18], (N_B_LAYERS, D_HEADS * D_V_DIM)),
        "w_b_out": nrm(ks[19], (N_B_LAYERS, D_HEADS * D_V_DIM, D_MODEL), (D_HEADS * D_V_DIM) ** -0.5),
        "g_final": gain(ks[20], (D_MODEL,)),
    }


def reference(x, c, w_mod, b_mod, g_norm, w_up, w_down, w_a_in, b_a_gate, g_a_hnorm, w_a_out,
              g_kv, w_kv_mod, b_kv_mod, w_kv, w_b_q, b_lambda, g_b_subln, w_b_out, g_final):
    k_sh = v_sh = None
    for l in range(DEPTH):
        mod = c @ w_mod[l] + b_mod[l]
        sh1, sc1, gt1, sh2, sc2, gt2 = jnp.split(mod, 6, axis=-1)
        h = modulate(rmsnorm(x, g_norm[l, 0]), sh1, sc1)
        if l < N_A_LAYERS:
            mix = mlstm_mixer(h, w_a_in[l], b_a_gate[l], g_a_hnorm[l], w_a_out[l])
        else:
            if l == N_A_LAYERS:
                k_sh, v_sh = shared_kv(x, c, g_kv, w_kv_mod, b_kv_mod, w_kv)
            j = l - N_A_LAYERS
            lam_init = 0.8 - 0.6 * math.exp(-0.3 * l)
            mix = diff_attn_mixer(h, k_sh, v_sh, w_b_q[j], b_lambda[j], g_b_subln[j], w_b_out[j], lam_init)
        x = x + gt1[:, None, :] * mix
        h = modulate(rmsnorm(x, g_norm[l, 1]), sh2, sc2)
        ff = jnp.square(jax.nn.relu(h @ w_up[l])) @ w_down[l]
        x = x + gt2[:, None, :] * ff
    return rmsnorm(x, g_final)
```

```python
import functools
import math

import jax
import jax.numpy as jnp
from jax import lax
from jax.experimental import pallas as pl
from jax.experimental.pallas import tpu as pltpu

_EPS = 1e-6
_F32 = jnp.float32
_BF16 = jnp.bfloat16

_LANES = 128
_SUBLANES = 8
_VMEM_LIMIT_BYTES = 56 * 1024 * 1024

_MODVEC_TN = 1024
_PROJ_TM = 1024
_PROJ_TN = 1024
_MLP_TM = 1024
_MLP_TF = 512
_NORM_ROWS = 64
_MLSTM_CHUNK = 128
_ATTN_BLOCK = 512

_NT_DIMS = (((1,), (1,)), ((), ()))
_TN_DIMS = (((0,), (0,)), ((), ()))


def _params(*semantics):
    return pltpu.CompilerParams(dimension_semantics=semantics, vmem_limit_bytes=_VMEM_LIMIT_BYTES)


def _modvec_kernel(c_ref, w_ref, b_ref, o_ref):
    o_ref[...] = jnp.dot(c_ref[...].astype(_BF16), w_ref[...].astype(_BF16),
                         preferred_element_type=_F32) + b_ref[...]


def _modvec(c_rows, w, b):
    n_l, d, n = w.shape
    tn = _MODVEC_TN
    return pl.pallas_call(
        _modvec_kernel,
        grid=(n_l, n // tn),
        in_specs=[pl.BlockSpec((_SUBLANES, d), lambda l, j: (0, 0)),
                  pl.BlockSpec((None, d, tn), lambda l, j: (l, 0, j)),
                  pl.BlockSpec((None, 1, tn), lambda l, j: (l, 0, j))],
        out_specs=pl.BlockSpec((None, _SUBLANES, tn), lambda l, j: (l, 0, j)),
        out_shape=jax.ShapeDtypeStruct((n_l, _SUBLANES, n), _F32),
        compiler_params=_params("parallel", "parallel"),
        name="modvec",
    )(c_rows, w, b)


def _norm_mod_tile(x_ref, gain, shift, h_ref):
    n_chunks = x_ref.shape[0] // _NORM_ROWS

    def body(r, carry):
        rows = pl.ds(pl.multiple_of(r * _NORM_ROWS, _NORM_ROWS), _NORM_ROWS)
        x = x_ref[rows, :]
        inv = lax.rsqrt(jnp.mean(x * x, axis=-1, keepdims=True) + _EPS)
        h_ref[rows, :] = (x * inv * gain + shift).astype(h_ref.dtype)
        return carry

    lax.fori_loop(0, n_chunks, body, 0)


def _norm_mod_matmul_kernel(x_ref, g_ref, sc_ref, sh_ref, w_ref, cs_ref, o_ref, h_ref):
    @pl.when(pl.program_id(1) == 0)
    def _():
        _norm_mod_tile(x_ref, g_ref[...] * (1.0 + sc_ref[...]), sh_ref[...], h_ref)

    acc = jnp.dot(h_ref[...], w_ref[...], preferred_element_type=_F32)
    o_ref[...] = (acc * cs_ref[...]).astype(o_ref.dtype)


def _norm_mod_matmul(x, g, scale, shift, w, col_scale, out_dtype, seq):
    t, d = x.shape
    n = w.shape[1]
    tm = min(_PROJ_TM, seq)
    tn = min(_PROJ_TN, n)
    tiles_per_seq = seq // tm
    return pl.pallas_call(
        _norm_mod_matmul_kernel,
        grid=(t // tm, n // tn),
        in_specs=[pl.BlockSpec((tm, d), lambda i, j: (i, 0)),
                  pl.BlockSpec((1, d), lambda i, j: (0, 0)),
                  pl.BlockSpec((None, 1, d), lambda i, j: (i // tiles_per_seq, 0, 0)),
                  pl.BlockSpec((None, 1, d), lambda i, j: (i // tiles_per_seq, 0, 0)),
                  pl.BlockSpec((d, tn), lambda i, j: (0, j)),
                  pl.BlockSpec((1, tn), lambda i, j: (0, j))],
        out_specs=pl.BlockSpec((tm, tn), lambda i, j: (i, j)),
        out_shape=jax.ShapeDtypeStruct((t, n), out_dtype),
        scratch_shapes=[pltpu.VMEM((tm, d), _BF16)],
        compiler_params=_params("parallel", "arbitrary"),
        name="norm_mod_matmul",
    )(x, g, scale, shift, w, col_scale)


def _matmul_residual_kernel(a_ref, w_ref, x_ref, gate_ref, o_ref):
    acc = jnp.dot(a_ref[...], w_ref[...], preferred_element_type=_F32)
    o_ref[...] = x_ref[...] + gate_ref[...] * acc


def _matmul_residual(a, w, x, gate, seq):
    t, k = a.shape
    d = w.shape[1]
    tm = min(_PROJ_TM, seq)
    tn = min(_PROJ_TN, d)
    tiles_per_seq = seq // tm
    return pl.pallas_call(
        _matmul_residual_kernel,
        grid=(t // tm, d // tn),
        in_specs=[pl.BlockSpec((tm, k), lambda i, j: (i, 0)),
                  pl.BlockSpec((k, tn), lambda i, j: (0, j)),
                  pl.BlockSpec((tm, tn), lambda i, j: (i, j)),
                  pl.BlockSpec((None, 1, tn), lambda i, j: (i // tiles_per_seq, 0, j))],
        out_specs=pl.BlockSpec((tm, tn), lambda i, j: (i, j)),
        out_shape=jax.ShapeDtypeStruct((t, d), _F32),
        compiler_params=_params("parallel", "parallel"),
        name="matmul_residual",
    )(a, w, x, gate)


def _mlp_kernel(x_ref, g_ref, sc_ref, sh_ref, gate_ref, wu_ref, wd_ref, gf_ref, o_ref, h_ref, *, final_norm):
    f = pl.program_id(1)

    @pl.when(f == 0)
    def _():
        _norm_mod_tile(x_ref, g_ref[...] * (1.0 + sc_ref[...]), sh_ref[...], h_ref)

    u = jnp.dot(h_ref[...], wu_ref[...], preferred_element_type=_F32)
    u = jnp.square(jnp.maximum(u, 0.0)).astype(_BF16)
    part = jnp.dot(u, wd_ref[...], preferred_element_type=_F32)

    @pl.when(f == 0)
    def _():
        o_ref[...] = part

    @pl.when(f > 0)
    def _():
        o_ref[...] += part

    @pl.when(f == pl.num_programs(1) - 1)
    def _():
        gate = gate_ref[...]
        gain = gf_ref[...]
        n_chunks = x_ref.shape[0] // _NORM_ROWS

        def body(r, carry):
            rows = pl.ds(pl.multiple_of(r * _NORM_ROWS, _NORM_ROWS), _NORM_ROWS)
            y = x_ref[rows, :] + gate * o_ref[rows, :]
            if final_norm:
                y = y * lax.rsqrt(jnp.mean(y * y, axis=-1, keepdims=True) + _EPS) * gain
            o_ref[rows, :] = y
            return carry

        lax.fori_loop(0, n_chunks, body, 0)


def _mlp(x, g, scale, shift, gate, w_up, w_down, g_final, seq, final_norm):
    t, d = x.shape
    d_ff = w_up.shape[1]
    tm = min(_MLP_TM, seq)
    tf = min(_MLP_TF, d_ff)
    tiles_per_seq = seq // tm
    batch_vec = pl.BlockSpec((None, 1, d), lambda i, f: (i // tiles_per_seq, 0, 0))
    return pl.pallas_call(
        functools.partial(_mlp_kernel, final_norm=final_norm),
        grid=(t // tm, d_ff // tf),
        in_specs=[pl.BlockSpec((tm, d), lambda i, f: (i, 0)),
                  pl.BlockSpec((1, d), lambda i, f: (0, 0)),
                  batch_vec, batch_vec, batch_vec,
                  pl.BlockSpec((d, tf), lambda i, f: (0, f)),
                  pl.BlockSpec((tf, d), lambda i, f: (f, 0)),
                  pl.BlockSpec((1, d), lambda i, f: (0, 0))],
        out_specs=pl.BlockSpec((tm, d), lambda i, f: (i, 0)),
        out_shape=jax.ShapeDtypeStruct((t, d), _F32),
        scratch_shapes=[pltpu.VMEM((tm, d), _BF16)],
        compiler_params=_params("parallel", "arbitrary"),
        name="mlp",
    )(x, g, scale, shift, gate, w_up, w_down, g_final)


def _log_sigmoid(x):
    return jnp.minimum(x, 0.0) - jnp.log1p(jnp.exp(-jnp.abs(x)))


def _mlstm_kernel(q_ref, k_ref, v_ref, og_ref, gt_ref, bg_ref, gh_ref, out_ref, c_ref, n_ref, m_ref,
                  *, heads, dk, dv):
    @pl.when(pl.program_id(1) == 0)
    def _():
        c_ref[...] = jnp.zeros_like(c_ref)
        n_ref[...] = jnp.zeros_like(n_ref)
        m_ref[...] = jnp.zeros_like(m_ref)

    chunk = q_ref.shape[0]
    gates = gt_ref[...] + bg_ref[...]
    logf = _log_sigmoid(gates)
    row = lax.broadcasted_iota(jnp.int32, (chunk, chunk), 0)
    col = lax.broadcasted_iota(jnp.int32, (chunk, chunk), 1)
    causal = col <= row
    b_all = jnp.dot(causal.astype(_F32), logf, preferred_element_type=_F32, precision=lax.Precision.HIGHEST)
    gates_t = gates.T
    b_all_t = b_all.T

    for h in range(heads):
        li_col = gates[:, h:h + 1]
        b_col = b_all[:, heads + h:heads + h + 1]
        li_row = gates_t[h:h + 1, :]
        b_row = b_all_t[heads + h:heads + h + 1, :]
        m_prev = m_ref[h]
        c_t = c_ref[h]
        n_vec = n_ref[h]
        qh = q_ref[:, h * dk:(h + 1) * dk]
        kh = k_ref[:, h * dk:(h + 1) * dk]
        vh = v_ref[:, h * dv:(h + 1) * dv]

        dmat = jnp.where(causal, b_col - b_row + li_row, -jnp.inf)
        inter = b_col + m_prev
        m_j = jnp.maximum(inter, jnp.max(dmat, axis=-1, keepdims=True))
        w_intra = jnp.exp(dmat - m_j)
        w_inter = jnp.exp(inter - m_j)
        s = lax.dot_general(qh, kh, _NT_DIMS, preferred_element_type=_F32) * w_intra
        num = (w_inter * jnp.dot(qh, c_t.astype(_BF16), preferred_element_type=_F32)
               + jnp.dot(s.astype(_BF16), vh, preferred_element_type=_F32))
        den = (w_inter * jnp.sum(qh.astype(_F32) * n_vec, axis=-1, keepdims=True)
               + jnp.sum(s, axis=-1, keepdims=True))
        hh = num / jnp.maximum(jnp.abs(den), jnp.exp(-m_j))

        b_last = b_col[chunk - 1:chunk, :]
        dec = b_last - b_col + li_col
        m_new = jnp.maximum(b_last + m_prev, jnp.max(dec, axis=0, keepdims=True))
        kw = kh.astype(_F32) * jnp.exp(dec - m_new)
        carry_scale = jnp.exp(b_last + m_prev - m_new)
        c_ref[h] = carry_scale * c_t + lax.dot_general(kw.astype(_BF16), vh, _TN_DIMS,
                                                       preferred_element_type=_F32)
        n_ref[h] = carry_scale * n_vec + jnp.sum(kw, axis=0, keepdims=True)
        m_ref[h] = m_new

        y = hh * lax.rsqrt(jnp.mean(hh * hh, axis=-1, keepdims=True) + _EPS) * gh_ref[:, h * dv:(h + 1) * dv]
        og = og_ref[:, h * dv:(h + 1) * dv].astype(_F32)
        out_ref[:, h * dv:(h + 1) * dv] = (y * jax.nn.sigmoid(og)).astype(out_ref.dtype)


def _mlstm(proj, gates, b_gate, g_hnorm, batch, seq, heads, dk, dv):
    t = proj.shape[0]
    chunk = _MLSTM_CHUNK
    n_chunks = seq // chunk
    qk_w, v_w = heads * dk, heads * dv
    assert v_w == 2 * qk_w

    def rows(b, c):
        return b * n_chunks + c

    return pl.pallas_call(
        functools.partial(_mlstm_kernel, heads=heads, dk=dk, dv=dv),
        grid=(batch, n_chunks),
        in_specs=[pl.BlockSpec((chunk, qk_w), lambda b, c: (rows(b, c), 0)),
                  pl.BlockSpec((chunk, qk_w), lambda b, c: (rows(b, c), 1)),
                  pl.BlockSpec((chunk, v_w), lambda b, c: (rows(b, c), 1)),
                  pl.BlockSpec((chunk, v_w), lambda b, c: (rows(b, c), 2)),
                  pl.BlockSpec((chunk, _LANES), lambda b, c: (rows(b, c), 0)),
                  pl.BlockSpec((1, _LANES), lambda b, c: (0, 0)),
                  pl.BlockSpec((1, v_w), lambda b, c: (0, 0))],
        out_specs=pl.BlockSpec((chunk, v_w), lambda b, c: (rows(b, c), 0)),
        out_shape=jax.ShapeDtypeStruct((t, v_w), _BF16),
        scratch_shapes=[pltpu.VMEM((heads, dk, dv), _F32),
                        pltpu.VMEM((heads, 1, dk), _F32),
                        pltpu.VMEM((heads, 1, 1), _F32)],
        compiler_params=_params("parallel", "arbitrary"),
        name="mlstm",
    )(proj, proj, proj, proj, gates, b_gate, g_hnorm)


def _attn_kernel(q_ref, k_ref, v_ref, slope_ref, lam_ref, g_ref, o_ref, bias_ref, m_ref, l_ref, acc_ref,
                 *, dh, lam_init):
    qi = pl.program_id(2)
    blk = q_ref.shape[0]
    slope = slope_ref[:, 0:1]

    row = lax.broadcasted_iota(jnp.int32, (blk, blk), 0)
    col = lax.broadcasted_iota(jnp.int32, (blk, blk), 1)
    bias_ref[...] = -slope * (row - col).astype(_F32)

    m_ref[...] = jnp.full_like(m_ref, -jnp.inf)
    l_ref[...] = jnp.zeros_like(l_ref)
    acc_ref[...] = jnp.zeros_like(acc_ref)

    def step(kj, masked):
        keys = pl.ds(pl.multiple_of(kj * blk, blk), blk)
        v_blk = v_ref[keys, :]
        block_bias = -slope * jnp.full((1, 1), (qi - kj) * blk, jnp.int32).astype(_F32)
        for s_idx in range(2):
            q = q_ref[:, s_idx * dh:(s_idx + 1) * dh]
            k = k_ref[keys, s_idx * dh:(s_idx + 1) * dh]
            s = lax.dot_general(q, k, _NT_DIMS, preferred_element_type=_F32) + bias_ref[...]
            if masked:
                s = jnp.where(col <= row, s, -jnp.inf)
            m_old = m_ref[s_idx]
            m_new = jnp.maximum(m_old, jnp.max(s, axis=-1, keepdims=True) + block_bias)
            p = jnp.exp(s - (m_new - block_bias))
            corr = jnp.exp(m_old - m_new)
            l_ref[s_idx] = l_ref[s_idx] * corr + jnp.sum(p, axis=-1, keepdims=True)
            acc_ref[s_idx] = acc_ref[s_idx] * corr + jnp.dot(p.astype(_BF16), v_blk,
                                                             preferred_element_type=_F32)
            m_ref[s_idx] = m_new

    def body(kj, carry):
        step(kj, masked=False)
        return carry

    lax.fori_loop(0, qi, body, 0)
    step(qi, masked=True)

    lp = lam_ref[...]
    lam = (jnp.exp(jnp.sum(lp[0:1, :] * lp[1:2, :], axis=-1, keepdims=True))
           - jnp.exp(jnp.sum(lp[2:3, :] * lp[3:4, :], axis=-1, keepdims=True)) + lam_init)
    o = acc_ref[0] / l_ref[0] - lam * (acc_ref[1] / l_ref[1])
    o = o * lax.rsqrt(jnp.mean(o * o, axis=-1, keepdims=True) + _EPS) * g_ref[...] * (1.0 - lam_init)
    o_ref[...] = o.astype(o_ref.dtype)


def _diff_attention(q, kv, lam_p, g_subln, batch, seq, heads, dh, lam_init):
    t = q.shape[0]
    dv = 2 * dh
    blk = min(_ATTN_BLOCK, seq)
    n_blk = seq // blk
    slopes = jnp.asarray([2.0 ** (-8.0 * (h + 1) / heads) for h in range(heads)], _F32)
    slopes = jnp.broadcast_to(slopes[:, None, None], (heads, 1, _LANES))
    return pl.pallas_call(
        functools.partial(_attn_kernel, dh=dh, lam_init=lam_init),
        grid=(batch, heads, n_blk),
        in_specs=[pl.BlockSpec((blk, dv), lambda b, h, i: (b * n_blk + i, h)),
                  pl.BlockSpec((seq, dv), lambda b, h, i: (b, h)),
                  pl.BlockSpec((seq, dv), lambda b, h, i: (b, heads + h)),
                  pl.BlockSpec((None, 1, _LANES), lambda b, h, i: (h, 0, 0)),
                  pl.BlockSpec((4, dh), lambda b, h, i: (0, 0)),
                  pl.BlockSpec((1, dv), lambda b, h, i: (0, h))],
        out_specs=pl.BlockSpec((blk, dv), lambda b, h, i: (b * n_blk + i, h)),
        out_shape=jax.ShapeDtypeStruct((t, heads * dv), _BF16),
        scratch_shapes=[pltpu.VMEM((blk, blk), _F32),
                        pltpu.VMEM((2, blk, 1), _F32),
                        pltpu.VMEM((2, blk, 1), _F32),
                        pltpu.VMEM((2, blk, dv), _F32)],
        compiler_params=_params("parallel", "parallel", "arbitrary"),
        name="diff_attention",
    )(q, kv, kv, slopes, lam_p, g_subln)


def kernel(x, c, w_mod, b_mod, g_norm, w_up, w_down, w_a_in, b_a_gate, g_a_hnorm, w_a_out, g_kv, w_kv_mod,
           b_kv_mod, w_kv, w_b_q, b_lambda, g_b_subln, w_b_out, g_final):
    batch, seq, d = x.shape
    depth = w_mod.shape[0]
    n_a = w_a_in.shape[0]
    m_heads = b_a_gate.shape[1] // 2
    m_dv = g_a_hnorm.shape[1] // m_heads
    m_dk = (w_a_in.shape[2] - 2 * m_heads * m_dv - 2 * m_heads) // (2 * m_heads)
    d_dh = b_lambda.shape[2]
    d_heads = w_b_q.shape[2] // (2 * d_dh)
    t = batch * seq

    xt = x.reshape(t, d)
    c_rows = jnp.zeros((_SUBLANES, d), _F32).at[:batch].set(c)

    mod = _modvec(c_rows, w_mod, b_mod.reshape(depth, 1, 6 * d))[:, :batch]
    kv_mod = _modvec(c_rows, w_kv_mod[None], b_kv_mod.reshape(1, 1, 2 * d))[0, :batch]

    def vec(v):
        return v.reshape(batch, 1, d)

    ones_d = jnp.ones((1, d), _F32)
    for l in range(depth):
        sh1, sc1, gt1, sh2, sc2, gt2 = (vec(m) for m in jnp.split(mod[l], 6, axis=-1))
        if l < n_a:
            qk_w, v_w = m_heads * m_dk, m_heads * m_dv
            main_w = 2 * qk_w + 2 * v_w
            w_in = w_a_in[l]
            col_scale = jnp.concatenate([jnp.ones((qk_w,), _F32), jnp.full((qk_w,), m_dk ** -0.5, _F32),
                                         jnp.ones((2 * v_w,), _F32)]).reshape(1, main_w)
            proj = _norm_mod_matmul(xt, g_norm[l, 0][None], sc1, sh1, w_in[:, :main_w].astype(_BF16),
                                    col_scale, _BF16, seq)
            w_gate = jnp.zeros((d, _LANES), _BF16).at[:, :2 * m_heads].set(w_in[:, main_w:].astype(_BF16))
            gates = _norm_mod_matmul(xt, g_norm[l, 0][None], sc1, sh1, w_gate,
                                     jnp.ones((1, _LANES), _F32), _F32, seq)
            b_gate = jnp.zeros((1, _LANES), _F32).at[0, :2 * m_heads].set(b_a_gate[l])
            mix_in = _mlstm(proj, gates, b_gate, g_a_hnorm[l][None], batch, seq, m_heads, m_dk, m_dv)
            w_out = w_a_out[l]
        else:
            j = l - n_a
            lam_init = 0.8 - 0.6 * math.exp(-0.3 * l)
            if l == n_a:
                kv_shift, kv_scale = (vec(m) for m in jnp.split(kv_mod, 2, axis=-1))
                kv = _norm_mod_matmul(xt, g_kv[None], kv_scale, kv_shift, w_kv.astype(_BF16),
                                      jnp.ones((1, w_kv.shape[1]), _F32), _BF16, seq)
            q = _norm_mod_matmul(xt, g_norm[l, 0][None], sc1, sh1, w_b_q[j].astype(_BF16),
                                 jnp.full((1, w_b_q.shape[2]), d_dh ** -0.5, _F32), _BF16, seq)
            mix_in = _diff_attention(q, kv, b_lambda[j], g_b_subln[j][None], batch, seq, d_heads, d_dh, lam_init)
            w_out = w_b_out[j]
        xt = _matmul_residual(mix_in, w_out.astype(_BF16), xt, gt1, seq)
        last = l == depth - 1
        xt = _mlp(xt, g_norm[l, 1][None], sc2, sh2, gt2, w_up[l].astype(_BF16), w_down[l].astype(_BF16),
                  g_final[None] if last else ones_d, seq, last)
    return xt.reshape(batch, seq, d)
```

```python
import functools
import math

import jax
import jax.numpy as jnp
from jax import lax
from jax.experimental import pallas as pl
from jax.experimental.pallas import tpu as pltpu

_EPS = 1e-6
_F32 = jnp.float32
_BF16 = jnp.bfloat16

_LANES = 128
_SUBLANES = 8
_BF16_ROWS = 16
_VMEM_LIMIT_BYTES = 56 * 1024 * 1024

_MODVEC_TN = 1024
_PROJ_TM = 1024
_PROJ_TN = 1024
_MLP_TM = 1024
_MLP_TF = 512
_NORM_ROWS = 64
_MLSTM_CHUNK = 128
_ATTN_BLOCK = 512
_QKV_TN = 512

_LOG2E = math.log2(math.e)

_NT_DIMS = (((1,), (1,)), ((), ()))
_TN_DIMS = (((0,), (0,)), ((), ()))


def _params(*semantics):
    return pltpu.CompilerParams(dimension_semantics=semantics, vmem_limit_bytes=_VMEM_LIMIT_BYTES)


def _modvec_kernel(c_ref, w_ref, b_ref, o_ref):
    o_ref[...] = jnp.dot(c_ref[...].astype(_BF16), w_ref[...].astype(_BF16),
                         preferred_element_type=_F32) + b_ref[...]


def _modvec(c_rows, w, b):
    n_l, d, n = w.shape
    tn = _MODVEC_TN
    return pl.pallas_call(
        _modvec_kernel,
        grid=(n_l, n // tn),
        in_specs=[pl.BlockSpec((_SUBLANES, d), lambda l, j: (0, 0)),
                  pl.BlockSpec((None, d, tn), lambda l, j: (l, 0, j)),
                  pl.BlockSpec((None, 1, tn), lambda l, j: (l, 0, j))],
        out_specs=pl.BlockSpec((None, _SUBLANES, tn), lambda l, j: (l, 0, j)),
        out_shape=jax.ShapeDtypeStruct((n_l, _SUBLANES, n), _F32),
        compiler_params=_params("parallel", "parallel"),
        name="modvec",
    )(c_rows, w, b)


def _norm_mod_tile(x_ref, gain, shift, h_ref):
    n_chunks = x_ref.shape[0] // _NORM_ROWS

    def body(r, carry):
        rows = pl.ds(pl.multiple_of(r * _NORM_ROWS, _NORM_ROWS), _NORM_ROWS)
        x = x_ref[rows, :]
        inv = lax.rsqrt(jnp.mean(x * x, axis=-1, keepdims=True) + _EPS)
        h_ref[rows, :] = (x * inv * gain + shift).astype(h_ref.dtype)
        return carry

    lax.fori_loop(0, n_chunks, body, 0)


def _norm_mod_matmul_kernel(x_ref, g_ref, sc_ref, sh_ref, w_ref, cs_ref, o_ref, h_ref):
    @pl.when(pl.program_id(1) == 0)
    def _():
        _norm_mod_tile(x_ref, g_ref[...] * (1.0 + sc_ref[...]), sh_ref[...], h_ref)

    acc = jnp.dot(h_ref[...], w_ref[...], preferred_element_type=_F32)
    o_ref[...] = (acc * cs_ref[...]).astype(o_ref.dtype)


def _norm_mod_matmul(x, g, scale, shift, w, col_scale, out_dtype, seq):
    t, d = x.shape
    n = w.shape[1]
    tm = min(_PROJ_TM, seq)
    tn = min(_PROJ_TN, n)
    tiles_per_seq = seq // tm
    return pl.pallas_call(
        _norm_mod_matmul_kernel,
        grid=(t // tm, n // tn),
        in_specs=[pl.BlockSpec((tm, d), lambda i, j: (i, 0)),
                  pl.BlockSpec((1, d), lambda i, j: (0, 0)),
                  pl.BlockSpec((None, 1, d), lambda i, j: (i // tiles_per_seq, 0, 0)),
                  pl.BlockSpec((None, 1, d), lambda i, j: (i // tiles_per_seq, 0, 0)),
                  pl.BlockSpec((d, tn), lambda i, j: (0, j)),
                  pl.BlockSpec((1, tn), lambda i, j: (0, j))],
        out_specs=pl.BlockSpec((tm, tn), lambda i, j: (i, j)),
        out_shape=jax.ShapeDtypeStruct((t, n), out_dtype),
        scratch_shapes=[pltpu.VMEM((tm, d), _BF16)],
        compiler_params=_params("parallel", "arbitrary"),
        name="norm_mod_matmul",
    )(x, g, scale, shift, w, col_scale)


def _matmul_residual_kernel(a_ref, w_ref, x_ref, gate_ref, o_ref):
    acc = jnp.dot(a_ref[...], w_ref[...], preferred_element_type=_F32)
    o_ref[...] = x_ref[...] + gate_ref[...] * acc


def _matmul_residual(a, w, x, gate, seq):
    t, k = a.shape
    d = w.shape[1]
    tm = min(_PROJ_TM, seq)
    tn = min(_PROJ_TN, d)
    tiles_per_seq = seq // tm
    return pl.pallas_call(
        _matmul_residual_kernel,
        grid=(t // tm, d // tn),
        in_specs=[pl.BlockSpec((tm, k), lambda i, j: (i, 0)),
                  pl.BlockSpec((k, tn), lambda i, j: (0, j)),
                  pl.BlockSpec((tm, tn), lambda i, j: (i, j)),
                  pl.BlockSpec((None, 1, tn), lambda i, j: (i // tiles_per_seq, 0, j))],
        out_specs=pl.BlockSpec((tm, tn), lambda i, j: (i, j)),
        out_shape=jax.ShapeDtypeStruct((t, d), _F32),
        compiler_params=_params("parallel", "parallel"),
        name="matmul_residual",
    )(a, w, x, gate)


def _mlp_kernel(x_ref, g_ref, sc_ref, sh_ref, gate_ref, wu_ref, wd_ref, gf_ref, o_ref, h_ref, *, final_norm):
    f = pl.program_id(1)

    @pl.when(f == 0)
    def _():
        _norm_mod_tile(x_ref, g_ref[...] * (1.0 + sc_ref[...]), sh_ref[...], h_ref)

    u = jnp.dot(h_ref[...], wu_ref[...], preferred_element_type=_F32)
    u = jnp.square(jnp.maximum(u, 0.0)).astype(_BF16)
    part = jnp.dot(u, wd_ref[...], preferred_element_type=_F32)

    @pl.when(f == 0)
    def _():
        o_ref[...] = part

    @pl.when(f > 0)
    def _():
        o_ref[...] += part

    @pl.when(f == pl.num_programs(1) - 1)
    def _():
        gate = gate_ref[...]
        gain = gf_ref[...]
        n_chunks = x_ref.shape[0] // _NORM_ROWS

        def body(r, carry):
            rows = pl.ds(pl.multiple_of(r * _NORM_ROWS, _NORM_ROWS), _NORM_ROWS)
            y = x_ref[rows, :] + gate * o_ref[rows, :]
            if final_norm:
                y = y * lax.rsqrt(jnp.mean(y * y, axis=-1, keepdims=True) + _EPS) * gain
            o_ref[rows, :] = y
            return carry

        lax.fori_loop(0, n_chunks, body, 0)


def _mlp(x, g, scale, shift, gate, w_up, w_down, g_final, seq, final_norm):
    t, d = x.shape
    d_ff = w_up.shape[1]
    tm = min(_MLP_TM, seq)
    tf = min(_MLP_TF, d_ff)
    tiles_per_seq = seq // tm
    batch_vec = pl.BlockSpec((None, 1, d), lambda i, f: (i // tiles_per_seq, 0, 0))
    return pl.pallas_call(
        functools.partial(_mlp_kernel, final_norm=final_norm),
        grid=(t // tm, d_ff // tf),
        in_specs=[pl.BlockSpec((tm, d), lambda i, f: (i, 0)),
                  pl.BlockSpec((1, d), lambda i, f: (0, 0)),
                  batch_vec, batch_vec, batch_vec,
                  pl.BlockSpec((d, tf), lambda i, f: (0, f)),
                  pl.BlockSpec((tf, d), lambda i, f: (f, 0)),
                  pl.BlockSpec((1, d), lambda i, f: (0, 0))],
        out_specs=pl.BlockSpec((tm, d), lambda i, f: (i, 0)),
        out_shape=jax.ShapeDtypeStruct((t, d), _F32),
        scratch_shapes=[pltpu.VMEM((tm, d), _BF16)],
        compiler_params=_params("parallel", "arbitrary"),
        name="mlp",
    )(x, g, scale, shift, gate, w_up, w_down, g_final)


def _log_sigmoid(x):
    return jnp.minimum(x, 0.0) - jnp.log1p(jnp.exp(-jnp.abs(x)))


def _mlstm_kernel(q_ref, k_ref, v_ref, og_ref, gt_ref, bg_ref, gh_ref, out_ref, c_ref, n_ref, m_ref,
                  *, heads, dk, dv):
    @pl.when(pl.program_id(1) == 0)
    def _():
        c_ref[...] = jnp.zeros_like(c_ref)
        n_ref[...] = jnp.zeros_like(n_ref)
        m_ref[...] = jnp.zeros_like(m_ref)

    chunk = q_ref.shape[0]
    gates = gt_ref[...] + bg_ref[...]
    logf = _log_sigmoid(gates)
    row = lax.broadcasted_iota(jnp.int32, (chunk, chunk), 0)
    col = lax.broadcasted_iota(jnp.int32, (chunk, chunk), 1)
    causal = col <= row
    b_all = jnp.dot(causal.astype(_F32), logf, preferred_element_type=_F32, precision=lax.Precision.HIGHEST)
    gates_t = gates.T
    b_all_t = b_all.T

    for h in range(heads):
        li_col = gates[:, h:h + 1]
        b_col = b_all[:, heads + h:heads + h + 1]
        li_row = gates_t[h:h + 1, :]
        b_row = b_all_t[heads + h:heads + h + 1, :]
        m_prev = m_ref[h]
        c_t = c_ref[h]
        n_vec = n_ref[h]
        qh = q_ref[:, h * dk:(h + 1) * dk]
        kh = k_ref[:, h * dk:(h + 1) * dk]
        vh = v_ref[:, h * dv:(h + 1) * dv]

        dmat = jnp.where(causal, b_col - b_row + li_row, -jnp.inf)
        inter = b_col + m_prev
        m_j = jnp.maximum(inter, jnp.max(dmat, axis=-1, keepdims=True))
        w_intra = jnp.exp(dmat - m_j)
        w_inter = jnp.exp(inter - m_j)
        s = lax.dot_general(qh, kh, _NT_DIMS, preferred_element_type=_F32) * w_intra
        num = (w_inter * jnp.dot(qh, c_t.astype(_BF16), preferred_element_type=_F32)
               + jnp.dot(s.astype(_BF16), vh, preferred_element_type=_F32))
        den = (w_inter * jnp.sum(qh.astype(_F32) * n_vec, axis=-1, keepdims=True)
               + jnp.sum(s, axis=-1, keepdims=True))
        hh = num / jnp.maximum(jnp.abs(den), jnp.exp(-m_j))

        b_last = b_col[chunk - 1:chunk, :]
        dec = b_last - b_col + li_col
        m_new = jnp.maximum(b_last + m_prev, jnp.max(dec, axis=0, keepdims=True))
        kw = kh.astype(_F32) * jnp.exp(dec - m_new)
        carry_scale = jnp.exp(b_last + m_prev - m_new)
        c_ref[h] = carry_scale * c_t + lax.dot_general(kw.astype(_BF16), vh, _TN_DIMS,
                                                       preferred_element_type=_F32)
        n_ref[h] = carry_scale * n_vec + jnp.sum(kw, axis=0, keepdims=True)
        m_ref[h] = m_new

        y = hh * lax.rsqrt(jnp.mean(hh * hh, axis=-1, keepdims=True) + _EPS) * gh_ref[:, h * dv:(h + 1) * dv]
        og = og_ref[:, h * dv:(h + 1) * dv].astype(_F32)
        out_ref[:, h * dv:(h + 1) * dv] = (y * jax.nn.sigmoid(og)).astype(out_ref.dtype)


def _mlstm(proj, gates, b_gate, g_hnorm, batch, seq, heads, dk, dv):
    t = proj.shape[0]
    chunk = _MLSTM_CHUNK
    n_chunks = seq // chunk
    qk_w, v_w = heads * dk, heads * dv
    assert v_w == 2 * qk_w

    def rows(b, c):
        return b * n_chunks + c

    return pl.pallas_call(
        functools.partial(_mlstm_kernel, heads=heads, dk=dk, dv=dv),
        grid=(batch, n_chunks),
        in_specs=[pl.BlockSpec((chunk, qk_w), lambda b, c: (rows(b, c), 0)),
                  pl.BlockSpec((chunk, qk_w), lambda b, c: (rows(b, c), 1)),
                  pl.BlockSpec((chunk, v_w), lambda b, c: (rows(b, c), 1)),
                  pl.BlockSpec((chunk, v_w), lambda b, c: (rows(b, c), 2)),
                  pl.BlockSpec((chunk, _LANES), lambda b, c: (rows(b, c), 0)),
                  pl.BlockSpec((1, _LANES), lambda b, c: (0, 0)),
                  pl.BlockSpec((1, v_w), lambda b, c: (0, 0))],
        out_specs=pl.BlockSpec((chunk, v_w), lambda b, c: (rows(b, c), 0)),
        out_shape=jax.ShapeDtypeStruct((t, v_w), _BF16),
        scratch_shapes=[pltpu.VMEM((heads, dk, dv), _F32),
                        pltpu.VMEM((heads, 1, dk), _F32),
                        pltpu.VMEM((heads, 1, 1), _F32)],
        compiler_params=_params("parallel", "arbitrary"),
        name="mlstm",
    )(proj, proj, proj, proj, gates, b_gate, g_hnorm)


def _qkv_proj_kernel(x_ref, g1_ref, sc1_ref, sh1_ref, gkv_ref, sckv_ref, shkv_ref, wq_ref, wk_ref, wvt_ref,
                     q_ref, k_ref, vt_ref, h1_ref, hk_ref, *, n_q, n_k, q_scale):
    j = pl.program_id(1)

    @pl.when(j == 0)
    def _():
        gain1 = g1_ref[...] * (1.0 + sc1_ref[...])
        shift1 = sh1_ref[...]
        gain_kv = gkv_ref[...] * (1.0 + sckv_ref[...])
        shift_kv = shkv_ref[...]
        n_chunks = x_ref.shape[0] // _NORM_ROWS

        def body(r, carry):
            rows = pl.ds(pl.multiple_of(r * _NORM_ROWS, _NORM_ROWS), _NORM_ROWS)
            x = x_ref[rows, :]
            xn = x * lax.rsqrt(jnp.mean(x * x, axis=-1, keepdims=True) + _EPS)
            h1_ref[rows, :] = (xn * gain1 + shift1).astype(h1_ref.dtype)
            hk_ref[rows, :] = (xn * gain_kv + shift_kv).astype(hk_ref.dtype)
            return carry

        lax.fori_loop(0, n_chunks, body, 0)

    @pl.when(j < n_q)
    def _():
        acc = jnp.dot(h1_ref[...], wq_ref[...], preferred_element_type=_F32)
        q_ref[...] = (acc * q_scale).astype(q_ref.dtype)

    @pl.when(jnp.logical_and(j >= n_q, j < n_q + n_k))
    def _():
        k_ref[...] = jnp.dot(hk_ref[...], wk_ref[...], preferred_element_type=_F32).astype(k_ref.dtype)

    @pl.when(j >= n_q + n_k)
    def _():
        vt_ref[...] = lax.dot_general(wvt_ref[...], hk_ref[...], _NT_DIMS,
                                      preferred_element_type=_F32).astype(vt_ref.dtype)


def _qkv_proj(x, g1, sc1, sh1, gkv, sckv, shkv, wq, wk, wvt, q_scale, batch, seq):
    t, d = x.shape
    n_qc, n_kc, n_vc = wq.shape[1], wk.shape[1], wvt.shape[0]
    tm = min(_PROJ_TM, seq)
    tn = _QKV_TN
    tiles_per_seq = seq // tm
    n_q, n_k, n_v = n_qc // tn, n_kc // tn, n_vc // tn

    def clamp(j, lo, n):
        return jnp.clip(j - lo, 0, n - 1)

    batch_vec = pl.BlockSpec((None, 1, d), lambda i, j: (i // tiles_per_seq, 0, 0))
    row_vec = pl.BlockSpec((1, d), lambda i, j: (0, 0))
    return pl.pallas_call(
        functools.partial(_qkv_proj_kernel, n_q=n_q, n_k=n_k, q_scale=q_scale),
        grid=(t // tm, n_q + n_k + n_v),
        in_specs=[pl.BlockSpec((tm, d), lambda i, j: (i, 0)),
                  row_vec, batch_vec, batch_vec, row_vec, batch_vec, batch_vec,
                  pl.BlockSpec((d, tn), lambda i, j: (0, clamp(j, 0, n_q))),
                  pl.BlockSpec((d, tn), lambda i, j: (0, clamp(j, n_q, n_k))),
                  pl.BlockSpec((tn, d), lambda i, j: (clamp(j, n_q + n_k, n_v), 0))],
        out_specs=[pl.BlockSpec((tm, tn), lambda i, j: (i, clamp(j, 0, n_q))),
                   pl.BlockSpec((tm, tn), lambda i, j: (i, clamp(j, n_q, n_k))),
                   pl.BlockSpec((None, tn, tm), lambda i, j: (i // tiles_per_seq, clamp(j, n_q + n_k, n_v),
                                                              i % tiles_per_seq))],
        out_shape=[jax.ShapeDtypeStruct((t, n_qc), _BF16),
                   jax.ShapeDtypeStruct((t, n_kc), _BF16),
                   jax.ShapeDtypeStruct((batch, n_vc, seq), _BF16)],
        scratch_shapes=[pltpu.VMEM((tm, d), _BF16), pltpu.VMEM((tm, d), _BF16)],
        compiler_params=_params("parallel", "arbitrary"),
        name="qkv_proj",
    )(x, g1, sc1, sh1, gkv, sckv, shkv, wq, wk, wvt)


def _attn_kernel(q_ref, k_ref, vt_ref, slope_ref, lam_ref, g_ref, o_ref, qa_ref, kpos_ref, s_ref, bmax_ref, m_ref,
                 acc_ref, *, dh, lam_init):
    qi = pl.program_id(2)
    blk = q_ref.shape[0]
    slope2 = slope_ref[:, 0:1] * _LOG2E

    dv = vt_ref.shape[0]

    lane = lax.broadcasted_iota(jnp.int32, (blk, dh), 1)
    key_bias = lax.broadcasted_iota(jnp.int32, (blk, dh), 0).astype(_F32) * slope2
    part_hi = key_bias.astype(_BF16).astype(_F32)
    part_mid = (key_bias - part_hi).astype(_BF16).astype(_F32)
    part_lo = key_bias - part_hi - part_mid
    kpos_ref[...] = jnp.where(lane == 0, part_hi, jnp.where(lane == 1, part_mid,
                                                            jnp.where(lane == 2, part_lo, 0.0))).astype(_BF16)
    for s_idx in range(2):
        qa_ref[s_idx, :, :dh] = q_ref[:, s_idx * dh:(s_idx + 1) * dh]
        qa_ref[s_idx, :, dh:] = jnp.where(lane < 3, 1.0, 0.0).astype(_BF16)

    acc_ref[...] = jnp.zeros_like(acc_ref)
    m_ref[...] = jnp.full_like(m_ref, -jnp.inf)

    def scores(kj, masked, buf):
        keys = pl.ds(pl.multiple_of(kj * blk, blk), blk)
        for s_idx in range(2):
            k_cat = jnp.concatenate([k_ref[keys, s_idx * dh:(s_idx + 1) * dh], kpos_ref[...]], axis=1)
            st = lax.dot_general(k_cat, qa_ref[s_idx], _NT_DIMS, preferred_element_type=_F32)
            if masked:
                key = lax.broadcasted_iota(jnp.int32, (blk, blk), 0)
                qry = lax.broadcasted_iota(jnp.int32, (blk, blk), 1)
                st = jnp.where(key <= qry, st, -jnp.inf)
            s_ref[buf, s_idx] = st
            bmax_ref[buf, s_idx] = jnp.max(st, axis=0, keepdims=True)

    def accumulate(kj, buf):
        keys = pl.ds(pl.multiple_of(kj * blk, blk), blk)
        vt_aug = jnp.concatenate([vt_ref[:, keys], jnp.ones((_BF16_ROWS, blk), _BF16)], axis=0)
        block_bias = slope2 * jnp.full((1, 1), (kj - qi) * blk, jnp.int32).astype(_F32)
        for s_idx in range(2):
            m_old = m_ref[s_idx]
            m_new = jnp.maximum(m_old, bmax_ref[buf, s_idx] + block_bias)
            pt = jnp.exp2(s_ref[buf, s_idx] - (m_new - block_bias))
            acc_ref[s_idx] = acc_ref[s_idx] * jnp.exp2(m_old - m_new) + jnp.dot(
                vt_aug, pt.astype(_BF16), preferred_element_type=_F32)
            m_ref[s_idx] = m_new

    def block_of(t):
        return jnp.where(t == 0, qi, t - 1)

    def pair(u, carry):
        t = 2 * u
        scores(t, False, 1)
        accumulate(block_of(t), 0)
        scores(t + 1, False, 0)
        accumulate(t, 1)
        return carry

    scores(qi, True, 0)
    n_pairs = lax.shift_right_logical(qi, 1)
    lax.fori_loop(0, n_pairs, pair, 0)
    t_rest = 2 * n_pairs
    one_more = (qi & 1) == 1

    @pl.when(one_more)
    def _():
        scores(t_rest, False, 1)

    accumulate(block_of(t_rest), 0)

    @pl.when(one_more)
    def _():
        accumulate(t_rest, 1)

    lp = lam_ref[...]
    lam = (jnp.exp(jnp.sum(lp[0:1, :] * lp[1:2, :], axis=-1, keepdims=True))
           - jnp.exp(jnp.sum(lp[2:3, :] * lp[3:4, :], axis=-1, keepdims=True)) + lam_init)
    ot = (acc_ref[0, :dv, :] * (1.0 / acc_ref[0, dv:dv + 1, :])
          - lam * (acc_ref[1, :dv, :] * (1.0 / acc_ref[1, dv:dv + 1, :])))
    ot = ot * lax.rsqrt(jnp.mean(ot * ot, axis=0, keepdims=True) + _EPS)
    o_ref[...] = (ot.T * g_ref[...] * (1.0 - lam_init)).astype(o_ref.dtype)


def _diff_attention(q, k, vt, lam_p, g_subln, batch, seq, heads, dh, lam_init):
    t = q.shape[0]
    dv = 2 * dh
    blk = min(_ATTN_BLOCK, seq)
    n_blk = seq // blk
    slopes = jnp.asarray([2.0 ** (-8.0 * (h + 1) / heads) for h in range(heads)], _F32)
    slopes = jnp.broadcast_to(slopes[:, None, None], (heads, 1, _LANES))
    return pl.pallas_call(
        functools.partial(_attn_kernel, dh=dh, lam_init=lam_init),
        grid=(batch, heads, n_blk),
        in_specs=[pl.BlockSpec((blk, dv), lambda b, h, i: (b * n_blk + i, h)),
                  pl.BlockSpec((seq, dv), lambda b, h, i: (b, h)),
                  pl.BlockSpec((None, dv, seq), lambda b, h, i: (b, h, 0)),
                  pl.BlockSpec((None, 1, _LANES), lambda b, h, i: (h, 0, 0)),
                  pl.BlockSpec((4, dh), lambda b, h, i: (0, 0)),
                  pl.BlockSpec((1, dv), lambda b, h, i: (0, h))],
        out_specs=pl.BlockSpec((blk, dv), lambda b, h, i: (b * n_blk + i, h)),
        out_shape=jax.ShapeDtypeStruct((t, heads * dv), _BF16),
        scratch_shapes=[pltpu.VMEM((2, blk, 2 * dh), _BF16),
                        pltpu.VMEM((blk, dh), _BF16),
                        pltpu.VMEM((2, 2, blk, blk), _F32),
                        pltpu.VMEM((2, 2, 1, blk), _F32),
                        pltpu.VMEM((2, 1, blk), _F32),
                        pltpu.VMEM((2, dv + _BF16_ROWS, blk), _F32)],
        compiler_params=_params("parallel", "parallel", "arbitrary"),
        name="diff_attention",
    )(q, k, vt, slopes, lam_p, g_subln)


def kernel(x, c, w_mod, b_mod, g_norm, w_up, w_down, w_a_in, b_a_gate, g_a_hnorm, w_a_out, g_kv, w_kv_mod,
           b_kv_mod, w_kv, w_b_q, b_lambda, g_b_subln, w_b_out, g_final):
    batch, seq, d = x.shape
    depth = w_mod.shape[0]
    n_a = w_a_in.shape[0]
    m_heads = b_a_gate.shape[1] // 2
    m_dv = g_a_hnorm.shape[1] // m_heads
    m_dk = (w_a_in.shape[2] - 2 * m_heads * m_dv - 2 * m_heads) // (2 * m_heads)
    d_dh = b_lambda.shape[2]
    d_heads = w_b_q.shape[2] // (2 * d_dh)
    t = batch * seq

    xt = x.reshape(t, d)
    c_rows = jnp.zeros((_SUBLANES, d), _F32).at[:batch].set(c)

    mod = _modvec(c_rows, w_mod, b_mod.reshape(depth, 1, 6 * d))[:, :batch]
    kv_mod = _modvec(c_rows, w_kv_mod[None], b_kv_mod.reshape(1, 1, 2 * d))[0, :batch]

    def vec(v):
        return v.reshape(batch, 1, d)

    ones_d = jnp.ones((1, d), _F32)
    for l in range(depth):
        sh1, sc1, gt1, sh2, sc2, gt2 = (vec(m) for m in jnp.split(mod[l], 6, axis=-1))
        if l < n_a:
            qk_w, v_w = m_heads * m_dk, m_heads * m_dv
            main_w = 2 * qk_w + 2 * v_w
            w_in = w_a_in[l]
            col_scale = jnp.concatenate([jnp.ones((qk_w,), _F32), jnp.full((qk_w,), m_dk ** -0.5, _F32),
                                         jnp.ones((2 * v_w,), _F32)]).reshape(1, main_w)
            proj = _norm_mod_matmul(xt, g_norm[l, 0][None], sc1, sh1, w_in[:, :main_w].astype(_BF16),
                                    col_scale, _BF16, seq)
            w_gate = jnp.zeros((d, _LANES), _BF16).at[:, :2 * m_heads].set(w_in[:, main_w:].astype(_BF16))
            gates = _norm_mod_matmul(xt, g_norm[l, 0][None], sc1, sh1, w_gate,
                                     jnp.ones((1, _LANES), _F32), _F32, seq)
            b_gate = jnp.zeros((1, _LANES), _F32).at[0, :2 * m_heads].set(b_a_gate[l])
            mix_in = _mlstm(proj, gates, b_gate, g_a_hnorm[l][None], batch, seq, m_heads, m_dk, m_dv)
            w_out = w_a_out[l]
        else:
            j = l - n_a
            lam_init = 0.8 - 0.6 * math.exp(-0.3 * l)
            q_scale = d_dh ** -0.5 * _LOG2E
            if l == n_a:
                kv_shift, kv_scale = (vec(m) for m in jnp.split(kv_mod, 2, axis=-1))
                k_w = w_b_q.shape[2]
                q, k_sh, vt_sh = _qkv_proj(xt, g_norm[l, 0][None], sc1, sh1, g_kv[None], kv_scale, kv_shift,
                                           w_b_q[j].astype(_BF16), w_kv[:, :k_w].astype(_BF16),
                                           w_kv[:, k_w:].T.astype(_BF16), q_scale, batch, seq)
            else:
                q = _norm_mod_matmul(xt, g_norm[l, 0][None], sc1, sh1, w_b_q[j].astype(_BF16),
                                     jnp.full((1, w_b_q.shape[2]), q_scale, _F32), _BF16, seq)
            mix_in = _diff_attention(q, k_sh, vt_sh, b_lambda[j], g_b_subln[j][None], batch, seq, d_heads, d_dh,
                                     lam_init)
            w_out = w_b_out[j]
        xt = _matmul_residual(mix_in, w_out.astype(_BF16), xt, gt1, seq)
        last = l == depth - 1
        xt = _mlp(xt, g_norm[l, 1][None], sc2, sh2, gt2, w_up[l].astype(_BF16), w_down[l].astype(_BF16),
                  g_final[None] if last else ones_d, seq, last)
    return xt.reshape(batch, seq, d)
```

```python
import functools
import math

import jax
import jax.numpy as jnp
from jax import lax
from jax.experimental import pallas as pl
from jax.experimental.pallas import tpu as pltpu

_EPS = 1e-6
_F32 = jnp.float32
_BF16 = jnp.bfloat16

_LANES = 128
_SUBLANES = 8
_BF16_ROWS = 16
_VMEM_LIMIT_BYTES = 56 * 1024 * 1024

_MODVEC_TN = 1024
_PROJ_TM = 1024
_PROJ_TN = 1024
_MLP_TM = 1024
_MLP_TF = 512
_NORM_ROWS = 16
_NORM_UNROLL = 8
_MLSTM_CHUNK = 128
_ATTN_BLOCK = 512
_QKV_TN = 512

_LOG2E = math.log2(math.e)

_NT_DIMS = (((1,), (1,)), ((), ()))
_TN_DIMS = (((0,), (0,)), ((), ()))


def _params(*semantics):
    return pltpu.CompilerParams(dimension_semantics=semantics, vmem_limit_bytes=_VMEM_LIMIT_BYTES)


def _modvec_kernel(c_ref, w_ref, b_ref, o_ref):
    o_ref[...] = jnp.dot(c_ref[...].astype(_BF16), w_ref[...].astype(_BF16),
                         preferred_element_type=_F32) + b_ref[...]


def _modvec(c_rows, w, b):
    n_l, d, n = w.shape
    tn = _MODVEC_TN
    return pl.pallas_call(
        _modvec_kernel,
        grid=(n_l, n // tn),
        in_specs=[pl.BlockSpec((_SUBLANES, d), lambda l, j: (0, 0)),
                  pl.BlockSpec((None, d, tn), lambda l, j: (l, 0, j)),
                  pl.BlockSpec((None, 1, tn), lambda l, j: (l, 0, j))],
        out_specs=pl.BlockSpec((None, _SUBLANES, tn), lambda l, j: (l, 0, j)),
        out_shape=jax.ShapeDtypeStruct((n_l, _SUBLANES, n), _F32),
        compiler_params=_params("parallel", "parallel"),
        name="modvec",
    )(c_rows, w, b)


def _norm_mod_tile(x_ref, gain, shift, h_ref):
    n_chunks = x_ref.shape[0] // _NORM_ROWS

    def body(r, carry):
        rows = pl.ds(pl.multiple_of(r * _NORM_ROWS, _NORM_ROWS), _NORM_ROWS)
        x = x_ref[rows, :]
        inv = lax.rsqrt(jnp.mean(x * x, axis=-1, keepdims=True) + _EPS)
        h_ref[rows, :] = (x * inv * gain + shift).astype(h_ref.dtype)
        return carry

    lax.fori_loop(0, n_chunks, body, 0, unroll=_NORM_UNROLL)


def _norm_mod_matmul_kernel(x_ref, g_ref, sc_ref, sh_ref, w_ref, cs_ref, o_ref, h_ref):
    @pl.when(pl.program_id(1) == 0)
    def _():
        _norm_mod_tile(x_ref, g_ref[...] * (1.0 + sc_ref[...]), sh_ref[...], h_ref)

    acc = jnp.dot(h_ref[...], w_ref[...], preferred_element_type=_F32)
    o_ref[...] = (acc * cs_ref[...]).astype(o_ref.dtype)


def _norm_mod_matmul(x, g, scale, shift, w, col_scale, out_dtype, seq):
    t, d = x.shape
    n = w.shape[1]
    tm = min(_PROJ_TM, seq)
    tn = min(_PROJ_TN, n)
    tiles_per_seq = seq // tm
    return pl.pallas_call(
        _norm_mod_matmul_kernel,
        grid=(t // tm, n // tn),
        in_specs=[pl.BlockSpec((tm, d), lambda i, j: (i, 0)),
                  pl.BlockSpec((1, d), lambda i, j: (0, 0)),
                  pl.BlockSpec((None, 1, d), lambda i, j: (i // tiles_per_seq, 0, 0)),
                  pl.BlockSpec((None, 1, d), lambda i, j: (i // tiles_per_seq, 0, 0)),
                  pl.BlockSpec((d, tn), lambda i, j: (0, j)),
                  pl.BlockSpec((1, tn), lambda i, j: (0, j))],
        out_specs=pl.BlockSpec((tm, tn), lambda i, j: (i, j)),
        out_shape=jax.ShapeDtypeStruct((t, n), out_dtype),
        scratch_shapes=[pltpu.VMEM((tm, d), _BF16)],
        compiler_params=_params("parallel", "arbitrary"),
        name="norm_mod_matmul",
    )(x, g, scale, shift, w, col_scale)


def _matmul_residual_kernel(a_ref, w_ref, x_ref, gate_ref, o_ref):
    acc = jnp.dot(a_ref[...], w_ref[...], preferred_element_type=_F32)
    o_ref[...] = x_ref[...] + gate_ref[...] * acc


def _matmul_residual(a, w, x, gate, seq):
    t, k = a.shape
    d = w.shape[1]
    tm = min(_PROJ_TM, seq)
    tn = min(_PROJ_TN, d)
    tiles_per_seq = seq // tm
    return pl.pallas_call(
        _matmul_residual_kernel,
        grid=(t // tm, d // tn),
        in_specs=[pl.BlockSpec((tm, k), lambda i, j: (i, 0)),
                  pl.BlockSpec((k, tn), lambda i, j: (0, j)),
                  pl.BlockSpec((tm, tn), lambda i, j: (i, j)),
                  pl.BlockSpec((None, 1, tn), lambda i, j: (i // tiles_per_seq, 0, j))],
        out_specs=pl.BlockSpec((tm, tn), lambda i, j: (i, j)),
        out_shape=jax.ShapeDtypeStruct((t, d), _F32),
        compiler_params=_params("parallel", "parallel"),
        name="matmul_residual",
    )(a, w, x, gate)


def _mlp_kernel(x_ref, g_ref, sc_ref, sh_ref, gate_ref, wu_ref, wd_ref, gf_ref, o_ref, h_ref, *, final_norm):
    f = pl.program_id(1)

    @pl.when(f == 0)
    def _():
        _norm_mod_tile(x_ref, g_ref[...] * (1.0 + sc_ref[...]), sh_ref[...], h_ref)
        o_ref[...] = jnp.zeros_like(o_ref)

    u = jnp.dot(h_ref[...], wu_ref[...], preferred_element_type=_F32)
    u = jnp.square(jnp.maximum(u, 0.0)).astype(_BF16)
    o_ref[...] += jnp.dot(u, wd_ref[...], preferred_element_type=_F32)

    @pl.when(f == pl.num_programs(1) - 1)
    def _():
        gate = gate_ref[...]
        gain = gf_ref[...]
        n_chunks = x_ref.shape[0] // _NORM_ROWS

        def body(r, carry):
            rows = pl.ds(pl.multiple_of(r * _NORM_ROWS, _NORM_ROWS), _NORM_ROWS)
            y = x_ref[rows, :] + gate * o_ref[rows, :]
            if final_norm:
                y = y * lax.rsqrt(jnp.mean(y * y, axis=-1, keepdims=True) + _EPS) * gain
            o_ref[rows, :] = y
            return carry

        lax.fori_loop(0, n_chunks, body, 0, unroll=_NORM_UNROLL)


def _mlp(x, g, scale, shift, gate, w_up, w_down, g_final, seq, final_norm):
    t, d = x.shape
    d_ff = w_up.shape[1]
    tm = min(_MLP_TM, seq)
    tf = min(_MLP_TF, d_ff)
    tiles_per_seq = seq // tm
    batch_vec = pl.BlockSpec((None, 1, d), lambda i, f: (i // tiles_per_seq, 0, 0))
    return pl.pallas_call(
        functools.partial(_mlp_kernel, final_norm=final_norm),
        grid=(t // tm, d_ff // tf),
        in_specs=[pl.BlockSpec((tm, d), lambda i, f: (i, 0)),
                  pl.BlockSpec((1, d), lambda i, f: (0, 0)),
                  batch_vec, batch_vec, batch_vec,
                  pl.BlockSpec((d, tf), lambda i, f: (0, f)),
                  pl.BlockSpec((tf, d), lambda i, f: (f, 0)),
                  pl.BlockSpec((1, d), lambda i, f: (0, 0))],
        out_specs=pl.BlockSpec((tm, d), lambda i, f: (i, 0)),
        out_shape=jax.ShapeDtypeStruct((t, d), _F32),
        scratch_shapes=[pltpu.VMEM((tm, d), _BF16)],
        compiler_params=_params("parallel", "arbitrary"),
        name="mlp",
    )(x, g, scale, shift, gate, w_up, w_down, g_final)


def _log_sigmoid(x):
    return jnp.minimum(x, 0.0) - jnp.log1p(jnp.exp(-jnp.abs(x)))


def _mlstm_kernel(q_ref, k_ref, v_ref, og_ref, gt_ref, bg_ref, gh_ref, out_ref, c_ref, n_ref, m_ref,
                  *, heads, dk, dv):
    @pl.when(pl.program_id(1) == 0)
    def _():
        c_ref[...] = jnp.zeros_like(c_ref)
        n_ref[...] = jnp.zeros_like(n_ref)
        m_ref[...] = jnp.zeros_like(m_ref)

    chunk = q_ref.shape[0]
    gates = gt_ref[...] + bg_ref[...]
    logf = _log_sigmoid(gates)
    row = lax.broadcasted_iota(jnp.int32, (chunk, chunk), 0)
    col = lax.broadcasted_iota(jnp.int32, (chunk, chunk), 1)
    causal = col <= row
    b_all = jnp.dot(causal.astype(_F32), logf, preferred_element_type=_F32, precision=lax.Precision.HIGHEST)
    gates_t = gates.T
    b_all_t = b_all.T

    for h in range(heads):
        li_col = gates[:, h:h + 1]
        b_col = b_all[:, heads + h:heads + h + 1]
        li_row = gates_t[h:h + 1, :]
        b_row = b_all_t[heads + h:heads + h + 1, :]
        m_prev = m_ref[h]
        c_t = c_ref[h]
        n_vec = n_ref[h]
        qh = q_ref[:, h * dk:(h + 1) * dk]
        kh = k_ref[:, h * dk:(h + 1) * dk]
        vh = v_ref[:, h * dv:(h + 1) * dv]

        dmat = jnp.where(causal, b_col - b_row + li_row, -jnp.inf)
        inter = b_col + m_prev
        m_j = jnp.maximum(inter, jnp.max(dmat, axis=-1, keepdims=True))
        w_intra = jnp.exp(dmat - m_j)
        w_inter = jnp.exp(inter - m_j)
        s = lax.dot_general(qh, kh, _NT_DIMS, preferred_element_type=_F32) * w_intra
        num = (w_inter * jnp.dot(qh, c_t.astype(_BF16), preferred_element_type=_F32)
               + jnp.dot(s.astype(_BF16), vh, preferred_element_type=_F32))
        den = (w_inter * jnp.sum(qh.astype(_F32) * n_vec, axis=-1, keepdims=True)
               + jnp.sum(s, axis=-1, keepdims=True))
        hh = num / jnp.maximum(jnp.abs(den), jnp.exp(-m_j))

        b_last = b_col[chunk - 1:chunk, :]
        dec = b_last - b_col + li_col
        m_new = jnp.maximum(b_last + m_prev, jnp.max(dec, axis=0, keepdims=True))
        kw = kh.astype(_F32) * jnp.exp(dec - m_new)
        carry_scale = jnp.exp(b_last + m_prev - m_new)
        c_ref[h] = carry_scale * c_t + lax.dot_general(kw.astype(_BF16), vh, _TN_DIMS,
                                                       preferred_element_type=_F32)
        n_ref[h] = carry_scale * n_vec + jnp.sum(kw, axis=0, keepdims=True)
        m_ref[h] = m_new

        y = hh * lax.rsqrt(jnp.mean(hh * hh, axis=-1, keepdims=True) + _EPS) * gh_ref[:, h * dv:(h + 1) * dv]
        og = og_ref[:, h * dv:(h + 1) * dv].astype(_F32)
        out_ref[:, h * dv:(h + 1) * dv] = (y * jax.nn.sigmoid(og)).astype(out_ref.dtype)


def _mlstm(proj, gates, b_gate, g_hnorm, batch, seq, heads, dk, dv):
    t = proj.shape[0]
    chunk = _MLSTM_CHUNK
    n_chunks = seq // chunk
    qk_w, v_w = heads * dk, heads * dv
    assert v_w == 2 * qk_w

    def rows(b, c):
        return b * n_chunks + c

    return pl.pallas_call(
        functools.partial(_mlstm_kernel, heads=heads, dk=dk, dv=dv),
        grid=(batch, n_chunks),
        in_specs=[pl.BlockSpec((chunk, qk_w), lambda b, c: (rows(b, c), 0)),
                  pl.BlockSpec((chunk, qk_w), lambda b, c: (rows(b, c), 1)),
                  pl.BlockSpec((chunk, v_w), lambda b, c: (rows(b, c), 1)),
                  pl.BlockSpec((chunk, v_w), lambda b, c: (rows(b, c), 2)),
                  pl.BlockSpec((chunk, _LANES), lambda b, c: (rows(b, c), 0)),
                  pl.BlockSpec((1, _LANES), lambda b, c: (0, 0)),
                  pl.BlockSpec((1, v_w), lambda b, c: (0, 0))],
        out_specs=pl.BlockSpec((chunk, v_w), lambda b, c: (rows(b, c), 0)),
        out_shape=jax.ShapeDtypeStruct((t, v_w), _BF16),
        scratch_shapes=[pltpu.VMEM((heads, dk, dv), _F32),
                        pltpu.VMEM((heads, 1, dk), _F32),
                        pltpu.VMEM((heads, 1, 1), _F32)],
        compiler_params=_params("parallel", "arbitrary"),
        name="mlstm",
    )(proj, proj, proj, proj, gates, b_gate, g_hnorm)


def _qkv_proj_kernel(x_ref, g1_ref, sc1_ref, sh1_ref, gkv_ref, sckv_ref, shkv_ref, wq_ref, wk_ref, wvt_ref,
                     q_ref, k_ref, vt_ref, h1_ref, hk_ref, *, n_q, n_k, q_scale):
    j = pl.program_id(1)

    @pl.when(j == 0)
    def _():
        gain1 = g1_ref[...] * (1.0 + sc1_ref[...])
        shift1 = sh1_ref[...]
        gain_kv = gkv_ref[...] * (1.0 + sckv_ref[...])
        shift_kv = shkv_ref[...]
        n_chunks = x_ref.shape[0] // _NORM_ROWS

        def body(r, carry):
            rows = pl.ds(pl.multiple_of(r * _NORM_ROWS, _NORM_ROWS), _NORM_ROWS)
            x = x_ref[rows, :]
            xn = x * lax.rsqrt(jnp.mean(x * x, axis=-1, keepdims=True) + _EPS)
            h1_ref[rows, :] = (xn * gain1 + shift1).astype(h1_ref.dtype)
            hk_ref[rows, :] = (xn * gain_kv + shift_kv).astype(hk_ref.dtype)
            return carry

        lax.fori_loop(0, n_chunks, body, 0, unroll=_NORM_UNROLL)

    @pl.when(j < n_q)
    def _():
        acc = jnp.dot(h1_ref[...], wq_ref[...], preferred_element_type=_F32)
        q_ref[...] = (acc * q_scale).astype(q_ref.dtype)

    @pl.when(jnp.logical_and(j >= n_q, j < n_q + n_k))
    def _():
        k_ref[...] = jnp.dot(hk_ref[...], wk_ref[...], preferred_element_type=_F32).astype(k_ref.dtype)

    @pl.when(j >= n_q + n_k)
    def _():
        vt_ref[...] = lax.dot_general(wvt_ref[...], hk_ref[...], _NT_DIMS,
                                      preferred_element_type=_F32).astype(vt_ref.dtype)


def _qkv_proj(x, g1, sc1, sh1, gkv, sckv, shkv, wq, wk, wvt, q_scale, batch, seq):
    t, d = x.shape
    n_qc, n_kc, n_vc = wq.shape[1], wk.shape[1], wvt.shape[0]
    tm = min(_PROJ_TM, seq)
    tn = _QKV_TN
    tiles_per_seq = seq // tm
    n_q, n_k, n_v = n_qc // tn, n_kc // tn, n_vc // tn

    def clamp(j, lo, n):
        return jnp.clip(j - lo, 0, n - 1)

    batch_vec = pl.BlockSpec((None, 1, d), lambda i, j: (i // tiles_per_seq, 0, 0))
    row_vec = pl.BlockSpec((1, d), lambda i, j: (0, 0))
    return pl.pallas_call(
        functools.partial(_qkv_proj_kernel, n_q=n_q, n_k=n_k, q_scale=q_scale),
        grid=(t // tm, n_q + n_k + n_v),
        in_specs=[pl.BlockSpec((tm, d), lambda i, j: (i, 0)),
                  row_vec, batch_vec, batch_vec, row_vec, batch_vec, batch_vec,
                  pl.BlockSpec((d, tn), lambda i, j: (0, clamp(j, 0, n_q))),
                  pl.BlockSpec((d, tn), lambda i, j: (0, clamp(j, n_q, n_k))),
                  pl.BlockSpec((tn, d), lambda i, j: (clamp(j, n_q + n_k, n_v), 0))],
        out_specs=[pl.BlockSpec((tm, tn), lambda i, j: (i, clamp(j, 0, n_q))),
                   pl.BlockSpec((tm, tn), lambda i, j: (i, clamp(j, n_q, n_k))),
                   pl.BlockSpec((None, tn, tm), lambda i, j: (i // tiles_per_seq, clamp(j, n_q + n_k, n_v),
                                                              i % tiles_per_seq))],
        out_shape=[jax.ShapeDtypeStruct((t, n_qc), _BF16),
                   jax.ShapeDtypeStruct((t, n_kc), _BF16),
                   jax.ShapeDtypeStruct((batch, n_vc, seq), _BF16)],
        scratch_shapes=[pltpu.VMEM((tm, d), _BF16), pltpu.VMEM((tm, d), _BF16)],
        compiler_params=_params("parallel", "arbitrary"),
        name="qkv_proj",
    )(x, g1, sc1, sh1, gkv, sckv, shkv, wq, wk, wvt)


def _attn_kernel(q_ref, k_ref, vt_ref, slope_ref, lam_ref, g_ref, o_ref, qa_ref, kpos_ref, s_ref, bmax_ref, m_ref,
                 acc_ref, *, dh, lam_init):
    qi = pl.program_id(2)
    blk = q_ref.shape[0]
    slope2 = slope_ref[:, 0:1] * _LOG2E

    dv = vt_ref.shape[0]

    @pl.when(qi == 0)
    def _():
        lane = lax.broadcasted_iota(jnp.int32, (blk, dh), 1)
        key_bias = lax.broadcasted_iota(jnp.int32, (blk, dh), 0).astype(_F32) * slope2
        part_hi = key_bias.astype(_BF16).astype(_F32)
        part_mid = (key_bias - part_hi).astype(_BF16).astype(_F32)
        part_lo = key_bias - part_hi - part_mid
        kpos_ref[...] = jnp.where(lane == 0, part_hi, jnp.where(lane == 1, part_mid,
                                                                jnp.where(lane == 2, part_lo, 0.0))).astype(_BF16)
        for s_idx in range(2):
            qa_ref[s_idx, :, dh:] = jnp.where(lane < 3, 1.0, 0.0).astype(_BF16)

    for s_idx in range(2):
        qa_ref[s_idx, :, :dh] = q_ref[:, s_idx * dh:(s_idx + 1) * dh]

    acc_ref[...] = jnp.zeros_like(acc_ref)
    m_ref[...] = jnp.full_like(m_ref, -jnp.inf)

    def scores(kj, masked, buf):
        keys = pl.ds(pl.multiple_of(kj * blk, blk), blk)
        for s_idx in range(2):
            k_cat = jnp.concatenate([k_ref[keys, s_idx * dh:(s_idx + 1) * dh], kpos_ref[...]], axis=1)
            st = lax.dot_general(k_cat, qa_ref[s_idx], _NT_DIMS, preferred_element_type=_F32)
            if masked:
                key = lax.broadcasted_iota(jnp.int32, (blk, blk), 0)
                qry = lax.broadcasted_iota(jnp.int32, (blk, blk), 1)
                st = jnp.where(key <= qry, st, -jnp.inf)
            s_ref[buf, s_idx] = st
            bmax_ref[buf, s_idx] = jnp.max(st, axis=0, keepdims=True)

    def accumulate(kj, buf):
        keys = pl.ds(pl.multiple_of(kj * blk, blk), blk)
        vt_aug = jnp.concatenate([vt_ref[:, keys], jnp.ones((_BF16_ROWS, blk), _BF16)], axis=0)
        block_bias = slope2 * jnp.full((1, 1), (kj - qi) * blk, jnp.int32).astype(_F32)
        for s_idx in range(2):
            m_old = m_ref[s_idx]
            m_new = jnp.maximum(m_old, bmax_ref[buf, s_idx] + block_bias)
            pt = jnp.exp2(s_ref[buf, s_idx] - (m_new - block_bias))
            acc_ref[s_idx] = acc_ref[s_idx] * jnp.exp2(m_old - m_new) + jnp.dot(
                vt_aug, pt.astype(_BF16), preferred_element_type=_F32)
            m_ref[s_idx] = m_new

    def block_of(t):
        return jnp.where(t == 0, qi, t - 1)

    def pair(u, carry):
        t = 2 * u
        scores(t, False, 1)
        accumulate(block_of(t), 0)
        scores(t + 1, False, 0)
        accumulate(t, 1)
        return carry

    scores(qi, True, 0)
    n_pairs = lax.shift_right_logical(qi, 1)
    lax.fori_loop(0, n_pairs, pair, 0)
    t_rest = 2 * n_pairs
    one_more = (qi & 1) == 1

    @pl.when(one_more)
    def _():
        scores(t_rest, False, 1)
        accumulate(block_of(t_rest), 0)
        accumulate(t_rest, 1)

    @pl.when(jnp.logical_not(one_more))
    def _():
        accumulate(block_of(t_rest), 0)

    lp = lam_ref[...]
    lam = (jnp.exp(jnp.sum(lp[0:1, :] * lp[1:2, :], axis=-1, keepdims=True))
           - jnp.exp(jnp.sum(lp[2:3, :] * lp[3:4, :], axis=-1, keepdims=True)) + lam_init)
    ot = (acc_ref[0, :dv, :] * (1.0 / acc_ref[0, dv:dv + 1, :])
          - lam * (acc_ref[1, :dv, :] * (1.0 / acc_ref[1, dv:dv + 1, :])))
    ot = ot * lax.rsqrt(jnp.mean(ot * ot, axis=0, keepdims=True) + _EPS)
    o_ref[...] = (ot.T * g_ref[...] * (1.0 - lam_init)).astype(o_ref.dtype)


def _diff_attention(q, k, vt, lam_p, g_subln, batch, seq, heads, dh, lam_init):
    t = q.shape[0]
    dv = 2 * dh
    blk = min(_ATTN_BLOCK, seq)
    n_blk = seq // blk
    slopes = jnp.asarray([2.0 ** (-8.0 * (h + 1) / heads) for h in range(heads)], _F32)
    slopes = jnp.broadcast_to(slopes[:, None, None], (heads, 1, _LANES))
    return pl.pallas_call(
        functools.partial(_attn_kernel, dh=dh, lam_init=lam_init),
        grid=(batch, heads, n_blk),
        in_specs=[pl.BlockSpec((blk, dv), lambda b, h, i: (b * n_blk + i, h)),
                  pl.BlockSpec((seq, dv), lambda b, h, i: (b, h)),
                  pl.BlockSpec((None, dv, seq), lambda b, h, i: (b, h, 0)),
                  pl.BlockSpec((None, 1, _LANES), lambda b, h, i: (h, 0, 0)),
                  pl.BlockSpec((4, dh), lambda b, h, i: (0, 0)),
                  pl.BlockSpec((1, dv), lambda b, h, i: (0, h))],
        out_specs=pl.BlockSpec((blk, dv), lambda b, h, i: (b * n_blk + i, h)),
        out_shape=jax.ShapeDtypeStruct((t, heads * dv), _BF16),
        scratch_shapes=[pltpu.VMEM((2, blk, 2 * dh), _BF16),
                        pltpu.VMEM((blk, dh), _BF16),
                        pltpu.VMEM((2, 2, blk, blk), _F32),
                        pltpu.VMEM((2, 2, 1, blk), _F32),
                        pltpu.VMEM((2, 1, blk), _F32),
                        pltpu.VMEM((2, dv + _BF16_ROWS, blk), _F32)],
        compiler_params=_params("parallel", "parallel", "arbitrary"),
        name="diff_attention",
    )(q, k, vt, slopes, lam_p, g_subln)


def kernel(x, c, w_mod, b_mod, g_norm, w_up, w_down, w_a_in, b_a_gate, g_a_hnorm, w_a_out, g_kv, w_kv_mod,
           b_kv_mod, w_kv, w_b_q, b_lambda, g_b_subln, w_b_out, g_final):
    batch, seq, d = x.shape
    depth = w_mod.shape[0]
    n_a = w_a_in.shape[0]
    m_heads = b_a_gate.shape[1] // 2
    m_dv = g_a_hnorm.shape[1] // m_heads
    m_dk = (w_a_in.shape[2] - 2 * m_heads * m_dv - 2 * m_heads) // (2 * m_heads)
    d_dh = b_lambda.shape[2]
    d_heads = w_b_q.shape[2] // (2 * d_dh)
    t = batch * seq

    xt = x.reshape(t, d)
    c_rows = jnp.zeros((_SUBLANES, d), _F32).at[:batch].set(c)

    mod = _modvec(c_rows, w_mod, b_mod.reshape(depth, 1, 6 * d))[:, :batch]
    kv_mod = _modvec(c_rows, w_kv_mod[None], b_kv_mod.reshape(1, 1, 2 * d))[0, :batch]

    def vec(v):
        return v.reshape(batch, 1, d)

    ones_d = jnp.ones((1, d), _F32)
    for l in range(depth):
        sh1, sc1, gt1, sh2, sc2, gt2 = (vec(m) for m in jnp.split(mod[l], 6, axis=-1))
        if l < n_a:
            qk_w, v_w = m_heads * m_dk, m_heads * m_dv
            main_w = 2 * qk_w + 2 * v_w
            w_in = w_a_in[l]
            col_scale = jnp.concatenate([jnp.ones((qk_w,), _F32), jnp.full((qk_w,), m_dk ** -0.5, _F32),
                                         jnp.ones((2 * v_w,), _F32)]).reshape(1, main_w)
            proj = _norm_mod_matmul(xt, g_norm[l, 0][None], sc1, sh1, w_in[:, :main_w].astype(_BF16),
                                    col_scale, _BF16, seq)
            w_gate = jnp.zeros((d, _LANES), _BF16).at[:, :2 * m_heads].set(w_in[:, main_w:].astype(_BF16))
            gates = _norm_mod_matmul(xt, g_norm[l, 0][None], sc1, sh1, w_gate,
                                     jnp.ones((1, _LANES), _F32), _F32, seq)
            b_gate = jnp.zeros((1, _LANES), _F32).at[0, :2 * m_heads].set(b_a_gate[l])
            mix_in = _mlstm(proj, gates, b_gate, g_a_hnorm[l][None], batch, seq, m_heads, m_dk, m_dv)
            w_out = w_a_out[l]
        else:
            j = l - n_a
            lam_init = 0.8 - 0.6 * math.exp(-0.3 * l)
            q_scale = d_dh ** -0.5 * _LOG2E
            if l == n_a:
                kv_shift, kv_scale = (vec(m) for m in jnp.split(kv_mod, 2, axis=-1))
                k_w = w_b_q.shape[2]
                q, k_sh, vt_sh = _qkv_proj(xt, g_norm[l, 0][None], sc1, sh1, g_kv[None], kv_scale, kv_shift,
                                           w_b_q[j].astype(_BF16), w_kv[:, :k_w].astype(_BF16),
                                           w_kv[:, k_w:].T.astype(_BF16), q_scale, batch, seq)
            else:
                q = _norm_mod_matmul(xt, g_norm[l, 0][None], sc1, sh1, w_b_q[j].astype(_BF16),
                                     jnp.full((1, w_b_q.shape[2]), q_scale, _F32), _BF16, seq)
            mix_in = _diff_attention(q, k_sh, vt_sh, b_lambda[j], g_b_subln[j][None], batch, seq, d_heads, d_dh,
                                     lam_init)
            w_out = w_b_out[j]
        xt = _matmul_residual(mix_in, w_out.astype(_BF16), xt, gt1, seq)
        last = l == depth - 1
        xt = _mlp(xt, g_norm[l, 1][None], sc2, sh2, gt2, w_up[l].astype(_BF16), w_down[l].astype(_BF16),
                  g_final[None] if last else ones_d, seq, last)
    return xt.reshape(batch, seq, d)
```

```python
import functools
import math

import jax
import jax.numpy as jnp
from jax import lax
from jax.experimental import pallas as pl
from jax.experimental.pallas import tpu as pltpu

_EPS = 1e-6
_F32 = jnp.float32
_BF16 = jnp.bfloat16

_LANES = 128
_SUBLANES = 8
_BF16_ROWS = 16
_VMEM_LIMIT_BYTES = 56 * 1024 * 1024

_MODVEC_TN = 1024
_PROJ_TM = 1024
_PROJ_TN = 1024
_MLP_TM = 1024
_MLP_TF = 512
_NORM_ROWS = 16
_NORM_UNROLL = 8
_MLSTM_CHUNK = 128
_ATTN_BLOCK = 512
_QKV_TN = 512

_LOG2E = math.log2(math.e)

_NT_DIMS = (((1,), (1,)), ((), ()))
_TN_DIMS = (((0,), (0,)), ((), ()))


def _params(*semantics):
    return pltpu.CompilerParams(dimension_semantics=semantics, vmem_limit_bytes=_VMEM_LIMIT_BYTES)


def _modvec_kernel(c_ref, w_ref, b_ref, o_ref):
    o_ref[...] = jnp.dot(c_ref[...].astype(_BF16), w_ref[...].astype(_BF16),
                         preferred_element_type=_F32) + b_ref[...]


def _modvec(c_rows, w, b):
    n_l, d, n = w.shape
    tn = _MODVEC_TN
    return pl.pallas_call(
        _modvec_kernel,
        grid=(n_l, n // tn),
        in_specs=[pl.BlockSpec((_SUBLANES, d), lambda l, j: (0, 0)),
                  pl.BlockSpec((None, d, tn), lambda l, j: (l, 0, j)),
                  pl.BlockSpec((None, 1, tn), lambda l, j: (l, 0, j))],
        out_specs=pl.BlockSpec((None, _SUBLANES, tn), lambda l, j: (l, 0, j)),
        out_shape=jax.ShapeDtypeStruct((n_l, _SUBLANES, n), _F32),
        compiler_params=_params("parallel", "parallel"),
        name="modvec",
    )(c_rows, w, b)


def _norm_mod_tile(x_ref, gain, shift, h_ref):
    n_chunks = x_ref.shape[0] // _NORM_ROWS

    def body(r, carry):
        rows = pl.ds(pl.multiple_of(r * _NORM_ROWS, _NORM_ROWS), _NORM_ROWS)
        x = x_ref[rows, :]
        inv = lax.rsqrt(jnp.mean(x * x, axis=-1, keepdims=True) + _EPS)
        h_ref[rows, :] = (x * inv * gain + shift).astype(h_ref.dtype)
        return carry

    lax.fori_loop(0, n_chunks, body, 0, unroll=_NORM_UNROLL)


def _norm_mod_matmul_kernel(x_ref, g_ref, sc_ref, sh_ref, w_ref, cs_ref, *rest, narrow):
    if narrow:
        wn_ref, o_ref, on_ref, h_ref = rest
    else:
        o_ref, h_ref = rest

    @pl.when(pl.program_id(1) == 0)
    def _():
        _norm_mod_tile(x_ref, g_ref[...] * (1.0 + sc_ref[...]), sh_ref[...], h_ref)
        if narrow:
            on_ref[...] = jnp.dot(h_ref[...], wn_ref[...], preferred_element_type=_F32)

    acc = jnp.dot(h_ref[...], w_ref[...], preferred_element_type=_F32)
    o_ref[...] = (acc * cs_ref[...]).astype(o_ref.dtype)


def _norm_mod_matmul(x, g, scale, shift, w, col_scale, seq, w_narrow=None):
    t, d = x.shape
    n = w.shape[1]
    tm = min(_PROJ_TM, seq)
    tn = min(_PROJ_TN, n)
    tiles_per_seq = seq // tm
    narrow = w_narrow is not None
    batch_vec = pl.BlockSpec((None, 1, d), lambda i, j: (i // tiles_per_seq, 0, 0))
    in_specs = [pl.BlockSpec((tm, d), lambda i, j: (i, 0)),
                pl.BlockSpec((1, d), lambda i, j: (0, 0)),
                batch_vec, batch_vec,
                pl.BlockSpec((d, tn), lambda i, j: (0, j)),
                pl.BlockSpec((1, tn), lambda i, j: (0, j))]
    out_specs = [pl.BlockSpec((tm, tn), lambda i, j: (i, j))]
    out_shape = [jax.ShapeDtypeStruct((t, n), _BF16)]
    operands = [x, g, scale, shift, w, col_scale]
    if narrow:
        in_specs.append(pl.BlockSpec((d, _LANES), lambda i, j: (0, 0)))
        out_specs.append(pl.BlockSpec((tm, _LANES), lambda i, j: (i, 0)))
        out_shape.append(jax.ShapeDtypeStruct((t, _LANES), _F32))
        operands.append(w_narrow)
    outs = pl.pallas_call(
        functools.partial(_norm_mod_matmul_kernel, narrow=narrow),
        grid=(t // tm, n // tn),
        in_specs=in_specs,
        out_specs=out_specs,
        out_shape=out_shape,
        scratch_shapes=[pltpu.VMEM((tm, d), _BF16)],
        compiler_params=_params("parallel", "arbitrary"),
        name="norm_mod_matmul",
    )(*operands)
    return outs if narrow else outs[0]


def _matmul_residual_kernel(a_ref, w_ref, x_ref, gate_ref, o_ref):
    acc = jnp.dot(a_ref[...], w_ref[...], preferred_element_type=_F32)
    o_ref[...] = x_ref[...] + gate_ref[...] * acc


def _matmul_residual(a, w, x, gate, seq):
    t, k = a.shape
    d = w.shape[1]
    tm = min(_PROJ_TM, seq)
    tn = min(_PROJ_TN, d)
    tiles_per_seq = seq // tm
    return pl.pallas_call(
        _matmul_residual_kernel,
        grid=(t // tm, d // tn),
        in_specs=[pl.BlockSpec((tm, k), lambda i, j: (i, 0)),
                  pl.BlockSpec((k, tn), lambda i, j: (0, j)),
                  pl.BlockSpec((tm, tn), lambda i, j: (i, j)),
                  pl.BlockSpec((None, 1, tn), lambda i, j: (i // tiles_per_seq, 0, j))],
        out_specs=pl.BlockSpec((tm, tn), lambda i, j: (i, j)),
        out_shape=jax.ShapeDtypeStruct((t, d), _F32),
        compiler_params=_params("parallel", "parallel"),
        name="matmul_residual",
    )(a, w, x, gate)


def _mlp_kernel(x_ref, g_ref, sc_ref, sh_ref, gate_ref, wu_ref, wd_ref, gf_ref, o_ref, h_ref, *, final_norm):
    f = pl.program_id(1)

    @pl.when(f == 0)
    def _():
        _norm_mod_tile(x_ref, g_ref[...] * (1.0 + sc_ref[...]), sh_ref[...], h_ref)
        o_ref[...] = jnp.zeros_like(o_ref)

    u = jnp.dot(h_ref[...], wu_ref[...], preferred_element_type=_F32)
    u = jnp.square(jnp.maximum(u, 0.0)).astype(_BF16)
    o_ref[...] += jnp.dot(u, wd_ref[...], preferred_element_type=_F32)

    @pl.when(f == pl.num_programs(1) - 1)
    def _():
        gate = gate_ref[...]
        gain = gf_ref[...]
        n_chunks = x_ref.shape[0] // _NORM_ROWS

        def body(r, carry):
            rows = pl.ds(pl.multiple_of(r * _NORM_ROWS, _NORM_ROWS), _NORM_ROWS)
            y = x_ref[rows, :] + gate * o_ref[rows, :]
            if final_norm:
                y = y * lax.rsqrt(jnp.mean(y * y, axis=-1, keepdims=True) + _EPS) * gain
            o_ref[rows, :] = y
            return carry

        lax.fori_loop(0, n_chunks, body, 0, unroll=_NORM_UNROLL)


def _mlp(x, g, scale, shift, gate, w_up, w_down, g_final, seq, final_norm):
    t, d = x.shape
    d_ff = w_up.shape[1]
    tm = min(_MLP_TM, seq)
    tf = min(_MLP_TF, d_ff)
    tiles_per_seq = seq // tm
    batch_vec = pl.BlockSpec((None, 1, d), lambda i, f: (i // tiles_per_seq, 0, 0))
    return pl.pallas_call(
        functools.partial(_mlp_kernel, final_norm=final_norm),
        grid=(t // tm, d_ff // tf),
        in_specs=[pl.BlockSpec((tm, d), lambda i, f: (i, 0)),
                  pl.BlockSpec((1, d), lambda i, f: (0, 0)),
                  batch_vec, batch_vec, batch_vec,
                  pl.BlockSpec((d, tf), lambda i, f: (0, f)),
                  pl.BlockSpec((tf, d), lambda i, f: (f, 0)),
                  pl.BlockSpec((1, d), lambda i, f: (0, 0))],
        out_specs=pl.BlockSpec((tm, d), lambda i, f: (i, 0)),
        out_shape=jax.ShapeDtypeStruct((t, d), _F32),
        scratch_shapes=[pltpu.VMEM((tm, d), _BF16)],
        compiler_params=_params("parallel", "arbitrary"),
        name="mlp",
    )(x, g, scale, shift, gate, w_up, w_down, g_final)


def _log_sigmoid(x):
    return jnp.minimum(x, 0.0) - jnp.log1p(jnp.exp(-jnp.abs(x)))


def _mlstm_kernel(q_ref, k_ref, v_ref, og_ref, gt_ref, bg_ref, gh_ref, out_ref, c_ref, m_ref,
                  *, heads, dk, dv):
    @pl.when(pl.program_id(1) == 0)
    def _():
        c_ref[...] = jnp.zeros_like(c_ref)
        m_ref[...] = jnp.zeros_like(m_ref)

    chunk = q_ref.shape[0]
    gates = gt_ref[...] + bg_ref[...]
    logf = _log_sigmoid(gates)
    src = lax.broadcasted_iota(jnp.int32, (chunk, chunk), 0)
    dst = lax.broadcasted_iota(jnp.int32, (chunk, chunk), 1)
    b_all = jnp.dot((dst <= src).astype(_F32), logf, preferred_element_type=_F32,
                    precision=lax.Precision.HIGHEST)
    gates_t = gates.T
    b_all_t = b_all.T
    visible = src <= dst

    for h in range(heads):
        b_row = b_all_t[heads + h:heads + h + 1, :]
        src_col = gates[:, h:h + 1] - b_all[:, heads + h:heads + h + 1]
        b_last = b_row[:, chunk - 1:chunk]
        m_prev = m_ref[h]
        c_aug = c_ref[h]
        qh = q_ref[:, h * dk:(h + 1) * dk]
        kh = k_ref[:, h * dk:(h + 1) * dk]
        vh = v_ref[:, h * dv:(h + 1) * dv]

        dmat_t = jnp.where(visible, b_row + src_col, -jnp.inf)
        inter = b_row + m_prev
        m_j = jnp.maximum(inter, jnp.max(dmat_t, axis=0, keepdims=True))
        w_inter = jnp.exp(inter - m_j)
        s_t = lax.dot_general(kh, qh, _NT_DIMS, preferred_element_type=_F32) * jnp.exp(dmat_t - m_j)
        state_q = lax.dot_general(c_aug.astype(_BF16), qh, _NT_DIMS, preferred_element_type=_F32)
        num_t = (w_inter * state_q[:dv, :]
                 + lax.dot_general(vh, s_t.astype(_BF16), _TN_DIMS, preferred_element_type=_F32))
        den = w_inter * state_q[dv:dv + 1, :] + jnp.sum(s_t, axis=0, keepdims=True)
        h_t = num_t * (1.0 / jnp.maximum(jnp.abs(den), jnp.exp(-m_j)))

        dec = b_last + src_col
        m_new = jnp.maximum(b_last + m_prev, jnp.max(dec, axis=0, keepdims=True))
        kw = kh.astype(_F32) * jnp.exp(dec - m_new)
        carry_scale = jnp.exp(b_last + m_prev - m_new)
        c_ref[h, :dv, :] = carry_scale * c_aug[:dv, :] + lax.dot_general(
            vh, kw.astype(_BF16), _TN_DIMS, preferred_element_type=_F32)
        c_ref[h, dv:dv + 1, :] = carry_scale * c_aug[dv:dv + 1, :] + jnp.sum(kw, axis=0, keepdims=True)
        m_ref[h] = m_new

        y_t = h_t * lax.rsqrt(jnp.mean(h_t * h_t, axis=0, keepdims=True) + _EPS)
        og = og_ref[:, h * dv:(h + 1) * dv].astype(_F32)
        out_ref[:, h * dv:(h + 1) * dv] = (y_t.T * gh_ref[:, h * dv:(h + 1) * dv]
                                           * jax.nn.sigmoid(og)).astype(out_ref.dtype)


def _mlstm(proj, gates, b_gate, g_hnorm, batch, seq, heads, dk, dv):
    t = proj.shape[0]
    chunk = _MLSTM_CHUNK
    n_chunks = seq // chunk
    qk_w, v_w = heads * dk, heads * dv
    assert v_w == 2 * qk_w

    def rows(b, c):
        return b * n_chunks + c

    return pl.pallas_call(
        functools.partial(_mlstm_kernel, heads=heads, dk=dk, dv=dv),
        grid=(batch, n_chunks),
        in_specs=[pl.BlockSpec((chunk, qk_w), lambda b, c: (rows(b, c), 0)),
                  pl.BlockSpec((chunk, qk_w), lambda b, c: (rows(b, c), 1)),
                  pl.BlockSpec((chunk, v_w), lambda b, c: (rows(b, c), 1)),
                  pl.BlockSpec((chunk, v_w), lambda b, c: (rows(b, c), 2)),
                  pl.BlockSpec((chunk, _LANES), lambda b, c: (rows(b, c), 0)),
                  pl.BlockSpec((1, _LANES), lambda b, c: (0, 0)),
                  pl.BlockSpec((1, v_w), lambda b, c: (0, 0))],
        out_specs=pl.BlockSpec((chunk, v_w), lambda b, c: (rows(b, c), 0)),
        out_shape=jax.ShapeDtypeStruct((t, v_w), _BF16),
        scratch_shapes=[pltpu.VMEM((heads, dv + _BF16_ROWS, dk), _F32),
                        pltpu.VMEM((heads, 1, 1), _F32)],
        compiler_params=_params("parallel", "arbitrary"),
        name="mlstm",
    )(proj, proj, proj, proj, gates, b_gate, g_hnorm)


def _qkv_proj_kernel(x_ref, g1_ref, sc1_ref, sh1_ref, gkv_ref, sckv_ref, shkv_ref, wq_ref, wk_ref, wvt_ref,
                     q_ref, k_ref, vt_ref, h1_ref, hk_ref, *, n_q, n_k, q_scale):
    j = pl.program_id(1)

    @pl.when(j == 0)
    def _():
        gain1 = g1_ref[...] * (1.0 + sc1_ref[...])
        shift1 = sh1_ref[...]
        gain_kv = gkv_ref[...] * (1.0 + sckv_ref[...])
        shift_kv = shkv_ref[...]
        n_chunks = x_ref.shape[0] // _NORM_ROWS

        def body(r, carry):
            rows = pl.ds(pl.multiple_of(r * _NORM_ROWS, _NORM_ROWS), _NORM_ROWS)
            x = x_ref[rows, :]
            xn = x * lax.rsqrt(jnp.mean(x * x, axis=-1, keepdims=True) + _EPS)
            h1_ref[rows, :] = (xn * gain1 + shift1).astype(h1_ref.dtype)
            hk_ref[rows, :] = (xn * gain_kv + shift_kv).astype(hk_ref.dtype)
            return carry

        lax.fori_loop(0, n_chunks, body, 0, unroll=_NORM_UNROLL)

    @pl.when(j < n_q)
    def _():
        acc = jnp.dot(h1_ref[...], wq_ref[...], preferred_element_type=_F32)
        q_ref[...] = (acc * q_scale).astype(q_ref.dtype)

    @pl.when(jnp.logical_and(j >= n_q, j < n_q + n_k))
    def _():
        k_ref[...] = jnp.dot(hk_ref[...], wk_ref[...], preferred_element_type=_F32).astype(k_ref.dtype)

    @pl.when(j >= n_q + n_k)
    def _():
        vt_ref[...] = lax.dot_general(wvt_ref[...], hk_ref[...], _NT_DIMS,
                                      preferred_element_type=_F32).astype(vt_ref.dtype)


def _qkv_proj(x, g1, sc1, sh1, gkv, sckv, shkv, wq, wk, wvt, q_scale, batch, seq):
    t, d = x.shape
    n_qc, n_kc, n_vc = wq.shape[1], wk.shape[1], wvt.shape[0]
    tm = min(_PROJ_TM, seq)
    tn = _QKV_TN
    tiles_per_seq = seq // tm
    n_q, n_k, n_v = n_qc // tn, n_kc // tn, n_vc // tn

    def clamp(j, lo, n):
        return jnp.clip(j - lo, 0, n - 1)

    batch_vec = pl.BlockSpec((None, 1, d), lambda i, j: (i // tiles_per_seq, 0, 0))
    row_vec = pl.BlockSpec((1, d), lambda i, j: (0, 0))
    return pl.pallas_call(
        functools.partial(_qkv_proj_kernel, n_q=n_q, n_k=n_k, q_scale=q_scale),
        grid=(t // tm, n_q + n_k + n_v),
        in_specs=[pl.BlockSpec((tm, d), lambda i, j: (i, 0)),
                  row_vec, batch_vec, batch_vec, row_vec, batch_vec, batch_vec,
                  pl.BlockSpec((d, tn), lambda i, j: (0, clamp(j, 0, n_q))),
                  pl.BlockSpec((d, tn), lambda i, j: (0, clamp(j, n_q, n_k))),
                  pl.BlockSpec((tn, d), lambda i, j: (clamp(j, n_q + n_k, n_v), 0))],
        out_specs=[pl.BlockSpec((tm, tn), lambda i, j: (i, clamp(j, 0, n_q))),
                   pl.BlockSpec((tm, tn), lambda i, j: (i, clamp(j, n_q, n_k))),
                   pl.BlockSpec((None, tn, tm), lambda i, j: (i // tiles_per_seq, clamp(j, n_q + n_k, n_v),
                                                              i % tiles_per_seq))],
        out_shape=[jax.ShapeDtypeStruct((t, n_qc), _BF16),
                   jax.ShapeDtypeStruct((t, n_kc), _BF16),
                   jax.ShapeDtypeStruct((batch, n_vc, seq), _BF16)],
        scratch_shapes=[pltpu.VMEM((tm, d), _BF16), pltpu.VMEM((tm, d), _BF16)],
        compiler_params=_params("parallel", "arbitrary"),
        name="qkv_proj",
    )(x, g1, sc1, sh1, gkv, sckv, shkv, wq, wk, wvt)


def _attn_kernel(q_ref, k_ref, vt_ref, slope_ref, lam_ref, g_ref, o_ref, qa_ref, kpos_ref, s_ref, bmax_ref, m_ref,
                 acc_ref, *, dh, lam_init):
    qi = pl.program_id(2)
    blk = q_ref.shape[0]
    slope2 = slope_ref[:, 0:1] * _LOG2E

    dv = vt_ref.shape[0]

    @pl.when(qi == 0)
    def _():
        lane = lax.broadcasted_iota(jnp.int32, (blk, dh), 1)
        key_bias = lax.broadcasted_iota(jnp.int32, (blk, dh), 0).astype(_F32) * slope2
        part_hi = key_bias.astype(_BF16).astype(_F32)
        part_mid = (key_bias - part_hi).astype(_BF16).astype(_F32)
        part_lo = key_bias - part_hi - part_mid
        kpos_ref[...] = jnp.where(lane == 0, part_hi, jnp.where(lane == 1, part_mid,
                                                                jnp.where(lane == 2, part_lo, 0.0))).astype(_BF16)
        for s_idx in range(2):
            qa_ref[s_idx, :, dh:] = jnp.where(lane < 3, 1.0, 0.0).astype(_BF16)

    for s_idx in range(2):
        qa_ref[s_idx, :, :dh] = q_ref[:, s_idx * dh:(s_idx + 1) * dh]

    acc_ref[...] = jnp.zeros_like(acc_ref)
    m_ref[...] = jnp.full_like(m_ref, -jnp.inf)

    def scores(kj, masked, buf):
        keys = pl.ds(pl.multiple_of(kj * blk, blk), blk)
        for s_idx in range(2):
            k_cat = jnp.concatenate([k_ref[keys, s_idx * dh:(s_idx + 1) * dh], kpos_ref[...]], axis=1)
            st = lax.dot_general(k_cat, qa_ref[s_idx], _NT_DIMS, preferred_element_type=_F32)
            if masked:
                key = lax.broadcasted_iota(jnp.int32, (blk, blk), 0)
                qry = lax.broadcasted_iota(jnp.int32, (blk, blk), 1)
                st = jnp.where(key <= qry, st, -jnp.inf)
            s_ref[buf, s_idx] = st
            bmax_ref[buf, s_idx] = jnp.max(st, axis=0, keepdims=True)

    def accumulate(kj, buf):
        keys = pl.ds(pl.multiple_of(kj * blk, blk), blk)
        vt_aug = jnp.concatenate([vt_ref[:, keys], jnp.ones((_BF16_ROWS, blk), _BF16)], axis=0)
        block_bias = slope2 * jnp.full((1, 1), (kj - qi) * blk, jnp.int32).astype(_F32)
        for s_idx in range(2):
            m_old = m_ref[s_idx]
            m_new = jnp.maximum(m_old, bmax_ref[buf, s_idx] + block_bias)
            pt = jnp.exp2(s_ref[buf, s_idx] - (m_new - block_bias))
            acc_ref[s_idx] = acc_ref[s_idx] * jnp.exp2(m_old - m_new) + jnp.dot(
                vt_aug, pt.astype(_BF16), preferred_element_type=_F32)
            m_ref[s_idx] = m_new

    def block_of(t):
        return jnp.where(t == 0, qi, t - 1)

    def pair(u):
        t = 2 * u
        scores(t, False, 1)
        accumulate(block_of(t), 0)
        scores(t + 1, False, 0)
        accumulate(t, 1)

    def two_pairs(u, carry):
        pair(2 * u)
        pair(2 * u + 1)
        return carry

    scores(qi, True, 0)
    n_pairs = lax.shift_right_logical(qi, 1)
    lax.fori_loop(0, lax.shift_right_logical(n_pairs, 1), two_pairs, 0)

    @pl.when((n_pairs & 1) == 1)
    def _():
        pair(n_pairs - 1)

    t_rest = 2 * n_pairs
    one_more = (qi & 1) == 1

    @pl.when(one_more)
    def _():
        scores(t_rest, False, 1)
        accumulate(block_of(t_rest), 0)
        accumulate(t_rest, 1)

    @pl.when(jnp.logical_not(one_more))
    def _():
        accumulate(block_of(t_rest), 0)

    lp = lam_ref[...]
    lam = (jnp.exp(jnp.sum(lp[0:1, :] * lp[1:2, :], axis=-1, keepdims=True))
           - jnp.exp(jnp.sum(lp[2:3, :] * lp[3:4, :], axis=-1, keepdims=True)) + lam_init)
    ot = (acc_ref[0, :dv, :] * (1.0 / acc_ref[0, dv:dv + 1, :])
          - lam * (acc_ref[1, :dv, :] * (1.0 / acc_ref[1, dv:dv + 1, :])))
    ot = ot * lax.rsqrt(jnp.mean(ot * ot, axis=0, keepdims=True) + _EPS)
    o_ref[...] = (ot.T * g_ref[...] * (1.0 - lam_init)).astype(o_ref.dtype)


def _diff_attention(q, k, vt, lam_p, g_subln, batch, seq, heads, dh, lam_init):
    t = q.shape[0]
    dv = 2 * dh
    blk = min(_ATTN_BLOCK, seq)
    n_blk = seq // blk
    slopes = jnp.asarray([2.0 ** (-8.0 * (h + 1) / heads) for h in range(heads)], _F32)
    slopes = jnp.broadcast_to(slopes[:, None, None], (heads, 1, _LANES))
    return pl.pallas_call(
        functools.partial(_attn_kernel, dh=dh, lam_init=lam_init),
        grid=(batch, heads, n_blk),
        in_specs=[pl.BlockSpec((blk, dv), lambda b, h, i: (b * n_blk + i, h)),
                  pl.BlockSpec((seq, dv), lambda b, h, i: (b, h)),
                  pl.BlockSpec((None, dv, seq), lambda b, h, i: (b, h, 0)),
                  pl.BlockSpec((None, 1, _LANES), lambda b, h, i: (h, 0, 0)),
                  pl.BlockSpec((4, dh), lambda b, h, i: (0, 0)),
                  pl.BlockSpec((1, dv), lambda b, h, i: (0, h))],
        out_specs=pl.BlockSpec((blk, dv), lambda b, h, i: (b * n_blk + i, h)),
        out_shape=jax.ShapeDtypeStruct((t, heads * dv), _BF16),
        scratch_shapes=[pltpu.VMEM((2, blk, 2 * dh), _BF16),
                        pltpu.VMEM((blk, dh), _BF16),
                        pltpu.VMEM((2, 2, blk, blk), _F32),
                        pltpu.VMEM((2, 2, 1, blk), _F32),
                        pltpu.VMEM((2, 1, blk), _F32),
                        pltpu.VMEM((2, dv + _BF16_ROWS, blk), _F32)],
        compiler_params=_params("parallel", "parallel", "arbitrary"),
        name="diff_attention",
    )(q, k, vt, slopes, lam_p, g_subln)


def kernel(x, c, w_mod, b_mod, g_norm, w_up, w_down, w_a_in, b_a_gate, g_a_hnorm, w_a_out, g_kv, w_kv_mod,
           b_kv_mod, w_kv, w_b_q, b_lambda, g_b_subln, w_b_out, g_final):
    batch, seq, d = x.shape
    depth = w_mod.shape[0]
    n_a = w_a_in.shape[0]
    m_heads = b_a_gate.shape[1] // 2
    m_dv = g_a_hnorm.shape[1] // m_heads
    m_dk = (w_a_in.shape[2] - 2 * m_heads * m_dv - 2 * m_heads) // (2 * m_heads)
    d_dh = b_lambda.shape[2]
    d_heads = w_b_q.shape[2] // (2 * d_dh)
    t = batch * seq

    xt = x.reshape(t, d)
    c_rows = jnp.zeros((_SUBLANES, d), _F32).at[:batch].set(c)

    mod = _modvec(c_rows, w_mod, b_mod.reshape(depth, 1, 6 * d))[:, :batch]
    kv_mod = _modvec(c_rows, w_kv_mod[None], b_kv_mod.reshape(1, 1, 2 * d))[0, :batch]

    def vec(v):
        return v.reshape(batch, 1, d)

    ones_d = jnp.ones((1, d), _F32)
    for l in range(depth):
        sh1, sc1, gt1, sh2, sc2, gt2 = (vec(m) for m in jnp.split(mod[l], 6, axis=-1))
        if l < n_a:
            qk_w, v_w = m_heads * m_dk, m_heads * m_dv
            main_w = 2 * qk_w + 2 * v_w
            w_in = w_a_in[l]
            col_scale = jnp.concatenate([jnp.ones((qk_w,), _F32), jnp.full((qk_w,), m_dk ** -0.5, _F32),
                                         jnp.ones((2 * v_w,), _F32)]).reshape(1, main_w)
            w_gate = jnp.zeros((d, _LANES), _BF16).at[:, :2 * m_heads].set(w_in[:, main_w:].astype(_BF16))
            proj, gates = _norm_mod_matmul(xt, g_norm[l, 0][None], sc1, sh1, w_in[:, :main_w].astype(_BF16),
                                           col_scale, seq, w_narrow=w_gate)
            b_gate = jnp.zeros((1, _LANES), _F32).at[0, :2 * m_heads].set(b_a_gate[l])
            mix_in = _mlstm(proj, gates, b_gate, g_a_hnorm[l][None], batch, seq, m_heads, m_dk, m_dv)
            w_out = w_a_out[l]
        else:
            j = l - n_a
            lam_init = 0.8 - 0.6 * math.exp(-0.3 * l)
            q_scale = d_dh ** -0.5 * _LOG2E
            if l == n_a:
                kv_shift, kv_scale = (vec(m) for m in jnp.split(kv_mod, 2, axis=-1))
                k_w = w_b_q.shape[2]
                q, k_sh, vt_sh = _qkv_proj(xt, g_norm[l, 0][None], sc1, sh1, g_kv[None], kv_scale, kv_shift,
                                           w_b_q[j].astype(_BF16), w_kv[:, :k_w].astype(_BF16),
                                           w_kv[:, k_w:].T.astype(_BF16), q_scale, batch, seq)
            else:
                q = _norm_mod_matmul(xt, g_norm[l, 0][None], sc1, sh1, w_b_q[j].astype(_BF16),
                                     jnp.full((1, w_b_q.shape[2]), q_scale, _F32), seq)
            mix_in = _diff_attention(q, k_sh, vt_sh, b_lambda[j], g_b_subln[j][None], batch, seq, d_heads, d_dh,
                                     lam_init)
            w_out = w_b_out[j]
        xt = _matmul_residual(mix_in, w_out.astype(_BF16), xt, gt1, seq)
        last = l == depth - 1
        xt = _mlp(xt, g_norm[l, 1][None], sc2, sh2, gt2, w_up[l].astype(_BF16), w_down[l].astype(_BF16),
                  g_final[None] if last else ones_d, seq, last)
    return xt.reshape(batch, seq, d)
```

```python
import functools
import math

import jax
import jax.numpy as jnp
from jax import lax
from jax.experimental import pallas as pl
from jax.experimental.pallas import tpu as pltpu

_EPS = 1e-6
_F32 = jnp.float32
_BF16 = jnp.bfloat16

_LANES = 128
_SUBLANES = 8
_BF16_ROWS = 16
_VMEM_LIMIT_BYTES = 56 * 1024 * 1024

_MODVEC_TN = 1024
_PROJ_TM = 1024
_PROJ_TN = 1024
_MLP_TM = 1024
_MLP_TF = 512
_NORM_ROWS = 16
_NORM_UNROLL = 8
_MLSTM_CHUNK = 128
_ATTN_BLOCK = 512
_QKV_TN = 512

_LOG2E = math.log2(math.e)

_NT_DIMS = (((1,), (1,)), ((), ()))
_TN_DIMS = (((0,), (0,)), ((), ()))


def _params(*semantics):
    return pltpu.CompilerParams(dimension_semantics=semantics, vmem_limit_bytes=_VMEM_LIMIT_BYTES)


def _modvec_kernel(c_ref, w_ref, b_ref, o_ref):
    o_ref[...] = jnp.dot(c_ref[...].astype(_BF16), w_ref[...].astype(_BF16),
                         preferred_element_type=_F32) + b_ref[...]


def _modvec(c_rows, w, b):
    n_l, d, n = w.shape
    tn = _MODVEC_TN
    return pl.pallas_call(
        _modvec_kernel,
        grid=(n_l, n // tn),
        in_specs=[pl.BlockSpec((_SUBLANES, d), lambda l, j: (0, 0)),
                  pl.BlockSpec((None, d, tn), lambda l, j: (l, 0, j)),
                  pl.BlockSpec((None, 1, tn), lambda l, j: (l, 0, j))],
        out_specs=pl.BlockSpec((None, _SUBLANES, tn), lambda l, j: (l, 0, j)),
        out_shape=jax.ShapeDtypeStruct((n_l, _SUBLANES, n), _F32),
        compiler_params=_params("parallel", "parallel"),
        name="modvec",
    )(c_rows, w, b)


def _norm_mod_tile(x_ref, gain, shift, h_ref):
    n_chunks = x_ref.shape[0] // _NORM_ROWS

    def body(r, carry):
        rows = pl.ds(pl.multiple_of(r * _NORM_ROWS, _NORM_ROWS), _NORM_ROWS)
        x = x_ref[rows, :]
        inv = lax.rsqrt(jnp.mean(x * x, axis=-1, keepdims=True) + _EPS)
        h_ref[rows, :] = (x * inv * gain + shift).astype(h_ref.dtype)
        return carry

    lax.fori_loop(0, n_chunks, body, 0, unroll=_NORM_UNROLL)


def _norm_mod_matmul_kernel(x_ref, g_ref, sc_ref, sh_ref, w_ref, cs_ref, *rest, narrow):
    if narrow:
        wn_ref, o_ref, on_ref, h_ref = rest
    else:
        o_ref, h_ref = rest

    @pl.when(pl.program_id(1) == 0)
    def _():
        _norm_mod_tile(x_ref, g_ref[...] * (1.0 + sc_ref[...]), sh_ref[...], h_ref)
        if narrow:
            on_ref[...] = jnp.dot(h_ref[...], wn_ref[...], preferred_element_type=_F32)

    acc = jnp.dot(h_ref[...], w_ref[...], preferred_element_type=_F32)
    o_ref[...] = (acc * cs_ref[...]).astype(o_ref.dtype)


def _norm_mod_matmul(x, g, scale, shift, w, layer, col_scale, seq, w_narrow=None):
    t, d = x.shape
    n = col_scale.shape[1]
    tm = min(_PROJ_TM, seq)
    tn = min(_PROJ_TN, n)
    tiles_per_seq = seq // tm
    narrow = w_narrow is not None
    batch_vec = pl.BlockSpec((None, 1, d), lambda i, j: (i // tiles_per_seq, 0, 0))
    in_specs = [pl.BlockSpec((tm, d), lambda i, j: (i, 0)),
                pl.BlockSpec((1, d), lambda i, j: (0, 0)),
                batch_vec, batch_vec,
                pl.BlockSpec((None, d, tn), lambda i, j: (layer, 0, j)),
                pl.BlockSpec((1, tn), lambda i, j: (0, j))]
    out_specs = [pl.BlockSpec((tm, tn), lambda i, j: (i, j))]
    out_shape = [jax.ShapeDtypeStruct((t, n), _BF16)]
    operands = [x, g, scale, shift, w, col_scale]
    if narrow:
        in_specs.append(pl.BlockSpec((d, _LANES), lambda i, j: (0, 0)))
        out_specs.append(pl.BlockSpec((tm, _LANES), lambda i, j: (i, 0)))
        out_shape.append(jax.ShapeDtypeStruct((t, _LANES), _F32))
        operands.append(w_narrow)
    outs = pl.pallas_call(
        functools.partial(_norm_mod_matmul_kernel, narrow=narrow),
        grid=(t // tm, n // tn),
        in_specs=in_specs,
        out_specs=out_specs,
        out_shape=out_shape,
        scratch_shapes=[pltpu.VMEM((tm, d), _BF16)],
        compiler_params=_params("parallel", "arbitrary"),
        name="norm_mod_matmul",
    )(*operands)
    return outs if narrow else outs[0]


def _matmul_residual_kernel(a_ref, w_ref, x_ref, gate_ref, o_ref):
    acc = jnp.dot(a_ref[...], w_ref[...], preferred_element_type=_F32)
    o_ref[...] = x_ref[...] + gate_ref[...] * acc


def _matmul_residual(a, w, layer, x, gate, seq):
    t, k = a.shape
    d = w.shape[2]
    tm = min(_PROJ_TM, seq)
    tn = min(_PROJ_TN, d)
    tiles_per_seq = seq // tm
    return pl.pallas_call(
        _matmul_residual_kernel,
        grid=(t // tm, d // tn),
        in_specs=[pl.BlockSpec((tm, k), lambda i, j: (i, 0)),
                  pl.BlockSpec((None, k, tn), lambda i, j: (layer, 0, j)),
                  pl.BlockSpec((tm, tn), lambda i, j: (i, j)),
                  pl.BlockSpec((None, 1, tn), lambda i, j: (i // tiles_per_seq, 0, j))],
        out_specs=pl.BlockSpec((tm, tn), lambda i, j: (i, j)),
        out_shape=jax.ShapeDtypeStruct((t, d), _F32),
        compiler_params=_params("parallel", "parallel"),
        name="matmul_residual",
    )(a, w, x, gate)


def _mlp_kernel(x_ref, g_ref, sc_ref, sh_ref, gate_ref, wu_ref, wd_ref, gf_ref, o_ref, h_ref, *, final_norm):
    f = pl.program_id(1)

    @pl.when(f == 0)
    def _():
        _norm_mod_tile(x_ref, g_ref[...] * (1.0 + sc_ref[...]), sh_ref[...], h_ref)
        o_ref[...] = jnp.zeros_like(o_ref)

    u = jnp.dot(h_ref[...], wu_ref[...], preferred_element_type=_F32)
    u = jnp.square(jnp.maximum(u, 0.0)).astype(_BF16)
    o_ref[...] += jnp.dot(u, wd_ref[...], preferred_element_type=_F32)

    @pl.when(f == pl.num_programs(1) - 1)
    def _():
        gate = gate_ref[...]
        gain = gf_ref[...]
        slab = _NORM_ROWS * _NORM_UNROLL

        def residual(rows):
            return x_ref[rows, :] + gate * o_ref[rows, :]

        def body(r, carry):
            base = pl.multiple_of(r * slab, slab)
            chunks = [pl.ds(base + c * _NORM_ROWS, _NORM_ROWS) for c in range(_NORM_UNROLL)]
            if final_norm:
                inv = [lax.rsqrt(jnp.mean(jnp.square(residual(rows)), axis=-1, keepdims=True) + _EPS)
                       for rows in chunks]
                for rows, scale in zip(chunks, inv):
                    o_ref[rows, :] = residual(rows) * scale * gain
            else:
                for rows in chunks:
                    o_ref[rows, :] = residual(rows)
            return carry

        lax.fori_loop(0, x_ref.shape[0] // slab, body, 0)


def _mlp(x, g, scale, shift, gate, w_up, w_down, layer, g_final, seq, final_norm):
    t, d = x.shape
    d_ff = w_up.shape[2]
    tm = min(_MLP_TM, seq)
    tf = min(_MLP_TF, d_ff)
    tiles_per_seq = seq // tm
    batch_vec = pl.BlockSpec((None, 1, d), lambda i, f: (i // tiles_per_seq, 0, 0))
    return pl.pallas_call(
        functools.partial(_mlp_kernel, final_norm=final_norm),
        grid=(t // tm, d_ff // tf),
        in_specs=[pl.BlockSpec((tm, d), lambda i, f: (i, 0)),
                  pl.BlockSpec((1, d), lambda i, f: (0, 0)),
                  batch_vec, batch_vec, batch_vec,
                  pl.BlockSpec((None, d, tf), lambda i, f: (layer, 0, f)),
                  pl.BlockSpec((None, tf, d), lambda i, f: (layer, f, 0)),
                  pl.BlockSpec((1, d), lambda i, f: (0, 0))],
        out_specs=pl.BlockSpec((tm, d), lambda i, f: (i, 0)),
        out_shape=jax.ShapeDtypeStruct((t, d), _F32),
        scratch_shapes=[pltpu.VMEM((tm, d), _BF16)],
        compiler_params=_params("parallel", "arbitrary"),
        name="mlp",
    )(x, g, scale, shift, gate, w_up, w_down, g_final)


def _log_sigmoid(x):
    return jnp.minimum(x, 0.0) - jnp.log1p(jnp.exp(-jnp.abs(x)))


def _mlstm_kernel(q_ref, k_ref, v_ref, og_ref, gt_ref, bg_ref, gh_ref, out_ref, c_ref, m_ref,
                  *, heads, dk, dv):
    @pl.when(pl.program_id(1) == 0)
    def _():
        c_ref[...] = jnp.zeros_like(c_ref)
        m_ref[...] = jnp.zeros_like(m_ref)

    chunk = q_ref.shape[0]
    gates = gt_ref[...] + bg_ref[...]
    logf = _log_sigmoid(gates)
    src = lax.broadcasted_iota(jnp.int32, (chunk, chunk), 0)
    dst = lax.broadcasted_iota(jnp.int32, (chunk, chunk), 1)
    b_all = jnp.dot((dst <= src).astype(_F32), logf, preferred_element_type=_F32,
                    precision=lax.Precision.HIGHEST)
    gates_t = gates.T
    b_all_t = b_all.T
    visible = src <= dst

    for h in range(heads):
        b_row = b_all_t[heads + h:heads + h + 1, :]
        src_col = gates[:, h:h + 1] - b_all[:, heads + h:heads + h + 1]
        b_last = b_row[:, chunk - 1:chunk]
        m_prev = m_ref[h]
        c_aug = c_ref[h]
        qh = q_ref[:, h * dk:(h + 1) * dk]
        kh = k_ref[:, h * dk:(h + 1) * dk]
        vh = v_ref[:, h * dv:(h + 1) * dv]

        dmat_t = jnp.where(visible, b_row + src_col, -jnp.inf)
        inter = b_row + m_prev
        m_j = jnp.maximum(inter, jnp.max(dmat_t, axis=0, keepdims=True))
        w_inter = jnp.exp(inter - m_j)
        s_t = lax.dot_general(kh, qh, _NT_DIMS, preferred_element_type=_F32) * jnp.exp(dmat_t - m_j)
        state_q = lax.dot_general(c_aug.astype(_BF16), qh, _NT_DIMS, preferred_element_type=_F32)
        num_t = (w_inter * state_q[:dv, :]
                 + lax.dot_general(vh, s_t.astype(_BF16), _TN_DIMS, preferred_element_type=_F32))
        den = w_inter * state_q[dv:dv + 1, :] + jnp.sum(s_t, axis=0, keepdims=True)
        h_t = num_t * (1.0 / jnp.maximum(jnp.abs(den), jnp.exp(-m_j)))

        dec = b_last + src_col
        m_new = jnp.maximum(b_last + m_prev, jnp.max(dec, axis=0, keepdims=True))
        kw = kh.astype(_F32) * jnp.exp(dec - m_new)
        carry_scale = jnp.exp(b_last + m_prev - m_new)
        c_ref[h, :dv, :] = carry_scale * c_aug[:dv, :] + lax.dot_general(
            vh, kw.astype(_BF16), _TN_DIMS, preferred_element_type=_F32)
        c_ref[h, dv:dv + 1, :] = carry_scale * c_aug[dv:dv + 1, :] + jnp.sum(kw, axis=0, keepdims=True)
        m_ref[h] = m_new

        y_t = h_t * lax.rsqrt(jnp.mean(h_t * h_t, axis=0, keepdims=True) + _EPS)
        og = og_ref[:, h * dv:(h + 1) * dv].astype(_F32)
        out_ref[:, h * dv:(h + 1) * dv] = (y_t.T * gh_ref[:, h * dv:(h + 1) * dv]
                                           * jax.nn.sigmoid(og)).astype(out_ref.dtype)


def _mlstm(proj, gates, b_gate, g_hnorm, batch, seq, heads, dk, dv):
    t = proj.shape[0]
    chunk = _MLSTM_CHUNK
    n_chunks = seq // chunk
    qk_w, v_w = heads * dk, heads * dv
    assert v_w == 2 * qk_w

    def rows(b, c):
        return b * n_chunks + c

    return pl.pallas_call(
        functools.partial(_mlstm_kernel, heads=heads, dk=dk, dv=dv),
        grid=(batch, n_chunks),
        in_specs=[pl.BlockSpec((chunk, qk_w), lambda b, c: (rows(b, c), 0)),
                  pl.BlockSpec((chunk, qk_w), lambda b, c: (rows(b, c), 1)),
                  pl.BlockSpec((chunk, v_w), lambda b, c: (rows(b, c), 1)),
                  pl.BlockSpec((chunk, v_w), lambda b, c: (rows(b, c), 2)),
                  pl.BlockSpec((chunk, _LANES), lambda b, c: (rows(b, c), 0)),
                  pl.BlockSpec((1, _LANES), lambda b, c: (0, 0)),
                  pl.BlockSpec((1, v_w), lambda b, c: (0, 0))],
        out_specs=pl.BlockSpec((chunk, v_w), lambda b, c: (rows(b, c), 0)),
        out_shape=jax.ShapeDtypeStruct((t, v_w), _BF16),
        scratch_shapes=[pltpu.VMEM((heads, dv + _BF16_ROWS, dk), _F32),
                        pltpu.VMEM((heads, 1, 1), _F32)],
        compiler_params=_params("parallel", "arbitrary"),
        name="mlstm",
    )(proj, proj, proj, proj, gates, b_gate, g_hnorm)


def _qkv_proj_kernel(x_ref, g1_ref, sc1_ref, sh1_ref, gkv_ref, sckv_ref, shkv_ref, wq_ref, wk_ref, wvt_ref,
                     q_ref, k_ref, vt_ref, h1_ref, hk_ref, *, n_q, n_k, q_scale):
    j = pl.program_id(1)

    @pl.when(j == 0)
    def _():
        gain1 = g1_ref[...] * (1.0 + sc1_ref[...])
        shift1 = sh1_ref[...]
        gain_kv = gkv_ref[...] * (1.0 + sckv_ref[...])
        shift_kv = shkv_ref[...]
        n_chunks = x_ref.shape[0] // _NORM_ROWS

        def body(r, carry):
            rows = pl.ds(pl.multiple_of(r * _NORM_ROWS, _NORM_ROWS), _NORM_ROWS)
            x = x_ref[rows, :]
            xn = x * lax.rsqrt(jnp.mean(x * x, axis=-1, keepdims=True) + _EPS)
            h1_ref[rows, :] = (xn * gain1 + shift1).astype(h1_ref.dtype)
            hk_ref[rows, :] = (xn * gain_kv + shift_kv).astype(hk_ref.dtype)
            return carry

        lax.fori_loop(0, n_chunks, body, 0, unroll=_NORM_UNROLL)

    @pl.when(j < n_q)
    def _():
        acc = jnp.dot(h1_ref[...], wq_ref[...], preferred_element_type=_F32)
        q_ref[...] = (acc * q_scale).astype(q_ref.dtype)

    @pl.when(jnp.logical_and(j >= n_q, j < n_q + n_k))
    def _():
        k_ref[...] = jnp.dot(hk_ref[...], wk_ref[...], preferred_element_type=_F32).astype(k_ref.dtype)

    @pl.when(j >= n_q + n_k)
    def _():
        vt_ref[...] = lax.dot_general(wvt_ref[...], hk_ref[...], _NT_DIMS,
                                      preferred_element_type=_F32).astype(vt_ref.dtype)


def _qkv_proj(x, g1, sc1, sh1, gkv, sckv, shkv, wq, layer, wkv, wvt, q_scale, batch, seq):
    t, d = x.shape
    n_qc, n_vc = wq.shape[2], wvt.shape[0]
    n_kc = wkv.shape[1] - n_vc
    tm = min(_PROJ_TM, seq)
    tn = _QKV_TN
    tiles_per_seq = seq // tm
    n_q, n_k, n_v = n_qc // tn, n_kc // tn, n_vc // tn

    def clamp(j, lo, n):
        return jnp.clip(j - lo, 0, n - 1)

    batch_vec = pl.BlockSpec((None, 1, d), lambda i, j: (i // tiles_per_seq, 0, 0))
    row_vec = pl.BlockSpec((1, d), lambda i, j: (0, 0))
    return pl.pallas_call(
        functools.partial(_qkv_proj_kernel, n_q=n_q, n_k=n_k, q_scale=q_scale),
        grid=(t // tm, n_q + n_k + n_v),
        in_specs=[pl.BlockSpec((tm, d), lambda i, j: (i, 0)),
                  row_vec, batch_vec, batch_vec, row_vec, batch_vec, batch_vec,
                  pl.BlockSpec((None, d, tn), lambda i, j: (layer, 0, clamp(j, 0, n_q))),
                  pl.BlockSpec((d, tn), lambda i, j: (0, clamp(j, n_q, n_k))),
                  pl.BlockSpec((tn, d), lambda i, j: (clamp(j, n_q + n_k, n_v), 0))],
        out_specs=[pl.BlockSpec((tm, tn), lambda i, j: (i, clamp(j, 0, n_q))),
                   pl.BlockSpec((tm, tn), lambda i, j: (i, clamp(j, n_q, n_k))),
                   pl.BlockSpec((None, tn, tm), lambda i, j: (i // tiles_per_seq, clamp(j, n_q + n_k, n_v),
                                                              i % tiles_per_seq))],
        out_shape=[jax.ShapeDtypeStruct((t, n_qc), _BF16),
                   jax.ShapeDtypeStruct((t, n_kc), _BF16),
                   jax.ShapeDtypeStruct((batch, n_vc, seq), _BF16)],
        scratch_shapes=[pltpu.VMEM((tm, d), _BF16), pltpu.VMEM((tm, d), _BF16)],
        compiler_params=_params("parallel", "arbitrary"),
        name="qkv_proj",
    )(x, g1, sc1, sh1, gkv, sckv, shkv, wq, wkv, wvt)


def _attn_kernel(q_ref, k_ref, vt_ref, slope_ref, lam_ref, g_ref, o_ref, qa_ref, kpos_ref, s_ref, bmax_ref, m_ref,
                 acc_ref, *, dh, lam_init):
    qi = pl.program_id(2)
    blk = q_ref.shape[0]
    slope2 = slope_ref[:, 0:1] * _LOG2E

    dv = vt_ref.shape[0]

    @pl.when(qi == 0)
    def _():
        lane = lax.broadcasted_iota(jnp.int32, (blk, dh), 1)
        key_bias = lax.broadcasted_iota(jnp.int32, (blk, dh), 0).astype(_F32) * slope2
        part_hi = key_bias.astype(_BF16).astype(_F32)
        part_mid = (key_bias - part_hi).astype(_BF16).astype(_F32)
        part_lo = key_bias - part_hi - part_mid
        kpos_ref[...] = jnp.where(lane == 0, part_hi, jnp.where(lane == 1, part_mid,
                                                                jnp.where(lane == 2, part_lo, 0.0))).astype(_BF16)
        for s_idx in range(2):
            qa_ref[s_idx, :, dh:] = jnp.where(lane < 3, 1.0, 0.0).astype(_BF16)

    for s_idx in range(2):
        qa_ref[s_idx, :, :dh] = q_ref[:, s_idx * dh:(s_idx + 1) * dh]

    acc_ref[...] = jnp.zeros_like(acc_ref)
    m_ref[...] = jnp.full_like(m_ref, -jnp.inf)

    def scores(kj, masked, buf):
        keys = pl.ds(pl.multiple_of(kj * blk, blk), blk)
        for s_idx in range(2):
            k_cat = jnp.concatenate([k_ref[keys, s_idx * dh:(s_idx + 1) * dh], kpos_ref[...]], axis=1)
            st = lax.dot_general(k_cat, qa_ref[s_idx], _NT_DIMS, preferred_element_type=_F32)
            if masked:
                key = lax.broadcasted_iota(jnp.int32, (blk, blk), 0)
                qry = lax.broadcasted_iota(jnp.int32, (blk, blk), 1)
                st = jnp.where(key <= qry, st, -jnp.inf)
            s_ref[buf, s_idx] = st
            bmax_ref[buf, s_idx] = jnp.max(st, axis=0, keepdims=True)

    def accumulate(kj, buf):
        keys = pl.ds(pl.multiple_of(kj * blk, blk), blk)
        vt_aug = jnp.concatenate([vt_ref[:, keys], jnp.ones((_BF16_ROWS, blk), _BF16)], axis=0)
        block_bias = slope2 * jnp.full((1, 1), (kj - qi) * blk, jnp.int32).astype(_F32)
        for s_idx in range(2):
            m_old = m_ref[s_idx]
            m_new = jnp.maximum(m_old, bmax_ref[buf, s_idx] + block_bias)
            pt = jnp.exp2(s_ref[buf, s_idx] - (m_new - block_bias))
            acc_ref[s_idx] = acc_ref[s_idx] * jnp.exp2(m_old - m_new) + jnp.dot(
                vt_aug, pt.astype(_BF16), preferred_element_type=_F32)
            m_ref[s_idx] = m_new

    def block_of(t):
        return jnp.where(t == 0, qi, t - 1)

    def pair(u):
        t = 2 * u
        scores(t, False, 1)
        accumulate(block_of(t), 0)
        scores(t + 1, False, 0)
        accumulate(t, 1)

    def two_pairs(u, carry):
        pair(2 * u)
        pair(2 * u + 1)
        return carry

    scores(qi, True, 0)
    n_pairs = lax.shift_right_logical(qi, 1)
    lax.fori_loop(0, lax.shift_right_logical(n_pairs, 1), two_pairs, 0)

    @pl.when((n_pairs & 1) == 1)
    def _():
        pair(n_pairs - 1)

    t_rest = 2 * n_pairs
    one_more = (qi & 1) == 1

    @pl.when(one_more)
    def _():
        scores(t_rest, False, 1)
        accumulate(block_of(t_rest), 0)
        accumulate(t_rest, 1)

    @pl.when(jnp.logical_not(one_more))
    def _():
        accumulate(block_of(t_rest), 0)

    lp = lam_ref[...]
    lam = (jnp.exp(jnp.sum(lp[0:1, :] * lp[1:2, :], axis=-1, keepdims=True))
           - jnp.exp(jnp.sum(lp[2:3, :] * lp[3:4, :], axis=-1, keepdims=True)) + lam_init)
    ot = (acc_ref[0, :dv, :] * (1.0 / acc_ref[0, dv:dv + 1, :])
          - lam * (acc_ref[1, :dv, :] * (1.0 / acc_ref[1, dv:dv + 1, :])))
    ot = ot * lax.rsqrt(jnp.mean(ot * ot, axis=0, keepdims=True) + _EPS)
    o_ref[...] = (ot.T * g_ref[...] * (1.0 - lam_init)).astype(o_ref.dtype)


def _diff_attention(q, k, vt, lam_p, g_subln, batch, seq, heads, dh, lam_init):
    t = q.shape[0]
    dv = 2 * dh
    blk = min(_ATTN_BLOCK, seq)
    n_blk = seq // blk
    slopes = jnp.asarray([2.0 ** (-8.0 * (h + 1) / heads) for h in range(heads)], _F32)
    slopes = jnp.broadcast_to(slopes[:, None, None], (heads, 1, _LANES))
    return pl.pallas_call(
        functools.partial(_attn_kernel, dh=dh, lam_init=lam_init),
        grid=(batch, heads, n_blk),
        in_specs=[pl.BlockSpec((blk, dv), lambda b, h, i: (b * n_blk + i, h)),
                  pl.BlockSpec((seq, dv), lambda b, h, i: (b, h)),
                  pl.BlockSpec((None, dv, seq), lambda b, h, i: (b, h, 0)),
                  pl.BlockSpec((None, 1, _LANES), lambda b, h, i: (h, 0, 0)),
                  pl.BlockSpec((4, dh), lambda b, h, i: (0, 0)),
                  pl.BlockSpec((1, dv), lambda b, h, i: (0, h))],
        out_specs=pl.BlockSpec((blk, dv), lambda b, h, i: (b * n_blk + i, h)),
        out_shape=jax.ShapeDtypeStruct((t, heads * dv), _BF16),
        scratch_shapes=[pltpu.VMEM((2, blk, 2 * dh), _BF16),
                        pltpu.VMEM((blk, dh), _BF16),
                        pltpu.VMEM((2, 2, blk, blk), _F32),
                        pltpu.VMEM((2, 2, 1, blk), _F32),
                        pltpu.VMEM((2, 1, blk), _F32),
                        pltpu.VMEM((2, dv + _BF16_ROWS, blk), _F32)],
        compiler_params=_params("parallel", "parallel", "arbitrary"),
        name="diff_attention",
    )(q, k, vt, slopes, lam_p, g_subln)


def kernel(x, c, w_mod, b_mod, g_norm, w_up, w_down, w_a_in, b_a_gate, g_a_hnorm, w_a_out, g_kv, w_kv_mod,
           b_kv_mod, w_kv, w_b_q, b_lambda, g_b_subln, w_b_out, g_final):
    batch, seq, d = x.shape
    depth = w_mod.shape[0]
    n_a = w_a_in.shape[0]
    m_heads = b_a_gate.shape[1] // 2
    m_dv = g_a_hnorm.shape[1] // m_heads
    m_dk = (w_a_in.shape[2] - 2 * m_heads * m_dv - 2 * m_heads) // (2 * m_heads)
    d_dh = b_lambda.shape[2]
    d_heads = w_b_q.shape[2] // (2 * d_dh)
    t = batch * seq

    xt = x.reshape(t, d)
    c_rows = jnp.zeros((_SUBLANES, d), _F32).at[:batch].set(c)

    mod = _modvec(c_rows, w_mod, b_mod.reshape(depth, 1, 6 * d))[:, :batch]
    kv_mod = _modvec(c_rows, w_kv_mod[None], b_kv_mod.reshape(1, 1, 2 * d))[0, :batch]

    def vec(v):
        return v.reshape(batch, 1, d)

    w_up_b, w_down_b = w_up.astype(_BF16), w_down.astype(_BF16)
    w_a_in_b, w_a_out_b = w_a_in.astype(_BF16), w_a_out.astype(_BF16)
    w_b_q_b, w_b_out_b, w_kv_b = w_b_q.astype(_BF16), w_b_out.astype(_BF16), w_kv.astype(_BF16)

    ones_d = jnp.ones((1, d), _F32)
    for l in range(depth):
        sh1, sc1, gt1, sh2, sc2, gt2 = (vec(m) for m in jnp.split(mod[l], 6, axis=-1))
        if l < n_a:
            qk_w, v_w = m_heads * m_dk, m_heads * m_dv
            main_w = 2 * qk_w + 2 * v_w
            col_scale = jnp.concatenate([jnp.ones((qk_w,), _F32), jnp.full((qk_w,), m_dk ** -0.5, _F32),
                                         jnp.ones((2 * v_w,), _F32)]).reshape(1, main_w)
            w_gate = jnp.zeros((d, _LANES), _BF16).at[:, :2 * m_heads].set(w_a_in_b[l, :, main_w:])
            proj, gates = _norm_mod_matmul(xt, g_norm[l, 0][None], sc1, sh1, w_a_in_b, l, col_scale, seq,
                                           w_narrow=w_gate)
            b_gate = jnp.zeros((1, _LANES), _F32).at[0, :2 * m_heads].set(b_a_gate[l])
            mix_in = _mlstm(proj, gates, b_gate, g_a_hnorm[l][None], batch, seq, m_heads, m_dk, m_dv)
            xt = _matmul_residual(mix_in, w_a_out_b, l, xt, gt1, seq)
        else:
            j = l - n_a
            lam_init = 0.8 - 0.6 * math.exp(-0.3 * l)
            q_scale = d_dh ** -0.5 * _LOG2E
            if l == n_a:
                kv_shift, kv_scale = (vec(m) for m in jnp.split(kv_mod, 2, axis=-1))
                k_w = w_b_q.shape[2]
                q, k_sh, vt_sh = _qkv_proj(xt, g_norm[l, 0][None], sc1, sh1, g_kv[None], kv_scale, kv_shift,
                                           w_b_q_b, j, w_kv_b, w_kv_b[:, k_w:].T, q_scale, batch, seq)
            else:
                q = _norm_mod_matmul(xt, g_norm[l, 0][None], sc1, sh1, w_b_q_b, j,
                                     jnp.full((1, w_b_q.shape[2]), q_scale, _F32), seq)
            mix_in = _diff_attention(q, k_sh, vt_sh, b_lambda[j], g_b_subln[j][None], batch, seq, d_heads, d_dh,
                                     lam_init)
            xt = _matmul_residual(mix_in, w_b_out_b, j, xt, gt1, seq)
        last = l == depth - 1
        xt = _mlp(xt, g_norm[l, 1][None], sc2, sh2, gt2, w_up_b, w_down_b, l,
                  g_final[None] if last else ones_d, seq, last)
    return xt.reshape(batch, seq, d)
```

```python
import functools
import math

import jax
import jax.numpy as jnp
from jax import lax
from jax.experimental import pallas as pl
from jax.experimental.pallas import tpu as pltpu

_EPS = 1e-6
_F32 = jnp.float32
_BF16 = jnp.bfloat16

_LANES = 128
_SUBLANES = 8
_BF16_ROWS = 16
_VMEM_LIMIT_BYTES = 59 * 1024 * 1024

_MODVEC_TN = 1024
_PROJ_TM = 1024
_PROJ_TN = 1024
_MLP_TM = 1024
_MLP_TF = 1024
_NORM_ROWS = 16
_NORM_UNROLL = 8
_MLSTM_CHUNK = 128
_ATTN_BLOCK = 512
_QKV_TN = 1024

_LOG2E = math.log2(math.e)

_NT_DIMS = (((1,), (1,)), ((), ()))
_TN_DIMS = (((0,), (0,)), ((), ()))


def _params(*semantics):
    return pltpu.CompilerParams(dimension_semantics=semantics, vmem_limit_bytes=_VMEM_LIMIT_BYTES)


def _modvec_kernel(c_ref, w_ref, b_ref, o_ref):
    o_ref[...] = jnp.dot(c_ref[...].astype(_BF16), w_ref[...].astype(_BF16),
                         preferred_element_type=_F32) + b_ref[...]


def _modvec(c_rows, w, b):
    n_l, d, n = w.shape
    tn = _MODVEC_TN
    return pl.pallas_call(
        _modvec_kernel,
        grid=(n_l, n // tn),
        in_specs=[pl.BlockSpec((_SUBLANES, d), lambda l, j: (0, 0)),
                  pl.BlockSpec((None, d, tn), lambda l, j: (l, 0, j)),
                  pl.BlockSpec((None, 1, tn), lambda l, j: (l, 0, j))],
        out_specs=pl.BlockSpec((None, _SUBLANES, tn), lambda l, j: (l, 0, j)),
        out_shape=jax.ShapeDtypeStruct((n_l, _SUBLANES, n), _F32),
        compiler_params=_params("parallel", "parallel"),
        name="modvec",
    )(c_rows, w, b)


def _norm_mod_tile(x_ref, gain, shift, h_ref):
    n_chunks = x_ref.shape[0] // _NORM_ROWS

    def body(r, carry):
        rows = pl.ds(pl.multiple_of(r * _NORM_ROWS, _NORM_ROWS), _NORM_ROWS)
        x = x_ref[rows, :]
        inv = lax.rsqrt(jnp.mean(x * x, axis=-1, keepdims=True) + _EPS)
        h_ref[rows, :] = (x * inv * gain + shift).astype(h_ref.dtype)
        return carry

    lax.fori_loop(0, n_chunks, body, 0, unroll=_NORM_UNROLL)


def _norm_mod_matmul_kernel(x_ref, g_ref, sc_ref, sh_ref, w_ref, cs_ref, *rest, narrow):
    if narrow:
        wn_ref, o_ref, on_ref, h_ref = rest
    else:
        o_ref, h_ref = rest

    @pl.when(pl.program_id(1) == 0)
    def _():
        _norm_mod_tile(x_ref, g_ref[...] * (1.0 + sc_ref[...]), sh_ref[...], h_ref)
        if narrow:
            on_ref[...] = jnp.dot(h_ref[...], wn_ref[...], preferred_element_type=_F32)

    acc = jnp.dot(h_ref[...], w_ref[...], preferred_element_type=_F32)
    o_ref[...] = (acc * cs_ref[...]).astype(o_ref.dtype)


def _norm_mod_matmul(x, g, scale, shift, w, layer, col_scale, seq, w_narrow=None):
    t, d = x.shape
    n = col_scale.shape[1]
    tm = min(_PROJ_TM, seq)
    tn = min(_PROJ_TN, n)
    tiles_per_seq = seq // tm
    narrow = w_narrow is not None
    batch_vec = pl.BlockSpec((None, 1, d), lambda i, j: (i // tiles_per_seq, 0, 0))
    in_specs = [pl.BlockSpec((tm, d), lambda i, j: (i, 0)),
                pl.BlockSpec((1, d), lambda i, j: (0, 0)),
                batch_vec, batch_vec,
                pl.BlockSpec((None, d, tn), lambda i, j: (layer, 0, j)),
                pl.BlockSpec((1, tn), lambda i, j: (0, j))]
    out_specs = [pl.BlockSpec((tm, tn), lambda i, j: (i, j))]
    out_shape = [jax.ShapeDtypeStruct((t, n), _BF16)]
    operands = [x, g, scale, shift, w, col_scale]
    if narrow:
        in_specs.append(pl.BlockSpec((d, _LANES), lambda i, j: (0, 0)))
        out_specs.append(pl.BlockSpec((tm, _LANES), lambda i, j: (i, 0)))
        out_shape.append(jax.ShapeDtypeStruct((t, _LANES), _F32))
        operands.append(w_narrow)
    outs = pl.pallas_call(
        functools.partial(_norm_mod_matmul_kernel, narrow=narrow),
        grid=(t // tm, n // tn),
        in_specs=in_specs,
        out_specs=out_specs,
        out_shape=out_shape,
        scratch_shapes=[pltpu.VMEM((tm, d), _BF16)],
        compiler_params=_params("parallel", "arbitrary"),
        name="norm_mod_matmul",
    )(*operands)
    return outs if narrow else outs[0]


def _matmul_residual_kernel(a_ref, w_ref, x_ref, gate_ref, o_ref):
    acc = jnp.dot(a_ref[...], w_ref[...], preferred_element_type=_F32)
    o_ref[...] = x_ref[...] + gate_ref[...] * acc


def _matmul_residual(a, w, layer, x, gate, seq):
    t, k = a.shape
    d = w.shape[2]
    tm = min(_PROJ_TM, seq)
    tn = min(_PROJ_TN, d)
    tiles_per_seq = seq // tm
    return pl.pallas_call(
        _matmul_residual_kernel,
        grid=(t // tm, d // tn),
        in_specs=[pl.BlockSpec((tm, k), lambda i, j: (i, 0)),
                  pl.BlockSpec((None, k, tn), lambda i, j: (layer, 0, j)),
                  pl.BlockSpec((tm, tn), lambda i, j: (i, j)),
                  pl.BlockSpec((None, 1, tn), lambda i, j: (i // tiles_per_seq, 0, j))],
        out_specs=pl.BlockSpec((tm, tn), lambda i, j: (i, j)),
        out_shape=jax.ShapeDtypeStruct((t, d), _F32),
        compiler_params=_params("parallel", "parallel"),
        name="matmul_residual",
    )(a, w, x, gate)


def _mlp_kernel(x_ref, g_ref, sc_ref, sh_ref, gate_ref, wu_ref, wd_ref, gf_ref, o_ref, h_ref, *, final_norm):
    f = pl.program_id(1)

    @pl.when(f == 0)
    def _():
        _norm_mod_tile(x_ref, g_ref[...] * (1.0 + sc_ref[...]), sh_ref[...], h_ref)
        o_ref[...] = jnp.zeros_like(o_ref)

    u = jnp.dot(h_ref[...], wu_ref[...], preferred_element_type=_F32)
    u = jnp.square(jnp.maximum(u, 0.0)).astype(_BF16)
    o_ref[...] += jnp.dot(u, wd_ref[...], preferred_element_type=_F32)

    @pl.when(f == pl.num_programs(1) - 1)
    def _():
        gate = gate_ref[...]
        gain = gf_ref[...]
        slab = _NORM_ROWS * _NORM_UNROLL

        def residual(rows):
            return x_ref[rows, :] + gate * o_ref[rows, :]

        def body(r, carry):
            base = pl.multiple_of(r * slab, slab)
            chunks = [pl.ds(base + c * _NORM_ROWS, _NORM_ROWS) for c in range(_NORM_UNROLL)]
            if final_norm:
                inv = [lax.rsqrt(jnp.mean(jnp.square(residual(rows)), axis=-1, keepdims=True) + _EPS)
                       for rows in chunks]
                for rows, scale in zip(chunks, inv):
                    o_ref[rows, :] = residual(rows) * scale * gain
            else:
                for rows in chunks:
                    o_ref[rows, :] = residual(rows)
            return carry

        lax.fori_loop(0, x_ref.shape[0] // slab, body, 0)


def _mlp(x, g, scale, shift, gate, w_up, w_down, layer, g_final, seq, final_norm):
    t, d = x.shape
    d_ff = w_up.shape[2]
    tm = min(_MLP_TM, seq)
    tf = min(_MLP_TF, d_ff)
    tiles_per_seq = seq // tm
    batch_vec = pl.BlockSpec((None, 1, d), lambda i, f: (i // tiles_per_seq, 0, 0))
    return pl.pallas_call(
        functools.partial(_mlp_kernel, final_norm=final_norm),
        grid=(t // tm, d_ff // tf),
        in_specs=[pl.BlockSpec((tm, d), lambda i, f: (i, 0)),
                  pl.BlockSpec((1, d), lambda i, f: (0, 0)),
                  batch_vec, batch_vec, batch_vec,
                  pl.BlockSpec((None, d, tf), lambda i, f: (layer, 0, f)),
                  pl.BlockSpec((None, tf, d), lambda i, f: (layer, f, 0)),
                  pl.BlockSpec((1, d), lambda i, f: (0, 0))],
        out_specs=pl.BlockSpec((tm, d), lambda i, f: (i, 0)),
        out_shape=jax.ShapeDtypeStruct((t, d), _F32),
        scratch_shapes=[pltpu.VMEM((tm, d), _BF16)],
        compiler_params=_params("parallel", "arbitrary"),
        name="mlp",
    )(x, g, scale, shift, gate, w_up, w_down, g_final)


def _log_sigmoid(x):
    return jnp.minimum(x, 0.0) - jnp.log1p(jnp.exp(-jnp.abs(x)))


def _mlstm_kernel(q_ref, k_ref, v_ref, og_ref, gt_ref, bg_ref, gh_ref, out_ref, c_ref, m_ref,
                  *, heads, dk, dv):
    @pl.when(pl.program_id(1) == 0)
    def _():
        c_ref[...] = jnp.zeros_like(c_ref)
        m_ref[...] = jnp.zeros_like(m_ref)

    chunk = q_ref.shape[0]
    gates = gt_ref[...] + bg_ref[...]
    logf = _log_sigmoid(gates)
    src = lax.broadcasted_iota(jnp.int32, (chunk, chunk), 0)
    dst = lax.broadcasted_iota(jnp.int32, (chunk, chunk), 1)
    b_all = jnp.dot((dst <= src).astype(_F32), logf, preferred_element_type=_F32,
                    precision=lax.Precision.HIGHEST)
    gates_t = gates.T
    b_all_t = b_all.T
    visible = src <= dst

    for h in range(heads):
        b_row = b_all_t[heads + h:heads + h + 1, :]
        src_col = gates[:, h:h + 1] - b_all[:, heads + h:heads + h + 1]
        b_last = b_row[:, chunk - 1:chunk]
        m_prev = m_ref[h]
        c_aug = c_ref[h]
        qh = q_ref[:, h * dk:(h + 1) * dk]
        kh = k_ref[:, h * dk:(h + 1) * dk]
        vh = v_ref[:, h * dv:(h + 1) * dv]

        dmat_t = jnp.where(visible, b_row + src_col, -jnp.inf)
        inter = b_row + m_prev
        m_j = jnp.maximum(inter, jnp.max(dmat_t, axis=0, keepdims=True))
        w_inter = jnp.exp(inter - m_j)
        s_t = lax.dot_general(kh, qh, _NT_DIMS, preferred_element_type=_F32) * jnp.exp(dmat_t - m_j)
        state_q = lax.dot_general(c_aug.astype(_BF16), qh, _NT_DIMS, preferred_element_type=_F32)
        num_t = (w_inter * state_q[:dv, :]
                 + lax.dot_general(vh, s_t.astype(_BF16), _TN_DIMS, preferred_element_type=_F32))
        den = w_inter * state_q[dv:dv + 1, :] + jnp.sum(s_t, axis=0, keepdims=True)
        h_t = num_t * (1.0 / jnp.maximum(jnp.abs(den), jnp.exp(-m_j)))

        dec = b_last + src_col
        m_new = jnp.maximum(b_last + m_prev, jnp.max(dec, axis=0, keepdims=True))
        kw = kh.astype(_F32) * jnp.exp(dec - m_new)
        carry_scale = jnp.exp(b_last + m_prev - m_new)
        c_ref[h, :dv, :] = carry_scale * c_aug[:dv, :] + lax.dot_general(
            vh, kw.astype(_BF16), _TN_DIMS, preferred_element_type=_F32)
        c_ref[h, dv:dv + 1, :] = carry_scale * c_aug[dv:dv + 1, :] + jnp.sum(kw, axis=0, keepdims=True)
        m_ref[h] = m_new

        y_t = h_t * lax.rsqrt(jnp.mean(h_t * h_t, axis=0, keepdims=True) + _EPS)
        og = og_ref[:, h * dv:(h + 1) * dv].astype(_F32)
        out_ref[:, h * dv:(h + 1) * dv] = (y_t.T * gh_ref[:, h * dv:(h + 1) * dv]
                                           * jax.nn.sigmoid(og)).astype(out_ref.dtype)


def _mlstm(proj, gates, b_gate, g_hnorm, batch, seq, heads, dk, dv):
    t = proj.shape[0]
    chunk = _MLSTM_CHUNK
    n_chunks = seq // chunk
    qk_w, v_w = heads * dk, heads * dv
    assert v_w == 2 * qk_w

    def rows(b, c):
        return b * n_chunks + c

    return pl.pallas_call(
        functools.partial(_mlstm_kernel, heads=heads, dk=dk, dv=dv),
        grid=(batch, n_chunks),
        in_specs=[pl.BlockSpec((chunk, qk_w), lambda b, c: (rows(b, c), 0)),
                  pl.BlockSpec((chunk, qk_w), lambda b, c: (rows(b, c), 1)),
                  pl.BlockSpec((chunk, v_w), lambda b, c: (rows(b, c), 1)),
                  pl.BlockSpec((chunk, v_w), lambda b, c: (rows(b, c), 2)),
                  pl.BlockSpec((chunk, _LANES), lambda b, c: (rows(b, c), 0)),
                  pl.BlockSpec((1, _LANES), lambda b, c: (0, 0)),
                  pl.BlockSpec((1, v_w), lambda b, c: (0, 0))],
        out_specs=pl.BlockSpec((chunk, v_w), lambda b, c: (rows(b, c), 0)),
        out_shape=jax.ShapeDtypeStruct((t, v_w), _BF16),
        scratch_shapes=[pltpu.VMEM((heads, dv + _BF16_ROWS, dk), _F32),
                        pltpu.VMEM((heads, 1, 1), _F32)],
        compiler_params=_params("parallel", "arbitrary"),
        name="mlstm",
    )(proj, proj, proj, proj, gates, b_gate, g_hnorm)


def _qkv_proj_kernel(x_ref, g1_ref, sc1_ref, sh1_ref, gkv_ref, sckv_ref, shkv_ref, wqk_ref, wvt_ref,
                     qk_ref, vt_ref, h1_ref, hk_ref, *, n_q, n_k, q_scale):
    j = pl.program_id(1)

    @pl.when(j == 0)
    def _():
        gain1 = g1_ref[...] * (1.0 + sc1_ref[...])
        shift1 = sh1_ref[...]
        gain_kv = gkv_ref[...] * (1.0 + sckv_ref[...])
        shift_kv = shkv_ref[...]
        n_chunks = x_ref.shape[0] // _NORM_ROWS

        def body(r, carry):
            rows = pl.ds(pl.multiple_of(r * _NORM_ROWS, _NORM_ROWS), _NORM_ROWS)
            x = x_ref[rows, :]
            xn = x * lax.rsqrt(jnp.mean(x * x, axis=-1, keepdims=True) + _EPS)
            h1_ref[rows, :] = (xn * gain1 + shift1).astype(h1_ref.dtype)
            hk_ref[rows, :] = (xn * gain_kv + shift_kv).astype(hk_ref.dtype)
            return carry

        lax.fori_loop(0, n_chunks, body, 0, unroll=_NORM_UNROLL)

    @pl.when(j < n_q)
    def _():
        acc = jnp.dot(h1_ref[...], wqk_ref[...], preferred_element_type=_F32)
        qk_ref[...] = (acc * q_scale).astype(qk_ref.dtype)

    @pl.when(jnp.logical_and(j >= n_q, j < n_q + n_k))
    def _():
        qk_ref[...] = jnp.dot(hk_ref[...], wqk_ref[...], preferred_element_type=_F32).astype(qk_ref.dtype)

    @pl.when(j >= n_q + n_k)
    def _():
        vt_ref[...] = lax.dot_general(wvt_ref[...], hk_ref[...], _NT_DIMS,
                                      preferred_element_type=_F32).astype(vt_ref.dtype)


def _qkv_proj(x, g1, sc1, sh1, gkv, sckv, shkv, wqk, n_qc, wvt, q_scale, batch, seq):
    t, d = x.shape
    n_qkc, n_vc = wqk.shape[1], wvt.shape[0]
    tm = min(_PROJ_TM, seq)
    tn = _QKV_TN
    tiles_per_seq = seq // tm
    n_q, n_qk, n_v = n_qc // tn, n_qkc // tn, n_vc // tn

    def qk_col(j):
        return jnp.minimum(j, n_qk - 1)

    def vt_row(j):
        return jnp.maximum(j - n_qk, 0)

    batch_vec = pl.BlockSpec((None, 1, d), lambda i, j: (i // tiles_per_seq, 0, 0))
    row_vec = pl.BlockSpec((1, d), lambda i, j: (0, 0))
    return pl.pallas_call(
        functools.partial(_qkv_proj_kernel, n_q=n_q, n_k=n_qk - n_q, q_scale=q_scale),
        grid=(t // tm, n_qk + n_v),
        in_specs=[pl.BlockSpec((tm, d), lambda i, j: (i, 0)),
                  row_vec, batch_vec, batch_vec, row_vec, batch_vec, batch_vec,
                  pl.BlockSpec((d, tn), lambda i, j: (0, qk_col(j))),
                  pl.BlockSpec((tn, d), lambda i, j: (vt_row(j), 0))],
        out_specs=[pl.BlockSpec((tm, tn), lambda i, j: (i, qk_col(j))),
                   pl.BlockSpec((None, tn, tm), lambda i, j: (i // tiles_per_seq, vt_row(j), i % tiles_per_seq))],
        out_shape=[jax.ShapeDtypeStruct((t, n_qkc), _BF16),
                   jax.ShapeDtypeStruct((batch, n_vc, seq), _BF16)],
        scratch_shapes=[pltpu.VMEM((tm, d), _BF16), pltpu.VMEM((tm, d), _BF16)],
        compiler_params=_params("parallel", "arbitrary"),
        name="qkv_proj",
    )(x, g1, sc1, sh1, gkv, sckv, shkv, wqk, wvt)


def _attn_kernel(q_ref, k_ref, vt_ref, slope_ref, lam_ref, g_ref, o_ref, qa_ref, kpos_ref, s_ref, bmax_ref, m_ref,
                 acc_ref, *, dh, lam_init):
    qi = pl.program_id(2)
    blk = q_ref.shape[0]
    slope2 = slope_ref[:, 0:1] * _LOG2E

    dv = vt_ref.shape[0]

    @pl.when(qi == 0)
    def _():
        lane = lax.broadcasted_iota(jnp.int32, (blk, dh), 1)
        key_bias = lax.broadcasted_iota(jnp.int32, (blk, dh), 0).astype(_F32) * slope2
        part_hi = key_bias.astype(_BF16).astype(_F32)
        part_mid = (key_bias - part_hi).astype(_BF16).astype(_F32)
        part_lo = key_bias - part_hi - part_mid
        kpos_ref[...] = jnp.where(lane == 0, part_hi, jnp.where(lane == 1, part_mid,
                                                                jnp.where(lane == 2, part_lo, 0.0))).astype(_BF16)
        for s_idx in range(2):
            qa_ref[s_idx, :, dh:] = jnp.where(lane < 3, 1.0, 0.0).astype(_BF16)

    for s_idx in range(2):
        qa_ref[s_idx, :, :dh] = q_ref[:, s_idx * dh:(s_idx + 1) * dh]

    acc_ref[...] = jnp.zeros_like(acc_ref)
    m_ref[...] = jnp.full_like(m_ref, -jnp.inf)

    def scores(kj, masked, buf):
        keys = pl.ds(pl.multiple_of(kj * blk, blk), blk)
        for s_idx in range(2):
            k_cat = jnp.concatenate([k_ref[keys, s_idx * dh:(s_idx + 1) * dh], kpos_ref[...]], axis=1)
            st = lax.dot_general(k_cat, qa_ref[s_idx], _NT_DIMS, preferred_element_type=_F32)
            if masked:
                key = lax.broadcasted_iota(jnp.int32, (blk, blk), 0)
                qry = lax.broadcasted_iota(jnp.int32, (blk, blk), 1)
                st = jnp.where(key <= qry, st, -jnp.inf)
            s_ref[buf, s_idx] = st
            bmax_ref[buf, s_idx] = jnp.max(st, axis=0, keepdims=True)

    def accumulate(kj, buf):
        keys = pl.ds(pl.multiple_of(kj * blk, blk), blk)
        vt_aug = jnp.concatenate([vt_ref[:, keys], jnp.ones((_BF16_ROWS, blk), _BF16)], axis=0)
        block_bias = slope2 * jnp.full((1, 1), (kj - qi) * blk, jnp.int32).astype(_F32)
        for s_idx in range(2):
            m_old = m_ref[s_idx]
            m_new = jnp.maximum(m_old, bmax_ref[buf, s_idx] + block_bias)
            pt = jnp.exp2(s_ref[buf, s_idx] - (m_new - block_bias))
            acc_ref[s_idx] = acc_ref[s_idx] * jnp.exp2(m_old - m_new) + jnp.dot(
                vt_aug, pt.astype(_BF16), preferred_element_type=_F32)
            m_ref[s_idx] = m_new

    def block_of(t):
        return jnp.where(t == 0, qi, t - 1)

    def pair(u):
        t = 2 * u
        scores(t, False, 1)
        accumulate(block_of(t), 0)
        scores(t + 1, False, 0)
        accumulate(t, 1)

    def two_pairs(u, carry):
        pair(2 * u)
        pair(2 * u + 1)
        return carry

    scores(qi, True, 0)
    n_pairs = lax.shift_right_logical(qi, 1)
    lax.fori_loop(0, lax.shift_right_logical(n_pairs, 1), two_pairs, 0)

    @pl.when((n_pairs & 1) == 1)
    def _():
        pair(n_pairs - 1)

    t_rest = 2 * n_pairs
    one_more = (qi & 1) == 1

    @pl.when(one_more)
    def _():
        scores(t_rest, False, 1)
        accumulate(block_of(t_rest), 0)
        accumulate(t_rest, 1)

    @pl.when(jnp.logical_not(one_more))
    def _():
        accumulate(block_of(t_rest), 0)

    lp = lam_ref[...]
    lam = (jnp.exp(jnp.sum(lp[0:1, :] * lp[1:2, :], axis=-1, keepdims=True))
           - jnp.exp(jnp.sum(lp[2:3, :] * lp[3:4, :], axis=-1, keepdims=True)) + lam_init)
    ot = (acc_ref[0, :dv, :] * (1.0 / acc_ref[0, dv:dv + 1, :])
          - lam * (acc_ref[1, :dv, :] * (1.0 / acc_ref[1, dv:dv + 1, :])))
    ot = ot * lax.rsqrt(jnp.mean(ot * ot, axis=0, keepdims=True) + _EPS)
    o_ref[...] = (ot.T * g_ref[...] * (1.0 - lam_init)).astype(o_ref.dtype)


def _diff_attention(q, k, k_head0, vt, lam_p, g_subln, batch, seq, heads, dh, lam_init):
    t = q.shape[0]
    dv = 2 * dh
    blk = min(_ATTN_BLOCK, seq)
    n_blk = seq // blk
    slopes = jnp.asarray([2.0 ** (-8.0 * (h + 1) / heads) for h in range(heads)], _F32)
    slopes = jnp.broadcast_to(slopes[:, None, None], (heads, 1, _LANES))
    return pl.pallas_call(
        functools.partial(_attn_kernel, dh=dh, lam_init=lam_init),
        grid=(batch, heads, n_blk),
        in_specs=[pl.BlockSpec((blk, dv), lambda b, h, i: (b * n_blk + i, h)),
                  pl.BlockSpec((seq, dv), lambda b, h, i: (b, k_head0 + h)),
                  pl.BlockSpec((None, dv, seq), lambda b, h, i: (b, h, 0)),
                  pl.BlockSpec((None, 1, _LANES), lambda b, h, i: (h, 0, 0)),
                  pl.BlockSpec((4, dh), lambda b, h, i: (0, 0)),
                  pl.BlockSpec((1, dv), lambda b, h, i: (0, h))],
        out_specs=pl.BlockSpec((blk, dv), lambda b, h, i: (b * n_blk + i, h)),
        out_shape=jax.ShapeDtypeStruct((t, heads * dv), _BF16),
        scratch_shapes=[pltpu.VMEM((2, blk, 2 * dh), _BF16),
                        pltpu.VMEM((blk, dh), _BF16),
                        pltpu.VMEM((2, 2, blk, blk), _F32),
                        pltpu.VMEM((2, 2, 1, blk), _F32),
                        pltpu.VMEM((2, 1, blk), _F32),
                        pltpu.VMEM((2, dv + _BF16_ROWS, blk), _F32)],
        compiler_params=_params("parallel", "parallel", "arbitrary"),
        name="diff_attention",
    )(q, k, vt, slopes, lam_p, g_subln)


def kernel(x, c, w_mod, b_mod, g_norm, w_up, w_down, w_a_in, b_a_gate, g_a_hnorm, w_a_out, g_kv, w_kv_mod,
           b_kv_mod, w_kv, w_b_q, b_lambda, g_b_subln, w_b_out, g_final):
    batch, seq, d = x.shape
    depth = w_mod.shape[0]
    n_a = w_a_in.shape[0]
    m_heads = b_a_gate.shape[1] // 2
    m_dv = g_a_hnorm.shape[1] // m_heads
    m_dk = (w_a_in.shape[2] - 2 * m_heads * m_dv - 2 * m_heads) // (2 * m_heads)
    d_dh = b_lambda.shape[2]
    d_heads = w_b_q.shape[2] // (2 * d_dh)
    t = batch * seq

    xt = x.reshape(t, d)
    c_rows = jnp.zeros((_SUBLANES, d), _F32).at[:batch].set(c)

    mod = _modvec(c_rows, w_mod, b_mod.reshape(depth, 1, 6 * d))[:, :batch]
    kv_mod = _modvec(c_rows, w_kv_mod[None], b_kv_mod.reshape(1, 1, 2 * d))[0, :batch]

    def vec(v):
        return v.reshape(batch, 1, d)

    w_up_b, w_down_b = w_up.astype(_BF16), w_down.astype(_BF16)
    w_a_in_b, w_a_out_b = w_a_in.astype(_BF16), w_a_out.astype(_BF16)
    w_b_q_b, w_b_out_b, w_kv_b = w_b_q.astype(_BF16), w_b_out.astype(_BF16), w_kv.astype(_BF16)

    ones_d = jnp.ones((1, d), _F32)
    for l in range(depth):
        sh1, sc1, gt1, sh2, sc2, gt2 = (vec(m) for m in jnp.split(mod[l], 6, axis=-1))
        if l < n_a:
            qk_w, v_w = m_heads * m_dk, m_heads * m_dv
            main_w = 2 * qk_w + 2 * v_w
            col_scale = jnp.concatenate([jnp.ones((qk_w,), _F32), jnp.full((qk_w,), m_dk ** -0.5, _F32),
                                         jnp.ones((2 * v_w,), _F32)]).reshape(1, main_w)
            w_gate = jnp.zeros((d, _LANES), _BF16).at[:, :2 * m_heads].set(w_a_in_b[l, :, main_w:])
            proj, gates = _norm_mod_matmul(xt, g_norm[l, 0][None], sc1, sh1, w_a_in_b, l, col_scale, seq,
                                           w_narrow=w_gate)
            b_gate = jnp.zeros((1, _LANES), _F32).at[0, :2 * m_heads].set(b_a_gate[l])
            mix_in = _mlstm(proj, gates, b_gate, g_a_hnorm[l][None], batch, seq, m_heads, m_dk, m_dv)
            xt = _matmul_residual(mix_in, w_a_out_b, l, xt, gt1, seq)
        else:
            j = l - n_a
            lam_init = 0.8 - 0.6 * math.exp(-0.3 * l)
            q_scale = d_dh ** -0.5 * _LOG2E
            k_w = w_b_q.shape[2]
            if l == n_a:
                kv_shift, kv_scale = (vec(m) for m in jnp.split(kv_mod, 2, axis=-1))
                w_qk = jnp.concatenate([w_b_q_b[j], w_kv_b[:, :k_w]], axis=1)
                q, vt_sh = _qkv_proj(xt, g_norm[l, 0][None], sc1, sh1, g_kv[None], kv_scale, kv_shift,
                                     w_qk, k_w, w_kv_b[:, k_w:].T, q_scale, batch, seq)
                k_sh = q
            else:
                q = _norm_mod_matmul(xt, g_norm[l, 0][None], sc1, sh1, w_b_q_b, j,
                                     jnp.full((1, k_w), q_scale, _F32), seq)
            mix_in = _diff_attention(q, k_sh, d_heads, vt_sh, b_lambda[j], g_b_subln[j][None], batch, seq,
                                     d_heads, d_dh, lam_init)
            xt = _matmul_residual(mix_in, w_b_out_b, j, xt, gt1, seq)
        last = l == depth - 1
        xt = _mlp(xt, g_norm[l, 1][None], sc2, sh2, gt2, w_up_b, w_down_b, l,
                  g_final[None] if last else ones_d, seq, last)
    return xt.reshape(batch, seq, d)
```

```python
import functools
import math

import jax
import jax.numpy as jnp
from jax import lax
from jax.experimental import pallas as pl
from jax.experimental.pallas import tpu as pltpu

_EPS = 1e-6
_F32 = jnp.float32
_BF16 = jnp.bfloat16

_LANES = 128
_SUBLANES = 8
_BF16_ROWS = 16
_VMEM_LIMIT_BYTES = 59 * 1024 * 1024

_MODVEC_TN = 1024
_PROJ_TM = 1024
_IN_PROJ_TN = 2048
_OUT_PROJ_TM = 512
_OUT_PROJ_TN = 2048
_MLP_TM = 1024
_MLP_TF = 1024
_NORM_ROWS = 16
_NORM_UNROLL = 8
_MLSTM_CHUNK = 128
_MLSTM_CHUNKS_PER_STEP = 2
_ATTN_BLOCK = 512
_QKV_TN = 1024

_LOG2E = math.log2(math.e)

_NT_DIMS = (((1,), (1,)), ((), ()))
_TN_DIMS = (((0,), (0,)), ((), ()))


def _params(*semantics):
    return pltpu.CompilerParams(dimension_semantics=semantics, vmem_limit_bytes=_VMEM_LIMIT_BYTES)


def _modvec_kernel(c_ref, w_ref, b_ref, o_ref):
    o_ref[...] = jnp.dot(c_ref[...].astype(_BF16), w_ref[...].astype(_BF16),
                         preferred_element_type=_F32) + b_ref[...]


def _modvec(c_rows, w, b):
    n_l, d, n = w.shape
    tn = _MODVEC_TN
    return pl.pallas_call(
        _modvec_kernel,
        grid=(n_l, n // tn),
        in_specs=[pl.BlockSpec((_SUBLANES, d), lambda l, j: (0, 0)),
                  pl.BlockSpec((None, d, tn), lambda l, j: (l, 0, j)),
                  pl.BlockSpec((None, 1, tn), lambda l, j: (l, 0, j))],
        out_specs=pl.BlockSpec((None, _SUBLANES, tn), lambda l, j: (l, 0, j)),
        out_shape=jax.ShapeDtypeStruct((n_l, _SUBLANES, n), _F32),
        compiler_params=_params("parallel", "parallel"),
        name="modvec",
    )(c_rows, w, b)


def _norm_mod_tile(x_ref, gain, shift, h_ref):
    n_chunks = x_ref.shape[0] // _NORM_ROWS

    def body(r, carry):
        rows = pl.ds(pl.multiple_of(r * _NORM_ROWS, _NORM_ROWS), _NORM_ROWS)
        x = x_ref[rows, :]
        inv = lax.rsqrt(jnp.mean(x * x, axis=-1, keepdims=True) + _EPS)
        h_ref[rows, :] = (x * inv * gain + shift).astype(h_ref.dtype)
        return carry

    lax.fori_loop(0, n_chunks, body, 0, unroll=_NORM_UNROLL)


def _norm_mod_matmul_kernel(x_ref, g_ref, sc_ref, sh_ref, w_ref, cs_ref, *rest, narrow):
    if narrow:
        wn_ref, o_ref, on_ref, h_ref = rest
    else:
        o_ref, h_ref = rest

    @pl.when(pl.program_id(1) == 0)
    def _():
        _norm_mod_tile(x_ref, g_ref[...] * (1.0 + sc_ref[...]), sh_ref[...], h_ref)
        if narrow:
            on_ref[...] = jnp.dot(h_ref[...], wn_ref[...], preferred_element_type=_F32)

    acc = jnp.dot(h_ref[...], w_ref[...], preferred_element_type=_F32)
    o_ref[...] = (acc * cs_ref[...]).astype(o_ref.dtype)


def _norm_mod_matmul(x, g, scale, shift, w, layer, col_scale, seq, w_narrow=None):
    t, d = x.shape
    n = col_scale.shape[1]
    tm = min(_PROJ_TM, seq)
    tn = min(_IN_PROJ_TN, n)
    tiles_per_seq = seq // tm
    narrow = w_narrow is not None
    batch_vec = pl.BlockSpec((None, 1, d), lambda i, j: (i // tiles_per_seq, 0, 0))
    in_specs = [pl.BlockSpec((tm, d), lambda i, j: (i, 0)),
                pl.BlockSpec((1, d), lambda i, j: (0, 0)),
                batch_vec, batch_vec,
                pl.BlockSpec((None, d, tn), lambda i, j: (layer, 0, j)),
                pl.BlockSpec((1, tn), lambda i, j: (0, j))]
    out_specs = [pl.BlockSpec((tm, tn), lambda i, j: (i, j))]
    out_shape = [jax.ShapeDtypeStruct((t, n), _BF16)]
    operands = [x, g, scale, shift, w, col_scale]
    if narrow:
        in_specs.append(pl.BlockSpec((d, _LANES), lambda i, j: (0, 0)))
        out_specs.append(pl.BlockSpec((tm, _LANES), lambda i, j: (i, 0)))
        out_shape.append(jax.ShapeDtypeStruct((t, _LANES), _F32))
        operands.append(w_narrow)
    outs = pl.pallas_call(
        functools.partial(_norm_mod_matmul_kernel, narrow=narrow),
        grid=(t // tm, n // tn),
        in_specs=in_specs,
        out_specs=out_specs,
        out_shape=out_shape,
        scratch_shapes=[pltpu.VMEM((tm, d), _BF16)],
        compiler_params=_params("parallel", "arbitrary"),
        name="norm_mod_matmul",
    )(*operands)
    return outs if narrow else outs[0]


def _matmul_residual_kernel(a_ref, w_ref, x_ref, gate_ref, o_ref):
    acc = jnp.dot(a_ref[...], w_ref[...], preferred_element_type=_F32)
    o_ref[...] = x_ref[...] + gate_ref[...] * acc


def _matmul_residual(a, w, layer, x, gate, seq):
    t, k = a.shape
    d = w.shape[2]
    tm = min(_OUT_PROJ_TM, seq)
    tn = min(_OUT_PROJ_TN, d)
    tiles_per_seq = seq // tm
    return pl.pallas_call(
        _matmul_residual_kernel,
        grid=(t // tm, d // tn),
        in_specs=[pl.BlockSpec((tm, k), lambda i, j: (i, 0)),
                  pl.BlockSpec((None, k, tn), lambda i, j: (layer, 0, j)),
                  pl.BlockSpec((tm, tn), lambda i, j: (i, j)),
                  pl.BlockSpec((None, 1, tn), lambda i, j: (i // tiles_per_seq, 0, j))],
        out_specs=pl.BlockSpec((tm, tn), lambda i, j: (i, j)),
        out_shape=jax.ShapeDtypeStruct((t, d), _F32),
        compiler_params=_params("parallel", "parallel"),
        name="matmul_residual",
    )(a, w, x, gate)


def _mlp_kernel(x_ref, g_ref, sc_ref, sh_ref, gate_ref, wu_ref, wd_ref, gf_ref, o_ref, h_ref, *, final_norm):
    f = pl.program_id(1)

    @pl.when(f == 0)
    def _():
        _norm_mod_tile(x_ref, g_ref[...] * (1.0 + sc_ref[...]), sh_ref[...], h_ref)
        o_ref[...] = jnp.zeros_like(o_ref)

    u = jnp.dot(h_ref[...], wu_ref[...], preferred_element_type=_F32)
    u = jnp.square(jnp.maximum(u, 0.0)).astype(_BF16)
    o_ref[...] += jnp.dot(u, wd_ref[...], preferred_element_type=_F32)

    @pl.when(f == pl.num_programs(1) - 1)
    def _():
        gate = gate_ref[...]
        gain = gf_ref[...]
        slab = _NORM_ROWS * _NORM_UNROLL

        def residual(rows):
            return x_ref[rows, :] + gate * o_ref[rows, :]

        def body(r, carry):
            base = pl.multiple_of(r * slab, slab)
            chunks = [pl.ds(base + c * _NORM_ROWS, _NORM_ROWS) for c in range(_NORM_UNROLL)]
            if final_norm:
                inv = [lax.rsqrt(jnp.mean(jnp.square(residual(rows)), axis=-1, keepdims=True) + _EPS)
                       for rows in chunks]
                for rows, scale in zip(chunks, inv):
                    o_ref[rows, :] = residual(rows) * scale * gain
            else:
                for rows in chunks:
                    o_ref[rows, :] = residual(rows)
            return carry

        lax.fori_loop(0, x_ref.shape[0] // slab, body, 0)


def _mlp(x, g, scale, shift, gate, w_up, w_down, layer, g_final, seq, final_norm):
    t, d = x.shape
    d_ff = w_up.shape[2]
    tm = min(_MLP_TM, seq)
    tf = min(_MLP_TF, d_ff)
    tiles_per_seq = seq // tm
    batch_vec = pl.BlockSpec((None, 1, d), lambda i, f: (i // tiles_per_seq, 0, 0))
    return pl.pallas_call(
        functools.partial(_mlp_kernel, final_norm=final_norm),
        grid=(t // tm, d_ff // tf),
        in_specs=[pl.BlockSpec((tm, d), lambda i, f: (i, 0)),
                  pl.BlockSpec((1, d), lambda i, f: (0, 0)),
                  batch_vec, batch_vec, batch_vec,
                  pl.BlockSpec((None, d, tf), lambda i, f: (layer, 0, f)),
                  pl.BlockSpec((None, tf, d), lambda i, f: (layer, f, 0)),
                  pl.BlockSpec((1, d), lambda i, f: (0, 0))],
        out_specs=pl.BlockSpec((tm, d), lambda i, f: (i, 0)),
        out_shape=jax.ShapeDtypeStruct((t, d), _F32),
        scratch_shapes=[pltpu.VMEM((tm, d), _BF16)],
        compiler_params=_params("parallel", "arbitrary"),
        name="mlp",
    )(x, g, scale, shift, gate, w_up, w_down, g_final)


def _log_sigmoid(x):
    return jnp.minimum(x, 0.0) - jnp.log1p(jnp.exp(-jnp.abs(x)))


def _mlstm_kernel(q_ref, k_ref, v_ref, og_ref, gt_ref, bg_ref, gh_ref, out_ref, c_ref, m_ref,
                  *, heads, dk, dv):
    @pl.when(pl.program_id(1) == 0)
    def _():
        c_ref[...] = jnp.zeros_like(c_ref)
        m_ref[...] = jnp.zeros_like(m_ref)

    chunk = _MLSTM_CHUNK
    src = lax.broadcasted_iota(jnp.int32, (chunk, chunk), 0)
    dst = lax.broadcasted_iota(jnp.int32, (chunk, chunk), 1)
    visible = src <= dst
    for sub in range(q_ref.shape[0] // chunk):
        rows = slice(sub * chunk, (sub + 1) * chunk)
        gates = gt_ref[rows, :] + bg_ref[...]
        b_all = jnp.dot((dst <= src).astype(_F32), _log_sigmoid(gates), preferred_element_type=_F32,
                        precision=lax.Precision.HIGHEST)
        _mlstm_chunk(q_ref, k_ref, v_ref, og_ref, gh_ref, out_ref, c_ref, m_ref, rows, gates, b_all, visible,
                     heads, dk, dv)


def _mlstm_chunk(q_ref, k_ref, v_ref, og_ref, gh_ref, out_ref, c_ref, m_ref, rows, gates, b_all, visible,
                 heads, dk, dv):
    chunk = _MLSTM_CHUNK
    gates_t = gates.T
    b_all_t = b_all.T
    for h in range(heads):
        b_row = b_all_t[heads + h:heads + h + 1, :]
        src_col = gates[:, h:h + 1] - b_all[:, heads + h:heads + h + 1]
        b_last = b_row[:, chunk - 1:chunk]
        m_prev = m_ref[h]
        c_aug = c_ref[h]
        qh = q_ref[rows, h * dk:(h + 1) * dk]
        kh = k_ref[rows, h * dk:(h + 1) * dk]
        vh = v_ref[rows, h * dv:(h + 1) * dv]

        dmat_t = jnp.where(visible, b_row + src_col, -jnp.inf)
        inter = b_row + m_prev
        m_j = jnp.maximum(inter, jnp.max(dmat_t, axis=0, keepdims=True))
        w_inter = jnp.exp(inter - m_j)
        s_t = lax.dot_general(kh, qh, _NT_DIMS, preferred_element_type=_F32) * jnp.exp(dmat_t - m_j)
        state_q = lax.dot_general(c_aug.astype(_BF16), qh, _NT_DIMS, preferred_element_type=_F32)
        num_t = (w_inter * state_q[:dv, :]
                 + lax.dot_general(vh, s_t.astype(_BF16), _TN_DIMS, preferred_element_type=_F32))
        den = w_inter * state_q[dv:dv + 1, :] + jnp.sum(s_t, axis=0, keepdims=True)
        h_t = num_t * (1.0 / jnp.maximum(jnp.abs(den), jnp.exp(-m_j)))

        dec = b_last + src_col
        m_new = jnp.maximum(b_last + m_prev, jnp.max(dec, axis=0, keepdims=True))
        kw = kh.astype(_F32) * jnp.exp(dec - m_new)
        carry_scale = jnp.exp(b_last + m_prev - m_new)
        c_ref[h, :dv, :] = carry_scale * c_aug[:dv, :] + lax.dot_general(
            vh, kw.astype(_BF16), _TN_DIMS, preferred_element_type=_F32)
        c_ref[h, dv:dv + 1, :] = carry_scale * c_aug[dv:dv + 1, :] + jnp.sum(kw, axis=0, keepdims=True)
        m_ref[h] = m_new

        y_t = h_t * lax.rsqrt(jnp.mean(h_t * h_t, axis=0, keepdims=True) + _EPS)
        og = og_ref[rows, h * dv:(h + 1) * dv].astype(_F32)
        out_ref[rows, h * dv:(h + 1) * dv] = (y_t.T * gh_ref[:, h * dv:(h + 1) * dv]
                                              * jax.nn.sigmoid(og)).astype(out_ref.dtype)


def _mlstm(proj, gates, b_gate, g_hnorm, batch, seq, heads, dk, dv):
    t = proj.shape[0]
    chunk = _MLSTM_CHUNK * _MLSTM_CHUNKS_PER_STEP
    n_chunks = seq // chunk
    qk_w, v_w = heads * dk, heads * dv
    assert v_w == 2 * qk_w

    def rows(b, c):
        return b * n_chunks + c

    return pl.pallas_call(
        functools.partial(_mlstm_kernel, heads=heads, dk=dk, dv=dv),
        grid=(batch, n_chunks),
        in_specs=[pl.BlockSpec((chunk, qk_w), lambda b, c: (rows(b, c), 0)),
                  pl.BlockSpec((chunk, qk_w), lambda b, c: (rows(b, c), 1)),
                  pl.BlockSpec((chunk, v_w), lambda b, c: (rows(b, c), 1)),
                  pl.BlockSpec((chunk, v_w), lambda b, c: (rows(b, c), 2)),
                  pl.BlockSpec((chunk, _LANES), lambda b, c: (rows(b, c), 0)),
                  pl.BlockSpec((1, _LANES), lambda b, c: (0, 0)),
                  pl.BlockSpec((1, v_w), lambda b, c: (0, 0))],
        out_specs=pl.BlockSpec((chunk, v_w), lambda b, c: (rows(b, c), 0)),
        out_shape=jax.ShapeDtypeStruct((t, v_w), _BF16),
        scratch_shapes=[pltpu.VMEM((heads, dv + _BF16_ROWS, dk), _F32),
                        pltpu.VMEM((heads, 1, 1), _F32)],
        compiler_params=_params("parallel", "arbitrary"),
        name="mlstm",
    )(proj, proj, proj, proj, gates, b_gate, g_hnorm)


def _qkv_proj_kernel(x_ref, g1_ref, sc1_ref, sh1_ref, gkv_ref, sckv_ref, shkv_ref, wqk_ref, wvt_ref,
                     qk_ref, vt_ref, h1_ref, hk_ref, *, n_q, n_k, q_scale):
    j = pl.program_id(1)

    @pl.when(j == 0)
    def _():
        gain1 = g1_ref[...] * (1.0 + sc1_ref[...])
        shift1 = sh1_ref[...]
        gain_kv = gkv_ref[...] * (1.0 + sckv_ref[...])
        shift_kv = shkv_ref[...]
        n_chunks = x_ref.shape[0] // _NORM_ROWS

        def body(r, carry):
            rows = pl.ds(pl.multiple_of(r * _NORM_ROWS, _NORM_ROWS), _NORM_ROWS)
            x = x_ref[rows, :]
            xn = x * lax.rsqrt(jnp.mean(x * x, axis=-1, keepdims=True) + _EPS)
            h1_ref[rows, :] = (xn * gain1 + shift1).astype(h1_ref.dtype)
            hk_ref[rows, :] = (xn * gain_kv + shift_kv).astype(hk_ref.dtype)
            return carry

        lax.fori_loop(0, n_chunks, body, 0, unroll=_NORM_UNROLL)

    @pl.when(j < n_q)
    def _():
        acc = jnp.dot(h1_ref[...], wqk_ref[...], preferred_element_type=_F32)
        qk_ref[...] = (acc * q_scale).astype(qk_ref.dtype)

    @pl.when(jnp.logical_and(j >= n_q, j < n_q + n_k))
    def _():
        qk_ref[...] = jnp.dot(hk_ref[...], wqk_ref[...], preferred_element_type=_F32).astype(qk_ref.dtype)

    @pl.when(j >= n_q + n_k)
    def _():
        vt_ref[...] = lax.dot_general(wvt_ref[...], hk_ref[...], _NT_DIMS,
                                      preferred_element_type=_F32).astype(vt_ref.dtype)


def _qkv_proj(x, g1, sc1, sh1, gkv, sckv, shkv, wqk, n_qc, wvt, q_scale, batch, seq):
    t, d = x.shape
    n_qkc, n_vc = wqk.shape[1], wvt.shape[0]
    tm = min(_PROJ_TM, seq)
    tn = _QKV_TN
    tiles_per_seq = seq // tm
    n_q, n_qk, n_v = n_qc // tn, n_qkc // tn, n_vc // tn

    def qk_col(j):
        return jnp.minimum(j, n_qk - 1)

    def vt_row(j):
        return jnp.maximum(j - n_qk, 0)

    batch_vec = pl.BlockSpec((None, 1, d), lambda i, j: (i // tiles_per_seq, 0, 0))
    row_vec = pl.BlockSpec((1, d), lambda i, j: (0, 0))
    return pl.pallas_call(
        functools.partial(_qkv_proj_kernel, n_q=n_q, n_k=n_qk - n_q, q_scale=q_scale),
        grid=(t // tm, n_qk + n_v),
        in_specs=[pl.BlockSpec((tm, d), lambda i, j: (i, 0)),
                  row_vec, batch_vec, batch_vec, row_vec, batch_vec, batch_vec,
                  pl.BlockSpec((d, tn), lambda i, j: (0, qk_col(j))),
                  pl.BlockSpec((tn, d), lambda i, j: (vt_row(j), 0))],
        out_specs=[pl.BlockSpec((tm, tn), lambda i, j: (i, qk_col(j))),
                   pl.BlockSpec((None, tn, tm), lambda i, j: (i // tiles_per_seq, vt_row(j), i % tiles_per_seq))],
        out_shape=[jax.ShapeDtypeStruct((t, n_qkc), _BF16),
                   jax.ShapeDtypeStruct((batch, n_vc, seq), _BF16)],
        scratch_shapes=[pltpu.VMEM((tm, d), _BF16), pltpu.VMEM((tm, d), _BF16)],
        compiler_params=_params("parallel", "arbitrary"),
        name="qkv_proj",
    )(x, g1, sc1, sh1, gkv, sckv, shkv, wqk, wvt)


def _attn_kernel(q_ref, k_ref, vt_ref, slope_ref, lam_ref, g_ref, o_ref, qa_ref, kpos_ref, s_ref, bmax_ref, m_ref,
                 acc_ref, *, dh, lam_init):
    qi = pl.program_id(2)
    blk = q_ref.shape[0]
    slope2 = slope_ref[:, 0:1] * _LOG2E

    dv = vt_ref.shape[0]

    @pl.when(qi == 0)
    def _():
        lane = lax.broadcasted_iota(jnp.int32, (blk, dh), 1)
        key_bias = lax.broadcasted_iota(jnp.int32, (blk, dh), 0).astype(_F32) * slope2
        part_hi = key_bias.astype(_BF16).astype(_F32)
        part_mid = (key_bias - part_hi).astype(_BF16).astype(_F32)
        part_lo = key_bias - part_hi - part_mid
        kpos_ref[...] = jnp.where(lane == 0, part_hi, jnp.where(lane == 1, part_mid,
                                                                jnp.where(lane == 2, part_lo, 0.0))).astype(_BF16)
        for s_idx in range(2):
            qa_ref[s_idx, :, dh:] = jnp.where(lane < 3, 1.0, 0.0).astype(_BF16)

    for s_idx in range(2):
        qa_ref[s_idx, :, :dh] = q_ref[:, s_idx * dh:(s_idx + 1) * dh]

    acc_ref[...] = jnp.zeros_like(acc_ref)
    m_ref[...] = jnp.full_like(m_ref, -jnp.inf)

    def scores(kj, masked, buf):
        keys = pl.ds(pl.multiple_of(kj * blk, blk), blk)
        for s_idx in range(2):
            k_cat = jnp.concatenate([k_ref[keys, s_idx * dh:(s_idx + 1) * dh], kpos_ref[...]], axis=1)
            st = lax.dot_general(k_cat, qa_ref[s_idx], _NT_DIMS, preferred_element_type=_F32)
            if masked:
                key = lax.broadcasted_iota(jnp.int32, (blk, blk), 0)
                qry = lax.broadcasted_iota(jnp.int32, (blk, blk), 1)
                st = jnp.where(key <= qry, st, -jnp.inf)
            s_ref[buf, s_idx] = st
            bmax_ref[buf, s_idx] = jnp.max(st, axis=0, keepdims=True)

    def accumulate(kj, buf):
        keys = pl.ds(pl.multiple_of(kj * blk, blk), blk)
        vt_aug = jnp.concatenate([vt_ref[:, keys], jnp.ones((_BF16_ROWS, blk), _BF16)], axis=0)
        block_bias = slope2 * jnp.full((1, 1), (kj - qi) * blk, jnp.int32).astype(_F32)
        for s_idx in range(2):
            m_old = m_ref[s_idx]
            m_new = jnp.maximum(m_old, bmax_ref[buf, s_idx] + block_bias)
            pt = jnp.exp2(s_ref[buf, s_idx] - (m_new - block_bias))
            acc_ref[s_idx] = acc_ref[s_idx] * jnp.exp2(m_old - m_new) + jnp.dot(
                vt_aug, pt.astype(_BF16), preferred_element_type=_F32)
            m_ref[s_idx] = m_new

    def block_of(t):
        return jnp.where(t == 0, qi, t - 1)

    def pair(u):
        t = 2 * u
        scores(t, False, 1)
        accumulate(block_of(t), 0)
        scores(t + 1, False, 0)
        accumulate(t, 1)

    def two_pairs(u, carry):
        pair(2 * u)
        pair(2 * u + 1)
        return carry

    scores(qi, True, 0)
    n_pairs = lax.shift_right_logical(qi, 1)
    lax.fori_loop(0, lax.shift_right_logical(n_pairs, 1), two_pairs, 0)

    @pl.when((n_pairs & 1) == 1)
    def _():
        pair(n_pairs - 1)

    t_rest = 2 * n_pairs
    one_more = (qi & 1) == 1

    @pl.when(one_more)
    def _():
        scores(t_rest, False, 1)
        accumulate(block_of(t_rest), 0)
        accumulate(t_rest, 1)

    @pl.when(jnp.logical_not(one_more))
    def _():
        accumulate(block_of(t_rest), 0)

    lp = lam_ref[...]
    lam = (jnp.exp(jnp.sum(lp[0:1, :] * lp[1:2, :], axis=-1, keepdims=True))
           - jnp.exp(jnp.sum(lp[2:3, :] * lp[3:4, :], axis=-1, keepdims=True)) + lam_init)
    ot = (acc_ref[0, :dv, :] * (1.0 / acc_ref[0, dv:dv + 1, :])
          - lam * (acc_ref[1, :dv, :] * (1.0 / acc_ref[1, dv:dv + 1, :])))
    ot = ot * lax.rsqrt(jnp.mean(ot * ot, axis=0, keepdims=True) + _EPS)
    o_ref[...] = (ot.T * g_ref[...] * (1.0 - lam_init)).astype(o_ref.dtype)


def _diff_attention(q, k, k_head0, vt, lam_p, g_subln, batch, seq, heads, dh, lam_init):
    t = q.shape[0]
    dv = 2 * dh
    blk = min(_ATTN_BLOCK, seq)
    n_blk = seq // blk
    slopes = jnp.asarray([2.0 ** (-8.0 * (h + 1) / heads) for h in range(heads)], _F32)
    slopes = jnp.broadcast_to(slopes[:, None, None], (heads, 1, _LANES))
    return pl.pallas_call(
        functools.partial(_attn_kernel, dh=dh, lam_init=lam_init),
        grid=(batch, heads, n_blk),
        in_specs=[pl.BlockSpec((blk, dv), lambda b, h, i: (b * n_blk + i, h)),
                  pl.BlockSpec((seq, dv), lambda b, h, i: (b, k_head0 + h)),
                  pl.BlockSpec((None, dv, seq), lambda b, h, i: (b, h, 0)),
                  pl.BlockSpec((None, 1, _LANES), lambda b, h, i: (h, 0, 0)),
                  pl.BlockSpec((4, dh), lambda b, h, i: (0, 0)),
                  pl.BlockSpec((1, dv), lambda b, h, i: (0, h))],
        out_specs=pl.BlockSpec((blk, dv), lambda b, h, i: (b * n_blk + i, h)),
        out_shape=jax.ShapeDtypeStruct((t, heads * dv), _BF16),
        scratch_shapes=[pltpu.VMEM((2, blk, 2 * dh), _BF16),
                        pltpu.VMEM((blk, dh), _BF16),
                        pltpu.VMEM((2, 2, blk, blk), _F32),
                        pltpu.VMEM((2, 2, 1, blk), _F32),
                        pltpu.VMEM((2, 1, blk), _F32),
                        pltpu.VMEM((2, dv + _BF16_ROWS, blk), _F32)],
        compiler_params=_params("parallel", "parallel", "arbitrary"),
        name="diff_attention",
    )(q, k, vt, slopes, lam_p, g_subln)


def kernel(x, c, w_mod, b_mod, g_norm, w_up, w_down, w_a_in, b_a_gate, g_a_hnorm, w_a_out, g_kv, w_kv_mod,
           b_kv_mod, w_kv, w_b_q, b_lambda, g_b_subln, w_b_out, g_final):
    batch, seq, d = x.shape
    depth = w_mod.shape[0]
    n_a = w_a_in.shape[0]
    m_heads = b_a_gate.shape[1] // 2
    m_dv = g_a_hnorm.shape[1] // m_heads
    m_dk = (w_a_in.shape[2] - 2 * m_heads * m_dv - 2 * m_heads) // (2 * m_heads)
    d_dh = b_lambda.shape[2]
    d_heads = w_b_q.shape[2] // (2 * d_dh)
    t = batch * seq

    xt = x.reshape(t, d)
    c_rows = jnp.zeros((_SUBLANES, d), _F32).at[:batch].set(c)

    mod = _modvec(c_rows, w_mod, b_mod.reshape(depth, 1, 6 * d))[:, :batch]
    kv_mod = _modvec(c_rows, w_kv_mod[None], b_kv_mod.reshape(1, 1, 2 * d))[0, :batch]

    def vec(v):
        return v.reshape(batch, 1, d)

    w_up_b, w_down_b = w_up.astype(_BF16), w_down.astype(_BF16)
    w_a_in_b, w_a_out_b = w_a_in.astype(_BF16), w_a_out.astype(_BF16)
    w_b_q_b, w_b_out_b, w_kv_b = w_b_q.astype(_BF16), w_b_out.astype(_BF16), w_kv.astype(_BF16)

    ones_d = jnp.ones((1, d), _F32)
    for l in range(depth):
        sh1, sc1, gt1, sh2, sc2, gt2 = (vec(m) for m in jnp.split(mod[l], 6, axis=-1))
        if l < n_a:
            qk_w, v_w = m_heads * m_dk, m_heads * m_dv
            main_w = 2 * qk_w + 2 * v_w
            col_scale = jnp.concatenate([jnp.ones((qk_w,), _F32), jnp.full((qk_w,), m_dk ** -0.5, _F32),
                                         jnp.ones((2 * v_w,), _F32)]).reshape(1, main_w)
            w_gate = jnp.zeros((d, _LANES), _BF16).at[:, :2 * m_heads].set(w_a_in_b[l, :, main_w:])
            proj, gates = _norm_mod_matmul(xt, g_norm[l, 0][None], sc1, sh1, w_a_in_b, l, col_scale, seq,
                                           w_narrow=w_gate)
            b_gate = jnp.zeros((1, _LANES), _F32).at[0, :2 * m_heads].set(b_a_gate[l])
            mix_in = _mlstm(proj, gates, b_gate, g_a_hnorm[l][None], batch, seq, m_heads, m_dk, m_dv)
            xt = _matmul_residual(mix_in, w_a_out_b, l, xt, gt1, seq)
        else:
            j = l - n_a
            lam_init = 0.8 - 0.6 * math.exp(-0.3 * l)
            q_scale = d_dh ** -0.5 * _LOG2E
            k_w = w_b_q.shape[2]
            if l == n_a:
                kv_shift, kv_scale = (vec(m) for m in jnp.split(kv_mod, 2, axis=-1))
                w_qk = jnp.concatenate([w_b_q_b[j], w_kv_b[:, :k_w]], axis=1)
                q, vt_sh = _qkv_proj(xt, g_norm[l, 0][None], sc1, sh1, g_kv[None], kv_scale, kv_shift,
                                     w_qk, k_w, w_kv_b[:, k_w:].T, q_scale, batch, seq)
                k_sh = q
            else:
                q = _norm_mod_matmul(xt, g_norm[l, 0][None], sc1, sh1, w_b_q_b, j,
                                     jnp.full((1, k_w), q_scale, _F32), seq)
            mix_in = _diff_attention(q, k_sh, d_heads, vt_sh, b_lambda[j], g_b_subln[j][None], batch, seq,
                                     d_heads, d_dh, lam_init)
            xt = _matmul_residual(mix_in, w_b_out_b, j, xt, gt1, seq)
        last = l == depth - 1
        xt = _mlp(xt, g_norm[l, 1][None], sc2, sh2, gt2, w_up_b, w_down_b, l,
                  g_final[None] if last else ones_d, seq, last)
    return xt.reshape(batch, seq, d)
```

```python
import functools
import math

import jax
import jax.numpy as jnp
from jax import lax
from jax.experimental import pallas as pl
from jax.experimental.pallas import tpu as pltpu

_EPS = 1e-6
_F32 = jnp.float32
_BF16 = jnp.bfloat16

_LANES = 128
_SUBLANES = 8
_BF16_ROWS = 16
_VMEM_LIMIT_BYTES = 59 * 1024 * 1024

_MODVEC_TN = 1024
_PROJ_TM = 1024
_IN_PROJ_TN = 2048
_OUT_PROJ_TM = 512
_OUT_PROJ_TN = 2048
_MLP_TM = 1024
_MLP_TF = 1024
_NORM_ROWS = 16
_NORM_UNROLL = 8
_MLSTM_CHUNK = 128
_MLSTM_CHUNKS_PER_STEP = 2
_ATTN_BLOCK = 512
_QKV_TN = 1024

_LOG2E = math.log2(math.e)

_NT_DIMS = (((1,), (1,)), ((), ()))
_TN_DIMS = (((0,), (0,)), ((), ()))


def _params(*semantics):
    return pltpu.CompilerParams(dimension_semantics=semantics, vmem_limit_bytes=_VMEM_LIMIT_BYTES)


def _modvec_kernel(c_ref, w_ref, b_ref, o_ref):
    o_ref[...] = jnp.dot(c_ref[...].astype(_BF16), w_ref[...].astype(_BF16),
                         preferred_element_type=_F32) + b_ref[...]


def _modvec(c_rows, w, b):
    n_l, d, n = w.shape
    tn = _MODVEC_TN
    return pl.pallas_call(
        _modvec_kernel,
        grid=(n_l, n // tn),
        in_specs=[pl.BlockSpec((_SUBLANES, d), lambda l, j: (0, 0)),
                  pl.BlockSpec((None, d, tn), lambda l, j: (l, 0, j)),
                  pl.BlockSpec((None, 1, tn), lambda l, j: (l, 0, j))],
        out_specs=pl.BlockSpec((None, _SUBLANES, tn), lambda l, j: (l, 0, j)),
        out_shape=jax.ShapeDtypeStruct((n_l, _SUBLANES, n), _F32),
        compiler_params=_params("parallel", "parallel"),
        name="modvec",
    )(c_rows, w, b)


def _norm_mod_tile(x_ref, gain, shift, h_ref):
    n_chunks = x_ref.shape[0] // _NORM_ROWS

    def body(r, carry):
        rows = pl.ds(pl.multiple_of(r * _NORM_ROWS, _NORM_ROWS), _NORM_ROWS)
        x = x_ref[rows, :]
        inv = lax.rsqrt(jnp.mean(x * x, axis=-1, keepdims=True) + _EPS)
        h_ref[rows, :] = (x * inv * gain + shift).astype(h_ref.dtype)
        return carry

    lax.fori_loop(0, n_chunks, body, 0, unroll=_NORM_UNROLL)


def _norm_mod_matmul_kernel(x_ref, g_ref, sc_ref, sh_ref, w_ref, cs_ref, *rest, narrow):
    if narrow:
        wn_ref, o_ref, on_ref, h_ref = rest
    else:
        o_ref, h_ref = rest

    @pl.when(pl.program_id(1) == 0)
    def _():
        _norm_mod_tile(x_ref, g_ref[...] * (1.0 + sc_ref[...]), sh_ref[...], h_ref)
        if narrow:
            on_ref[...] = jnp.dot(h_ref[...], wn_ref[...], preferred_element_type=_F32)

    acc = jnp.dot(h_ref[...], w_ref[...], preferred_element_type=_F32)
    o_ref[...] = (acc * cs_ref[...]).astype(o_ref.dtype)


def _norm_mod_matmul(x, g, scale, shift, w, layer, col_scale, seq, w_narrow=None):
    t, d = x.shape
    n = col_scale.shape[1]
    tm = min(_PROJ_TM, seq)
    tn = min(_IN_PROJ_TN, n)
    tiles_per_seq = seq // tm
    narrow = w_narrow is not None
    batch_vec = pl.BlockSpec((None, 1, d), lambda i, j: (i // tiles_per_seq, 0, 0))
    in_specs = [pl.BlockSpec((tm, d), lambda i, j: (i, 0)),
                pl.BlockSpec((1, d), lambda i, j: (0, 0)),
                batch_vec, batch_vec,
                pl.BlockSpec((None, d, tn), lambda i, j: (layer, 0, j)),
                pl.BlockSpec((1, tn), lambda i, j: (0, j))]
    out_specs = [pl.BlockSpec((tm, tn), lambda i, j: (i, j))]
    out_shape = [jax.ShapeDtypeStruct((t, n), _BF16)]
    operands = [x, g, scale, shift, w, col_scale]
    if narrow:
        in_specs.append(pl.BlockSpec((d, _LANES), lambda i, j: (0, 0)))
        out_specs.append(pl.BlockSpec((tm, _LANES), lambda i, j: (i, 0)))
        out_shape.append(jax.ShapeDtypeStruct((t, _LANES), _F32))
        operands.append(w_narrow)
    outs = pl.pallas_call(
        functools.partial(_norm_mod_matmul_kernel, narrow=narrow),
        grid=(t // tm, n // tn),
        in_specs=in_specs,
        out_specs=out_specs,
        out_shape=out_shape,
        scratch_shapes=[pltpu.VMEM((tm, d), _BF16)],
        compiler_params=_params("parallel", "arbitrary"),
        name="norm_mod_matmul",
    )(*operands)
    return outs if narrow else outs[0]


def _matmul_residual_kernel(a_ref, w_ref, x_ref, gate_ref, o_ref):
    acc = jnp.dot(a_ref[...], w_ref[...], preferred_element_type=_F32)
    o_ref[...] = x_ref[...] + gate_ref[...] * acc


def _matmul_residual(a, w, layer, x, gate, seq):
    t, k = a.shape
    d = w.shape[2]
    tm = min(_OUT_PROJ_TM, seq)
    tn = min(_OUT_PROJ_TN, d)
    tiles_per_seq = seq // tm
    return pl.pallas_call(
        _matmul_residual_kernel,
        grid=(t // tm, d // tn),
        in_specs=[pl.BlockSpec((tm, k), lambda i, j: (i, 0)),
                  pl.BlockSpec((None, k, tn), lambda i, j: (layer, 0, j)),
                  pl.BlockSpec((tm, tn), lambda i, j: (i, j)),
                  pl.BlockSpec((None, 1, tn), lambda i, j: (i // tiles_per_seq, 0, j))],
        out_specs=pl.BlockSpec((tm, tn), lambda i, j: (i, j)),
        out_shape=jax.ShapeDtypeStruct((t, d), _F32),
        compiler_params=_params("parallel", "parallel"),
        name="matmul_residual",
    )(a, w, x, gate)


def _mlp_kernel(x_ref, g_ref, sc_ref, sh_ref, gate_ref, wu_ref, wd_ref, gf_ref, o_ref, h_ref, *, final_norm):
    f = pl.program_id(1)

    @pl.when(f == 0)
    def _():
        _norm_mod_tile(x_ref, g_ref[...] * (1.0 + sc_ref[...]), sh_ref[...], h_ref)
        o_ref[...] = jnp.zeros_like(o_ref)

    u = jnp.dot(h_ref[...], wu_ref[...], preferred_element_type=_F32)
    u = jnp.square(jnp.maximum(u, 0.0)).astype(_BF16)
    o_ref[...] += jnp.dot(u, wd_ref[...], preferred_element_type=_F32)

    @pl.when(f == pl.num_programs(1) - 1)
    def _():
        gate = gate_ref[...]
        gain = gf_ref[...]
        slab = _NORM_ROWS * _NORM_UNROLL

        def residual(rows):
            return x_ref[rows, :] + gate * o_ref[rows, :]

        def body(r, carry):
            base = pl.multiple_of(r * slab, slab)
            chunks = [pl.ds(base + c * _NORM_ROWS, _NORM_ROWS) for c in range(_NORM_UNROLL)]
            if final_norm:
                inv = [lax.rsqrt(jnp.mean(jnp.square(residual(rows)), axis=-1, keepdims=True) + _EPS)
                       for rows in chunks]
                for rows, scale in zip(chunks, inv):
                    o_ref[rows, :] = residual(rows) * scale * gain
            else:
                for rows in chunks:
                    o_ref[rows, :] = residual(rows)
            return carry

        lax.fori_loop(0, x_ref.shape[0] // slab, body, 0)


def _mlp(x, g, scale, shift, gate, w_up, w_down, layer, g_final, seq, final_norm):
    t, d = x.shape
    d_ff = w_up.shape[2]
    tm = min(_MLP_TM, seq)
    tf = min(_MLP_TF, d_ff)
    tiles_per_seq = seq // tm
    batch_vec = pl.BlockSpec((None, 1, d), lambda i, f: (i // tiles_per_seq, 0, 0))
    return pl.pallas_call(
        functools.partial(_mlp_kernel, final_norm=final_norm),
        grid=(t // tm, d_ff // tf),
        in_specs=[pl.BlockSpec((tm, d), lambda i, f: (i, 0)),
                  pl.BlockSpec((1, d), lambda i, f: (0, 0)),
                  batch_vec, batch_vec, batch_vec,
                  pl.BlockSpec((None, d, tf), lambda i, f: (layer, 0, f)),
                  pl.BlockSpec((None, tf, d), lambda i, f: (layer, f, 0)),
                  pl.BlockSpec((1, d), lambda i, f: (0, 0))],
        out_specs=pl.BlockSpec((tm, d), lambda i, f: (i, 0)),
        out_shape=jax.ShapeDtypeStruct((t, d), _F32),
        scratch_shapes=[pltpu.VMEM((tm, d), _BF16)],
        compiler_params=_params("parallel", "arbitrary"),
        name="mlp",
    )(x, g, scale, shift, gate, w_up, w_down, g_final)


def _log_sigmoid(x):
    return jnp.minimum(x, 0.0) - jnp.log1p(jnp.exp(-jnp.abs(x)))


def _mlstm_kernel(q_ref, k_ref, v_ref, og_ref, gt_ref, bg_ref, gh_ref, out_ref, c_ref, m_ref,
                  *, heads, dk, dv):
    @pl.when(pl.program_id(1) == 0)
    def _():
        c_ref[...] = jnp.zeros_like(c_ref)
        m_ref[...] = jnp.zeros_like(m_ref)

    chunk = _MLSTM_CHUNK
    src = lax.broadcasted_iota(jnp.int32, (chunk, chunk), 0)
    dst = lax.broadcasted_iota(jnp.int32, (chunk, chunk), 1)
    visible = src <= dst
    for sub in range(q_ref.shape[0] // chunk):
        rows = slice(sub * chunk, (sub + 1) * chunk)
        gates = gt_ref[rows, :] + bg_ref[...]
        b_all = jnp.dot((dst <= src).astype(_F32), _log_sigmoid(gates), preferred_element_type=_F32,
                        precision=lax.Precision.HIGHEST)
        _mlstm_chunk(q_ref, k_ref, v_ref, og_ref, gh_ref, out_ref, c_ref, m_ref, rows, gates, b_all, visible,
                     heads, dk, dv)


def _mlstm_chunk(q_ref, k_ref, v_ref, og_ref, gh_ref, out_ref, c_ref, m_ref, rows, gates, b_all, visible,
                 heads, dk, dv):
    chunk = _MLSTM_CHUNK
    gates_t = gates.T
    b_all_t = b_all.T
    for h in range(heads):
        b_row = b_all_t[heads + h:heads + h + 1, :]
        src_col = gates[:, h:h + 1] - b_all[:, heads + h:heads + h + 1]
        b_last = b_row[:, chunk - 1:chunk]
        m_prev = m_ref[h]
        c_aug = c_ref[h]
        qh = q_ref[rows, h * dk:(h + 1) * dk]
        kh = k_ref[rows, h * dk:(h + 1) * dk]
        vh = v_ref[rows, h * dv:(h + 1) * dv]

        dmat_t = jnp.where(visible, b_row + src_col, -jnp.inf)
        inter = b_row + m_prev
        m_j = jnp.maximum(inter, jnp.max(dmat_t, axis=0, keepdims=True))
        w_inter = jnp.exp(inter - m_j)
        s_t = lax.dot_general(kh, qh, _NT_DIMS, preferred_element_type=_F32) * jnp.exp(dmat_t - m_j)
        state_q = lax.dot_general(c_aug.astype(_BF16), qh, _NT_DIMS, preferred_element_type=_F32)
        num_t = (w_inter * state_q[:dv, :]
                 + lax.dot_general(vh, s_t.astype(_BF16), _TN_DIMS, preferred_element_type=_F32))
        den = w_inter * state_q[dv:dv + 1, :] + jnp.sum(s_t, axis=0, keepdims=True)
        h_t = num_t * (1.0 / jnp.maximum(jnp.abs(den), jnp.exp(-m_j)))

        dec = b_last + src_col
        m_new = jnp.maximum(b_last + m_prev, jnp.max(dec, axis=0, keepdims=True))
        kw = kh.astype(_F32) * jnp.exp(dec - m_new)
        carry_scale = jnp.exp(b_last + m_prev - m_new)
        c_ref[h, :dv, :] = carry_scale * c_aug[:dv, :] + lax.dot_general(
            vh, kw.astype(_BF16), _TN_DIMS, preferred_element_type=_F32)
        c_ref[h, dv:dv + 1, :] = carry_scale * c_aug[dv:dv + 1, :] + jnp.sum(kw, axis=0, keepdims=True)
        m_ref[h] = m_new

        y_t = h_t * lax.rsqrt(jnp.mean(h_t * h_t, axis=0, keepdims=True) + _EPS)
        og = og_ref[rows, h * dv:(h + 1) * dv].astype(_F32)
        out_ref[rows, h * dv:(h + 1) * dv] = (y_t.T * gh_ref[:, h * dv:(h + 1) * dv]
                                              * jax.nn.sigmoid(og)).astype(out_ref.dtype)


def _mlstm(proj, gates, b_gate, g_hnorm, batch, seq, heads, dk, dv):
    t = proj.shape[0]
    chunk = _MLSTM_CHUNK * _MLSTM_CHUNKS_PER_STEP
    n_chunks = seq // chunk
    qk_w, v_w = heads * dk, heads * dv
    assert v_w == 2 * qk_w

    def rows(b, c):
        return b * n_chunks + c

    return pl.pallas_call(
        functools.partial(_mlstm_kernel, heads=heads, dk=dk, dv=dv),
        grid=(batch, n_chunks),
        in_specs=[pl.BlockSpec((chunk, qk_w), lambda b, c: (rows(b, c), 0)),
                  pl.BlockSpec((chunk, qk_w), lambda b, c: (rows(b, c), 1)),
                  pl.BlockSpec((chunk, v_w), lambda b, c: (rows(b, c), 1)),
                  pl.BlockSpec((chunk, v_w), lambda b, c: (rows(b, c), 2)),
                  pl.BlockSpec((chunk, _LANES), lambda b, c: (rows(b, c), 0)),
                  pl.BlockSpec((1, _LANES), lambda b, c: (0, 0)),
                  pl.BlockSpec((1, v_w), lambda b, c: (0, 0))],
        out_specs=pl.BlockSpec((chunk, v_w), lambda b, c: (rows(b, c), 0)),
        out_shape=jax.ShapeDtypeStruct((t, v_w), _BF16),
        scratch_shapes=[pltpu.VMEM((heads, dv + _BF16_ROWS, dk), _F32),
                        pltpu.VMEM((heads, 1, 1), _F32)],
        compiler_params=_params("parallel", "arbitrary"),
        name="mlstm",
    )(proj, proj, proj, proj, gates, b_gate, g_hnorm)


def _qkv_proj_kernel(x_ref, g1_ref, sc1_ref, sh1_ref, gkv_ref, sckv_ref, shkv_ref, wqk_ref, wvt_ref,
                     qk_ref, vt_ref, h1_ref, hk_ref, *, n_q, n_k, q_scale):
    j = pl.program_id(1)

    @pl.when(j == 0)
    def _():
        gain1 = g1_ref[...] * (1.0 + sc1_ref[...])
        shift1 = sh1_ref[...]
        gain_kv = gkv_ref[...] * (1.0 + sckv_ref[...])
        shift_kv = shkv_ref[...]
        n_chunks = x_ref.shape[0] // _NORM_ROWS

        def body(r, carry):
            rows = pl.ds(pl.multiple_of(r * _NORM_ROWS, _NORM_ROWS), _NORM_ROWS)
            x = x_ref[rows, :]
            xn = x * lax.rsqrt(jnp.mean(x * x, axis=-1, keepdims=True) + _EPS)
            h1_ref[rows, :] = (xn * gain1 + shift1).astype(h1_ref.dtype)
            hk_ref[rows, :] = (xn * gain_kv + shift_kv).astype(hk_ref.dtype)
            return carry

        lax.fori_loop(0, n_chunks, body, 0, unroll=_NORM_UNROLL)

    @pl.when(j < n_q)
    def _():
        acc = jnp.dot(h1_ref[...], wqk_ref[...], preferred_element_type=_F32)
        qk_ref[...] = (acc * q_scale).astype(qk_ref.dtype)

    @pl.when(jnp.logical_and(j >= n_q, j < n_q + n_k))
    def _():
        qk_ref[...] = jnp.dot(hk_ref[...], wqk_ref[...], preferred_element_type=_F32).astype(qk_ref.dtype)

    @pl.when(j >= n_q + n_k)
    def _():
        vt_ref[...] = lax.dot_general(wvt_ref[...], hk_ref[...], _NT_DIMS,
                                      preferred_element_type=_F32).astype(vt_ref.dtype)


def _qkv_proj(x, g1, sc1, sh1, gkv, sckv, shkv, wqk, n_qc, wvt, q_scale, batch, seq):
    t, d = x.shape
    n_qkc, n_vc = wqk.shape[1], wvt.shape[0]
    tm = min(_PROJ_TM, seq)
    tn = _QKV_TN
    tiles_per_seq = seq // tm
    n_q, n_qk, n_v = n_qc // tn, n_qkc // tn, n_vc // tn

    def qk_col(j):
        return jnp.minimum(j, n_qk - 1)

    def vt_row(j):
        return jnp.maximum(j - n_qk, 0)

    batch_vec = pl.BlockSpec((None, 1, d), lambda i, j: (i // tiles_per_seq, 0, 0))
    row_vec = pl.BlockSpec((1, d), lambda i, j: (0, 0))
    return pl.pallas_call(
        functools.partial(_qkv_proj_kernel, n_q=n_q, n_k=n_qk - n_q, q_scale=q_scale),
        grid=(t // tm, n_qk + n_v),
        in_specs=[pl.BlockSpec((tm, d), lambda i, j: (i, 0)),
                  row_vec, batch_vec, batch_vec, row_vec, batch_vec, batch_vec,
                  pl.BlockSpec((d, tn), lambda i, j: (0, qk_col(j))),
                  pl.BlockSpec((tn, d), lambda i, j: (vt_row(j), 0))],
        out_specs=[pl.BlockSpec((tm, tn), lambda i, j: (i, qk_col(j))),
                   pl.BlockSpec((None, tn, tm), lambda i, j: (i // tiles_per_seq, vt_row(j), i % tiles_per_seq))],
        out_shape=[jax.ShapeDtypeStruct((t, n_qkc), _BF16),
                   jax.ShapeDtypeStruct((batch, n_vc, seq), _BF16)],
        scratch_shapes=[pltpu.VMEM((tm, d), _BF16), pltpu.VMEM((tm, d), _BF16)],
        compiler_params=_params("parallel", "arbitrary"),
        name="qkv_proj",
    )(x, g1, sc1, sh1, gkv, sckv, shkv, wqk, wvt)


def _attn_kernel(q_ref, k_ref, vt_ref, slope_ref, lam_ref, g_ref, o_ref, qa_ref, kpos_ref, s_ref, bmax_ref, m_ref,
                 acc_ref, *, dh, lam_init, n_blk):
    qi = pl.program_id(2)
    blk = q_ref.shape[0]
    slope2 = slope_ref[:, 0:1] * _LOG2E
    dv = vt_ref.shape[0]
    slot = qi & 1

    def build_head_constants():
        lane = lax.broadcasted_iota(jnp.int32, (blk, dh), 1)
        key_bias = lax.broadcasted_iota(jnp.int32, (blk, dh), 0).astype(_F32) * slope2
        part_hi = key_bias.astype(_BF16).astype(_F32)
        part_mid = (key_bias - part_hi).astype(_BF16).astype(_F32)
        part_lo = key_bias - part_hi - part_mid
        kpos_ref[...] = jnp.where(lane == 0, part_hi, jnp.where(lane == 1, part_mid,
                                                                jnp.where(lane == 2, part_lo, 0.0))).astype(_BF16)
        for s_idx in range(2):
            qa_ref[s_idx, :, dh:] = jnp.where(lane < 3, 1.0, 0.0).astype(_BF16)

    def finish_block(done_slot):
        lp = lam_ref[...]
        lam = (jnp.exp(jnp.sum(lp[0:1, :] * lp[1:2, :], axis=-1, keepdims=True))
               - jnp.exp(jnp.sum(lp[2:3, :] * lp[3:4, :], axis=-1, keepdims=True)) + lam_init)
        ot = (acc_ref[done_slot, 0, :dv, :] * (1.0 / acc_ref[done_slot, 0, dv:dv + 1, :])
              - lam * (acc_ref[done_slot, 1, :dv, :] * (1.0 / acc_ref[done_slot, 1, dv:dv + 1, :])))
        ot = ot * lax.rsqrt(jnp.mean(ot * ot, axis=0, keepdims=True) + _EPS)
        o_ref[...] = (ot.T * g_ref[...] * (1.0 - lam_init)).astype(o_ref.dtype)

    def start_block():
        for s_idx in range(2):
            qa_ref[s_idx, :, :dh] = q_ref[:, s_idx * dh:(s_idx + 1) * dh]
        acc_ref[slot] = jnp.zeros(acc_ref.shape[1:], _F32)
        m_ref[...] = jnp.full_like(m_ref, -jnp.inf)
        scores(qi, True, 0)

    def scores(kj, masked, buf):
        keys = pl.ds(pl.multiple_of(kj * blk, blk), blk)
        for s_idx in range(2):
            k_cat = jnp.concatenate([k_ref[keys, s_idx * dh:(s_idx + 1) * dh], kpos_ref[...]], axis=1)
            st = lax.dot_general(k_cat, qa_ref[s_idx], _NT_DIMS, preferred_element_type=_F32)
            if masked:
                key = lax.broadcasted_iota(jnp.int32, (blk, blk), 0)
                qry = lax.broadcasted_iota(jnp.int32, (blk, blk), 1)
                st = jnp.where(key <= qry, st, -jnp.inf)
            s_ref[buf, s_idx] = st
            bmax_ref[buf, s_idx] = jnp.max(st, axis=0, keepdims=True)

    def accumulate(kj, buf):
        keys = pl.ds(pl.multiple_of(kj * blk, blk), blk)
        vt_aug = jnp.concatenate([vt_ref[:, keys], jnp.ones((_BF16_ROWS, blk), _BF16)], axis=0)
        block_bias = slope2 * jnp.full((1, 1), (kj - qi) * blk, jnp.int32).astype(_F32)
        for s_idx in range(2):
            m_old = m_ref[s_idx]
            m_new = jnp.maximum(m_old, bmax_ref[buf, s_idx] + block_bias)
            pt = jnp.exp2(s_ref[buf, s_idx] - (m_new - block_bias))
            acc_ref[slot, s_idx] = acc_ref[slot, s_idx] * jnp.exp2(m_old - m_new) + jnp.dot(
                vt_aug, pt.astype(_BF16), preferred_element_type=_F32)
            m_ref[s_idx] = m_new

    def block_of(t):
        return jnp.where(t == 0, qi, t - 1)

    def pair(u):
        t = 2 * u
        scores(t, False, 1)
        accumulate(block_of(t), 0)
        scores(t + 1, False, 0)
        accumulate(t, 1)

    def two_pairs(u, carry):
        pair(2 * u)
        pair(2 * u + 1)
        return carry

    @pl.when(qi == 0)
    def _():
        build_head_constants()
        start_block()

    @pl.when(jnp.logical_and(qi > 0, qi < n_blk))
    def _():
        finish_block(1 - slot)
        start_block()

    @pl.when(qi == n_blk)
    def _():
        finish_block(1 - slot)

    @pl.when(qi < n_blk)
    def _():
        n_pairs = lax.shift_right_logical(qi, 1)
        lax.fori_loop(0, lax.shift_right_logical(n_pairs, 1), two_pairs, 0)

        @pl.when((n_pairs & 1) == 1)
        def _():
            pair(n_pairs - 1)

        t_rest = 2 * n_pairs
        one_more = (qi & 1) == 1

        @pl.when(one_more)
        def _():
            scores(t_rest, False, 1)
            accumulate(block_of(t_rest), 0)
            accumulate(t_rest, 1)

        @pl.when(jnp.logical_not(one_more))
        def _():
            accumulate(block_of(t_rest), 0)


def _diff_attention(q, k, k_head0, vt, lam_p, g_subln, batch, seq, heads, dh, lam_init):
    t = q.shape[0]
    dv = 2 * dh
    blk = min(_ATTN_BLOCK, seq)
    n_blk = seq // blk
    slopes = jnp.asarray([2.0 ** (-8.0 * (h + 1) / heads) for h in range(heads)], _F32)
    slopes = jnp.broadcast_to(slopes[:, None, None], (heads, 1, _LANES))
    def q_rows(b, i):
        return b * n_blk + jnp.minimum(i, n_blk - 1)

    def out_rows(b, i):
        return b * n_blk + jnp.maximum(i - 1, 0)

    return pl.pallas_call(
        functools.partial(_attn_kernel, dh=dh, lam_init=lam_init, n_blk=n_blk),
        grid=(batch, heads, n_blk + 1),
        in_specs=[pl.BlockSpec((blk, dv), lambda b, h, i: (q_rows(b, i), h)),
                  pl.BlockSpec((seq, dv), lambda b, h, i: (b, k_head0 + h)),
                  pl.BlockSpec((None, dv, seq), lambda b, h, i: (b, h, 0)),
                  pl.BlockSpec((None, 1, _LANES), lambda b, h, i: (h, 0, 0)),
                  pl.BlockSpec((4, dh), lambda b, h, i: (0, 0)),
                  pl.BlockSpec((1, dv), lambda b, h, i: (0, h))],
        out_specs=pl.BlockSpec((blk, dv), lambda b, h, i: (out_rows(b, i), h)),
        out_shape=jax.ShapeDtypeStruct((t, heads * dv), _BF16),
        scratch_shapes=[pltpu.VMEM((2, blk, 2 * dh), _BF16),
                        pltpu.VMEM((blk, dh), _BF16),
                        pltpu.VMEM((2, 2, blk, blk), _F32),
                        pltpu.VMEM((2, 2, 1, blk), _F32),
                        pltpu.VMEM((2, 1, blk), _F32),
                        pltpu.VMEM((2, 2, dv + _BF16_ROWS, blk), _F32)],
        compiler_params=_params("parallel", "parallel", "arbitrary"),
        name="diff_attention",
    )(q, k, vt, slopes, lam_p, g_subln)


def kernel(x, c, w_mod, b_mod, g_norm, w_up, w_down, w_a_in, b_a_gate, g_a_hnorm, w_a_out, g_kv, w_kv_mod,
           b_kv_mod, w_kv, w_b_q, b_lambda, g_b_subln, w_b_out, g_final):
    batch, seq, d = x.shape
    depth = w_mod.shape[0]
    n_a = w_a_in.shape[0]
    m_heads = b_a_gate.shape[1] // 2
    m_dv = g_a_hnorm.shape[1] // m_heads
    m_dk = (w_a_in.shape[2] - 2 * m_heads * m_dv - 2 * m_heads) // (2 * m_heads)
    d_dh = b_lambda.shape[2]
    d_heads = w_b_q.shape[2] // (2 * d_dh)
    t = batch * seq

    xt = x.reshape(t, d)
    c_rows = jnp.zeros((_SUBLANES, d), _F32).at[:batch].set(c)

    mod = _modvec(c_rows, w_mod, b_mod.reshape(depth, 1, 6 * d))[:, :batch]
    kv_mod = _modvec(c_rows, w_kv_mod[None], b_kv_mod.reshape(1, 1, 2 * d))[0, :batch]

    def vec(v):
        return v.reshape(batch, 1, d)

    w_up_b, w_down_b = w_up.astype(_BF16), w_down.astype(_BF16)
    w_a_in_b, w_a_out_b = w_a_in.astype(_BF16), w_a_out.astype(_BF16)
    w_b_q_b, w_b_out_b, w_kv_b = w_b_q.astype(_BF16), w_b_out.astype(_BF16), w_kv.astype(_BF16)

    ones_d = jnp.ones((1, d), _F32)
    for l in range(depth):
        sh1, sc1, gt1, sh2, sc2, gt2 = (vec(m) for m in jnp.split(mod[l], 6, axis=-1))
        if l < n_a:
            qk_w, v_w = m_heads * m_dk, m_heads * m_dv
            main_w = 2 * qk_w + 2 * v_w
            col_scale = jnp.concatenate([jnp.ones((qk_w,), _F32), jnp.full((qk_w,), m_dk ** -0.5, _F32),
                                         jnp.ones((2 * v_w,), _F32)]).reshape(1, main_w)
            w_gate = jnp.zeros((d, _LANES), _BF16).at[:, :2 * m_heads].set(w_a_in_b[l, :, main_w:])
            proj, gates = _norm_mod_matmul(xt, g_norm[l, 0][None], sc1, sh1, w_a_in_b, l, col_scale, seq,
                                           w_narrow=w_gate)
            b_gate = jnp.zeros((1, _LANES), _F32).at[0, :2 * m_heads].set(b_a_gate[l])
            mix_in = _mlstm(proj, gates, b_gate, g_a_hnorm[l][None], batch, seq, m_heads, m_dk, m_dv)
            xt = _matmul_residual(mix_in, w_a_out_b, l, xt, gt1, seq)
        else:
            j = l - n_a
            lam_init = 0.8 - 0.6 * math.exp(-0.3 * l)
            q_scale = d_dh ** -0.5 * _LOG2E
            k_w = w_b_q.shape[2]
            if l == n_a:
                kv_shift, kv_scale = (vec(m) for m in jnp.split(kv_mod, 2, axis=-1))
                w_qk = jnp.concatenate([w_b_q_b[j], w_kv_b[:, :k_w]], axis=1)
                q, vt_sh = _qkv_proj(xt, g_norm[l, 0][None], sc1, sh1, g_kv[None], kv_scale, kv_shift,
                                     w_qk, k_w, w_kv_b[:, k_w:].T, q_scale, batch, seq)
                k_sh = q
            else:
                q = _norm_mod_matmul(xt, g_norm[l, 0][None], sc1, sh1, w_b_q_b, j,
                                     jnp.full((1, k_w), q_scale, _F32), seq)
            mix_in = _diff_attention(q, k_sh, d_heads, vt_sh, b_lambda[j], g_b_subln[j][None], batch, seq,
                                     d_heads, d_dh, lam_init)
            xt = _matmul_residual(mix_in, w_b_out_b, j, xt, gt1, seq)
        last = l == depth - 1
        xt = _mlp(xt, g_norm[l, 1][None], sc2, sh2, gt2, w_up_b, w_down_b, l,
                  g_final[None] if last else ones_d, seq, last)
    return xt.reshape(batch, seq, d)
```

```python
import functools
import math

import jax
import jax.numpy as jnp
from jax import lax
from jax.experimental import pallas as pl
from jax.experimental.pallas import tpu as pltpu

_EPS = 1e-6
_F32 = jnp.float32
_BF16 = jnp.bfloat16

_LANES = 128
_SUBLANES = 8
_BF16_ROWS = 16
_VMEM_LIMIT_BYTES = 59 * 1024 * 1024

_MODVEC_TN = 1024
_PROJ_TM = 1024
_IN_PROJ_TN = 2048
_OUT_PROJ_TM = 512
_OUT_PROJ_TN = 2048
_MLP_TM = 1024
_MLP_TF = 1024
_NORM_ROWS = 16
_NORM_UNROLL = 8
_MLSTM_CHUNK = 128
_MLSTM_CHUNKS_PER_STEP = 2
_ATTN_BLOCK = 512
_QKV_TN = 1024

_LOG2E = math.log2(math.e)

_NT_DIMS = (((1,), (1,)), ((), ()))
_TN_DIMS = (((0,), (0,)), ((), ()))


def _params(*semantics):
    return pltpu.CompilerParams(dimension_semantics=semantics, vmem_limit_bytes=_VMEM_LIMIT_BYTES)


def _modvec_kernel(c_ref, w_ref, b_ref, o_ref):
    o_ref[...] = jnp.dot(c_ref[...].astype(_BF16), w_ref[...].astype(_BF16),
                         preferred_element_type=_F32) + b_ref[...]


def _modvec(c_rows, w, b):
    n_l, d, n = w.shape
    tn = _MODVEC_TN
    return pl.pallas_call(
        _modvec_kernel,
        grid=(n_l, n // tn),
        in_specs=[pl.BlockSpec((_SUBLANES, d), lambda l, j: (0, 0)),
                  pl.BlockSpec((None, d, tn), lambda l, j: (l, 0, j)),
                  pl.BlockSpec((None, 1, tn), lambda l, j: (l, 0, j))],
        out_specs=pl.BlockSpec((None, _SUBLANES, tn), lambda l, j: (l, 0, j)),
        out_shape=jax.ShapeDtypeStruct((n_l, _SUBLANES, n), _F32),
        compiler_params=_params("parallel", "parallel"),
        name="modvec",
    )(c_rows, w, b)


def _norm_mod_tile(x_ref, gain, shift, h_ref):
    n_chunks = x_ref.shape[0] // _NORM_ROWS

    def body(r, carry):
        rows = pl.ds(pl.multiple_of(r * _NORM_ROWS, _NORM_ROWS), _NORM_ROWS)
        x = x_ref[rows, :]
        inv = lax.rsqrt(jnp.mean(x * x, axis=-1, keepdims=True) + _EPS)
        h_ref[rows, :] = (x * inv * gain + shift).astype(h_ref.dtype)
        return carry

    lax.fori_loop(0, n_chunks, body, 0, unroll=_NORM_UNROLL)


def _norm_mod_matmul_kernel(x_ref, g_ref, sc_ref, sh_ref, w_ref, cs_ref, *rest, narrow):
    if narrow:
        wn_ref, o_ref, on_ref, h_ref = rest
    else:
        o_ref, h_ref = rest

    @pl.when(pl.program_id(1) == 0)
    def _():
        _norm_mod_tile(x_ref, g_ref[...] * (1.0 + sc_ref[...]), sh_ref[...], h_ref)
        if narrow:
            on_ref[...] = jnp.dot(h_ref[...], wn_ref[...], preferred_element_type=_F32)

    acc = jnp.dot(h_ref[...], w_ref[...], preferred_element_type=_F32)
    o_ref[...] = (acc * cs_ref[...]).astype(o_ref.dtype)


def _norm_mod_matmul(x, g, scale, shift, w, layer, col_scale, seq, w_narrow=None):
    t, d = x.shape
    n = col_scale.shape[1]
    tm = min(_PROJ_TM, seq)
    tn = min(_IN_PROJ_TN, n)
    tiles_per_seq = seq // tm
    narrow = w_narrow is not None
    batch_vec = pl.BlockSpec((None, 1, d), lambda i, j: (i // tiles_per_seq, 0, 0))
    in_specs = [pl.BlockSpec((tm, d), lambda i, j: (i, 0)),
                pl.BlockSpec((1, d), lambda i, j: (0, 0)),
                batch_vec, batch_vec,
                pl.BlockSpec((None, d, tn), lambda i, j: (layer, 0, j)),
                pl.BlockSpec((1, tn), lambda i, j: (0, j))]
    out_specs = [pl.BlockSpec((tm, tn), lambda i, j: (i, j))]
    out_shape = [jax.ShapeDtypeStruct((t, n), _BF16)]
    operands = [x, g, scale, shift, w, col_scale]
    if narrow:
        in_specs.append(pl.BlockSpec((d, _LANES), lambda i, j: (0, 0)))
        out_specs.append(pl.BlockSpec((tm, _LANES), lambda i, j: (i, 0)))
        out_shape.append(jax.ShapeDtypeStruct((t, _LANES), _F32))
        operands.append(w_narrow)
    outs = pl.pallas_call(
        functools.partial(_norm_mod_matmul_kernel, narrow=narrow),
        grid=(t // tm, n // tn),
        in_specs=in_specs,
        out_specs=out_specs,
        out_shape=out_shape,
        scratch_shapes=[pltpu.VMEM((tm, d), _BF16)],
        compiler_params=_params("parallel", "arbitrary"),
        name="norm_mod_matmul",
    )(*operands)
    return outs if narrow else outs[0]


def _matmul_residual_kernel(a_ref, w_ref, x_ref, gate_ref, o_ref):
    acc = jnp.dot(a_ref[...], w_ref[...], preferred_element_type=_F32)
    o_ref[...] = x_ref[...] + gate_ref[...] * acc


def _matmul_residual(a, w, layer, x, gate, seq):
    t, k = a.shape
    d = w.shape[2]
    tm = min(_OUT_PROJ_TM, seq)
    tn = min(_OUT_PROJ_TN, d)
    tiles_per_seq = seq // tm
    return pl.pallas_call(
        _matmul_residual_kernel,
        grid=(t // tm, d // tn),
        in_specs=[pl.BlockSpec((tm, k), lambda i, j: (i, 0)),
                  pl.BlockSpec((None, k, tn), lambda i, j: (layer, 0, j)),
                  pl.BlockSpec((tm, tn), lambda i, j: (i, j)),
                  pl.BlockSpec((None, 1, tn), lambda i, j: (i // tiles_per_seq, 0, j))],
        out_specs=pl.BlockSpec((tm, tn), lambda i, j: (i, j)),
        out_shape=jax.ShapeDtypeStruct((t, d), _F32),
        compiler_params=_params("parallel", "parallel"),
        name="matmul_residual",
    )(a, w, x, gate)


def _mlp_kernel(x_ref, g_ref, sc_ref, sh_ref, gate_ref, wu_ref, wd_ref, gf_ref, o_ref, h_ref, *, final_norm):
    f = pl.program_id(1)

    @pl.when(f == 0)
    def _():
        _norm_mod_tile(x_ref, g_ref[...] * (1.0 + sc_ref[...]), sh_ref[...], h_ref)
        o_ref[...] = jnp.zeros_like(o_ref)

    u = jnp.dot(h_ref[...], wu_ref[...], preferred_element_type=_F32)
    u = jnp.square(jnp.maximum(u, 0.0)).astype(_BF16)
    o_ref[...] += jnp.dot(u, wd_ref[...], preferred_element_type=_F32)

    @pl.when(f == pl.num_programs(1) - 1)
    def _():
        gate = gate_ref[...]
        gain = gf_ref[...]
        slab = _NORM_ROWS * _NORM_UNROLL

        def residual(rows):
            return x_ref[rows, :] + gate * o_ref[rows, :]

        def body(r, carry):
            base = pl.multiple_of(r * slab, slab)
            chunks = [pl.ds(base + c * _NORM_ROWS, _NORM_ROWS) for c in range(_NORM_UNROLL)]
            if final_norm:
                inv = [lax.rsqrt(jnp.mean(jnp.square(residual(rows)), axis=-1, keepdims=True) + _EPS)
                       for rows in chunks]
                for rows, scale in zip(chunks, inv):
                    o_ref[rows, :] = residual(rows) * scale * gain
            else:
                for rows in chunks:
                    o_ref[rows, :] = residual(rows)
            return carry

        lax.fori_loop(0, x_ref.shape[0] // slab, body, 0)


def _mlp(x, g, scale, shift, gate, w_up, w_down, layer, g_final, seq, final_norm):
    t, d = x.shape
    d_ff = w_up.shape[2]
    tm = min(_MLP_TM, seq)
    tf = min(_MLP_TF, d_ff)
    tiles_per_seq = seq // tm
    batch_vec = pl.BlockSpec((None, 1, d), lambda i, f: (i // tiles_per_seq, 0, 0))
    return pl.pallas_call(
        functools.partial(_mlp_kernel, final_norm=final_norm),
        grid=(t // tm, d_ff // tf),
        in_specs=[pl.BlockSpec((tm, d), lambda i, f: (i, 0)),
                  pl.BlockSpec((1, d), lambda i, f: (0, 0)),
                  batch_vec, batch_vec, batch_vec,
                  pl.BlockSpec((None, d, tf), lambda i, f: (layer, 0, f)),
                  pl.BlockSpec((None, tf, d), lambda i, f: (layer, f, 0)),
                  pl.BlockSpec((1, d), lambda i, f: (0, 0))],
        out_specs=pl.BlockSpec((tm, d), lambda i, f: (i, 0)),
        out_shape=jax.ShapeDtypeStruct((t, d), _F32),
        scratch_shapes=[pltpu.VMEM((tm, d), _BF16)],
        compiler_params=_params("parallel", "arbitrary"),
        name="mlp",
    )(x, g, scale, shift, gate, w_up, w_down, g_final)


def _log_sigmoid(x):
    return jnp.minimum(x, 0.0) - jnp.log1p(jnp.exp(-jnp.abs(x)))


def _mlstm_kernel(q_ref, k_ref, v_ref, og_ref, gt_ref, bg_ref, gh_ref, out_ref, c_ref, m_ref,
                  *, heads, dk, dv):
    @pl.when(pl.program_id(1) == 0)
    def _():
        c_ref[...] = jnp.zeros_like(c_ref)
        m_ref[...] = jnp.zeros_like(m_ref)

    chunk = _MLSTM_CHUNK
    src = lax.broadcasted_iota(jnp.int32, (chunk, chunk), 0)
    dst = lax.broadcasted_iota(jnp.int32, (chunk, chunk), 1)
    visible = src <= dst
    for sub in range(q_ref.shape[0] // chunk):
        rows = slice(sub * chunk, (sub + 1) * chunk)
        gates = gt_ref[rows, :] + bg_ref[...]
        b_all = jnp.dot((dst <= src).astype(_F32), _log_sigmoid(gates), preferred_element_type=_F32,
                        precision=lax.Precision.HIGHEST)
        _mlstm_chunk(q_ref, k_ref, v_ref, og_ref, gh_ref, out_ref, c_ref, m_ref, rows, gates, b_all, visible,
                     heads, dk, dv)


def _mlstm_chunk(q_ref, k_ref, v_ref, og_ref, gh_ref, out_ref, c_ref, m_ref, rows, gates, b_all, visible,
                 heads, dk, dv):
    chunk = _MLSTM_CHUNK
    gates_t = gates.T
    b_all_t = b_all.T
    for h in range(heads):
        b_row = b_all_t[heads + h:heads + h + 1, :]
        src_col = gates[:, h:h + 1] - b_all[:, heads + h:heads + h + 1]
        b_last = b_row[:, chunk - 1:chunk]
        m_prev = m_ref[h]
        c_aug = c_ref[h]
        qh = q_ref[rows, h * dk:(h + 1) * dk]
        kh = k_ref[rows, h * dk:(h + 1) * dk]
        vh = v_ref[rows, h * dv:(h + 1) * dv]

        dmat_t = jnp.where(visible, b_row + src_col, -jnp.inf)
        inter = b_row + m_prev
        m_j = jnp.maximum(inter, jnp.max(dmat_t, axis=0, keepdims=True))
        w_inter = jnp.exp(inter - m_j)
        s_t = lax.dot_general(kh, qh, _NT_DIMS, preferred_element_type=_F32) * jnp.exp(dmat_t - m_j)
        state_q = lax.dot_general(c_aug.astype(_BF16), qh, _NT_DIMS, preferred_element_type=_F32)
        num_t = (w_inter * state_q[:dv, :]
                 + lax.dot_general(vh, s_t.astype(_BF16), _TN_DIMS, preferred_element_type=_F32))
        den = w_inter * state_q[dv:dv + 1, :] + jnp.sum(s_t, axis=0, keepdims=True)
        h_t = num_t * (1.0 / jnp.maximum(jnp.abs(den), jnp.exp(-m_j)))

        dec = b_last + src_col
        m_new = jnp.maximum(b_last + m_prev, jnp.max(dec, axis=0, keepdims=True))
        kw = kh.astype(_F32) * jnp.exp(dec - m_new)
        carry_scale = jnp.exp(b_last + m_prev - m_new)
        c_ref[h, :dv, :] = carry_scale * c_aug[:dv, :] + lax.dot_general(
            vh, kw.astype(_BF16), _TN_DIMS, preferred_element_type=_F32)
        c_ref[h, dv:dv + 1, :] = carry_scale * c_aug[dv:dv + 1, :] + jnp.sum(kw, axis=0, keepdims=True)
        m_ref[h] = m_new

        y_t = h_t * lax.rsqrt(jnp.mean(h_t * h_t, axis=0, keepdims=True) + _EPS)
        og = og_ref[rows, h * dv:(h + 1) * dv].astype(_F32)
        out_ref[rows, h * dv:(h + 1) * dv] = (y_t.T * gh_ref[:, h * dv:(h + 1) * dv]
                                              * jax.nn.sigmoid(og)).astype(out_ref.dtype)


def _mlstm(proj, gates, b_gate, g_hnorm, batch, seq, heads, dk, dv):
    t = proj.shape[0]
    chunk = _MLSTM_CHUNK * _MLSTM_CHUNKS_PER_STEP
    n_chunks = seq // chunk
    qk_w, v_w = heads * dk, heads * dv
    assert v_w == 2 * qk_w

    def rows(b, c):
        return b * n_chunks + c

    return pl.pallas_call(
        functools.partial(_mlstm_kernel, heads=heads, dk=dk, dv=dv),
        grid=(batch, n_chunks),
        in_specs=[pl.BlockSpec((chunk, qk_w), lambda b, c: (rows(b, c), 0)),
                  pl.BlockSpec((chunk, qk_w), lambda b, c: (rows(b, c), 1)),
                  pl.BlockSpec((chunk, v_w), lambda b, c: (rows(b, c), 1)),
                  pl.BlockSpec((chunk, v_w), lambda b, c: (rows(b, c), 2)),
                  pl.BlockSpec((chunk, _LANES), lambda b, c: (rows(b, c), 0)),
                  pl.BlockSpec((1, _LANES), lambda b, c: (0, 0)),
                  pl.BlockSpec((1, v_w), lambda b, c: (0, 0))],
        out_specs=pl.BlockSpec((chunk, v_w), lambda b, c: (rows(b, c), 0)),
        out_shape=jax.ShapeDtypeStruct((t, v_w), _BF16),
        scratch_shapes=[pltpu.VMEM((heads, dv + _BF16_ROWS, dk), _F32),
                        pltpu.VMEM((heads, 1, 1), _F32)],
        compiler_params=_params("parallel", "arbitrary"),
        name="mlstm",
    )(proj, proj, proj, proj, gates, b_gate, g_hnorm)


def _qkv_proj_kernel(x_ref, g1_ref, sc1_ref, sh1_ref, gkv_ref, sckv_ref, shkv_ref, wqk_ref, wvt_ref,
                     qk_ref, vt_ref, h1_ref, hk_ref, *, n_q, n_k, q_scale):
    j = pl.program_id(1)

    @pl.when(j == 0)
    def _():
        gain1 = g1_ref[...] * (1.0 + sc1_ref[...])
        shift1 = sh1_ref[...]
        gain_kv = gkv_ref[...] * (1.0 + sckv_ref[...])
        shift_kv = shkv_ref[...]
        n_chunks = x_ref.shape[0] // _NORM_ROWS

        def body(r, carry):
            rows = pl.ds(pl.multiple_of(r * _NORM_ROWS, _NORM_ROWS), _NORM_ROWS)
            x = x_ref[rows, :]
            xn = x * lax.rsqrt(jnp.mean(x * x, axis=-1, keepdims=True) + _EPS)
            h1_ref[rows, :] = (xn * gain1 + shift1).astype(h1_ref.dtype)
            hk_ref[rows, :] = (xn * gain_kv + shift_kv).astype(hk_ref.dtype)
            return carry

        lax.fori_loop(0, n_chunks, body, 0, unroll=_NORM_UNROLL)

    def store_heads(res):
        width = qk_ref.shape[2]
        for hh in range(qk_ref.shape[0]):
            qk_ref[hh] = res[:, hh * width:(hh + 1) * width].astype(qk_ref.dtype)

    @pl.when(j < n_q)
    def _():
        store_heads(jnp.dot(h1_ref[...], wqk_ref[...], preferred_element_type=_F32) * q_scale)

    @pl.when(jnp.logical_and(j >= n_q, j < n_q + n_k))
    def _():
        store_heads(jnp.dot(hk_ref[...], wqk_ref[...], preferred_element_type=_F32))

    @pl.when(j >= n_q + n_k)
    def _():
        vt_ref[...] = lax.dot_general(wvt_ref[...], hk_ref[...], _NT_DIMS,
                                      preferred_element_type=_F32).astype(vt_ref.dtype)


def _qkv_proj(x, g1, sc1, sh1, gkv, sckv, shkv, wqk, n_qc, wvt, q_scale, batch, seq, head_w):
    t, d = x.shape
    n_qkc, n_vc = wqk.shape[1], wvt.shape[0]
    tm = min(_PROJ_TM, seq)
    tn = _QKV_TN
    tiles_per_seq = seq // tm
    n_q, n_qk, n_v = n_qc // tn, n_qkc // tn, n_vc // tn

    def qk_col(j):
        return jnp.minimum(j, n_qk - 1)

    def vt_row(j):
        return jnp.maximum(j - n_qk, 0)

    batch_vec = pl.BlockSpec((None, 1, d), lambda i, j: (i // tiles_per_seq, 0, 0))
    row_vec = pl.BlockSpec((1, d), lambda i, j: (0, 0))
    return pl.pallas_call(
        functools.partial(_qkv_proj_kernel, n_q=n_q, n_k=n_qk - n_q, q_scale=q_scale),
        grid=(t // tm, n_qk + n_v),
        in_specs=[pl.BlockSpec((tm, d), lambda i, j: (i, 0)),
                  row_vec, batch_vec, batch_vec, row_vec, batch_vec, batch_vec,
                  pl.BlockSpec((d, tn), lambda i, j: (0, qk_col(j))),
                  pl.BlockSpec((tn, d), lambda i, j: (vt_row(j), 0))],
        out_specs=[pl.BlockSpec((None, tn // head_w, tm, head_w),
                                lambda i, j: (i // tiles_per_seq, qk_col(j), i % tiles_per_seq, 0)),
                   pl.BlockSpec((None, tn, tm), lambda i, j: (i // tiles_per_seq, vt_row(j), i % tiles_per_seq))],
        out_shape=[jax.ShapeDtypeStruct((batch, n_qkc // head_w, seq, head_w), _BF16),
                   jax.ShapeDtypeStruct((batch, n_vc, seq), _BF16)],
        scratch_shapes=[pltpu.VMEM((tm, d), _BF16), pltpu.VMEM((tm, d), _BF16)],
        compiler_params=_params("parallel", "arbitrary"),
        name="qkv_proj",
    )(x, g1, sc1, sh1, gkv, sckv, shkv, wqk, wvt)


def _attn_kernel(q_ref, k_ref, vt_ref, slope_ref, lam_ref, g_ref, o_ref, qa_ref, kpos_ref, s_ref, bmax_ref, m_ref,
                 acc_ref, *, dh, lam_init):
    qi = pl.program_id(2)
    blk = q_ref.shape[0]
    slope2 = slope_ref[:, 0:1] * _LOG2E

    dv = vt_ref.shape[0]

    @pl.when(qi == 0)
    def _():
        lane = lax.broadcasted_iota(jnp.int32, (blk, dh), 1)
        key_bias = lax.broadcasted_iota(jnp.int32, (blk, dh), 0).astype(_F32) * slope2
        part_hi = key_bias.astype(_BF16).astype(_F32)
        part_mid = (key_bias - part_hi).astype(_BF16).astype(_F32)
        part_lo = key_bias - part_hi - part_mid
        kpos_ref[...] = jnp.where(lane == 0, part_hi, jnp.where(lane == 1, part_mid,
                                                                jnp.where(lane == 2, part_lo, 0.0))).astype(_BF16)
        for s_idx in range(2):
            qa_ref[s_idx, :, dh:] = jnp.where(lane < 3, 1.0, 0.0).astype(_BF16)

    for s_idx in range(2):
        qa_ref[s_idx, :, :dh] = q_ref[:, s_idx * dh:(s_idx + 1) * dh]

    acc_ref[...] = jnp.zeros_like(acc_ref)
    m_ref[...] = jnp.full_like(m_ref, -jnp.inf)

    def scores(kj, masked, buf):
        keys = pl.ds(pl.multiple_of(kj * blk, blk), blk)
        for s_idx in range(2):
            k_cat = jnp.concatenate([k_ref[keys, s_idx * dh:(s_idx + 1) * dh], kpos_ref[...]], axis=1)
            st = lax.dot_general(k_cat, qa_ref[s_idx], _NT_DIMS, preferred_element_type=_F32)
            if masked:
                key = lax.broadcasted_iota(jnp.int32, (blk, blk), 0)
                qry = lax.broadcasted_iota(jnp.int32, (blk, blk), 1)
                st = jnp.where(key <= qry, st, -jnp.inf)
            s_ref[buf, s_idx] = st
            bmax_ref[buf, s_idx] = jnp.max(st, axis=0, keepdims=True)

    def accumulate(kj, buf):
        keys = pl.ds(pl.multiple_of(kj * blk, blk), blk)
        vt_aug = jnp.concatenate([vt_ref[:, keys], jnp.ones((_BF16_ROWS, blk), _BF16)], axis=0)
        block_bias = slope2 * jnp.full((1, 1), (kj - qi) * blk, jnp.int32).astype(_F32)
        for s_idx in range(2):
            m_old = m_ref[s_idx]
            m_new = jnp.maximum(m_old, bmax_ref[buf, s_idx] + block_bias)
            pt = jnp.exp2(s_ref[buf, s_idx] - (m_new - block_bias))
            acc_ref[s_idx] = acc_ref[s_idx] * jnp.exp2(m_old - m_new) + jnp.dot(
                vt_aug, pt.astype(_BF16), preferred_element_type=_F32)
            m_ref[s_idx] = m_new

    def block_of(t):
        return jnp.where(t == 0, qi, t - 1)

    def pair(u):
        t = 2 * u
        scores(t, False, 1)
        accumulate(block_of(t), 0)
        scores(t + 1, False, 0)
        accumulate(t, 1)

    def two_pairs(u, carry):
        pair(2 * u)
        pair(2 * u + 1)
        return carry

    scores(qi, True, 0)
    n_pairs = lax.shift_right_logical(qi, 1)
    lax.fori_loop(0, lax.shift_right_logical(n_pairs, 1), two_pairs, 0)

    @pl.when((n_pairs & 1) == 1)
    def _():
        pair(n_pairs - 1)

    t_rest = 2 * n_pairs
    one_more = (qi & 1) == 1

    @pl.when(one_more)
    def _():
        scores(t_rest, False, 1)
        accumulate(block_of(t_rest), 0)
        accumulate(t_rest, 1)

    @pl.when(jnp.logical_not(one_more))
    def _():
        accumulate(block_of(t_rest), 0)

    lp = lam_ref[...]
    lam = (jnp.exp(jnp.sum(lp[0:1, :] * lp[1:2, :], axis=-1, keepdims=True))
           - jnp.exp(jnp.sum(lp[2:3, :] * lp[3:4, :], axis=-1, keepdims=True)) + lam_init)
    ot = (acc_ref[0, :dv, :] * (1.0 / acc_ref[0, dv:dv + 1, :])
          - lam * (acc_ref[1, :dv, :] * (1.0 / acc_ref[1, dv:dv + 1, :])))
    ot = ot * lax.rsqrt(jnp.mean(ot * ot, axis=0, keepdims=True) + _EPS)
    o_ref[...] = (ot.T * g_ref[...] * (1.0 - lam_init)).astype(o_ref.dtype)


def _diff_attention(q, k, k_head0, vt, lam_p, g_subln, batch, seq, heads, dh, lam_init):
    t = batch * seq
    dv = 2 * dh
    blk = min(_ATTN_BLOCK, seq)
    n_blk = seq // blk
    slopes = jnp.asarray([2.0 ** (-8.0 * (h + 1) / heads) for h in range(heads)], _F32)
    slopes = jnp.broadcast_to(slopes[:, None, None], (heads, 1, _LANES))
    return pl.pallas_call(
        functools.partial(_attn_kernel, dh=dh, lam_init=lam_init),
        grid=(batch, heads, n_blk),
        in_specs=[pl.BlockSpec((None, None, blk, dv), lambda b, h, i: (b, h, i, 0)),
                  pl.BlockSpec((None, None, seq, dv), lambda b, h, i: (b, k_head0 + h, 0, 0)),
                  pl.BlockSpec((None, dv, seq), lambda b, h, i: (b, h, 0)),
                  pl.BlockSpec((None, 1, _LANES), lambda b, h, i: (h, 0, 0)),
                  pl.BlockSpec((4, dh), lambda b, h, i: (0, 0)),
                  pl.BlockSpec((1, dv), lambda b, h, i: (0, h))],
        out_specs=pl.BlockSpec((blk, dv), lambda b, h, i: (b * n_blk + i, h)),
        out_shape=jax.ShapeDtypeStruct((t, heads * dv), _BF16),
        scratch_shapes=[pltpu.VMEM((2, blk, 2 * dh), _BF16),
                        pltpu.VMEM((blk, dh), _BF16),
                        pltpu.VMEM((2, 2, blk, blk), _F32),
                        pltpu.VMEM((2, 2, 1, blk), _F32),
                        pltpu.VMEM((2, 1, blk), _F32),
                        pltpu.VMEM((2, dv + _BF16_ROWS, blk), _F32)],
        compiler_params=_params("parallel", "parallel", "arbitrary"),
        name="diff_attention",
    )(q, k, vt, slopes, lam_p, g_subln)


def kernel(x, c, w_mod, b_mod, g_norm, w_up, w_down, w_a_in, b_a_gate, g_a_hnorm, w_a_out, g_kv, w_kv_mod,
           b_kv_mod, w_kv, w_b_q, b_lambda, g_b_subln, w_b_out, g_final):
    batch, seq, d = x.shape
    depth = w_mod.shape[0]
    n_a = w_a_in.shape[0]
    m_heads = b_a_gate.shape[1] // 2
    m_dv = g_a_hnorm.shape[1] // m_heads
    m_dk = (w_a_in.shape[2] - 2 * m_heads * m_dv - 2 * m_heads) // (2 * m_heads)
    d_dh = b_lambda.shape[2]
    d_heads = w_b_q.shape[2] // (2 * d_dh)
    t = batch * seq

    xt = x.reshape(t, d)
    c_rows = jnp.zeros((_SUBLANES, d), _F32).at[:batch].set(c)

    mod = _modvec(c_rows, w_mod, b_mod.reshape(depth, 1, 6 * d))[:, :batch]
    kv_mod = _modvec(c_rows, w_kv_mod[None], b_kv_mod.reshape(1, 1, 2 * d))[0, :batch]

    def vec(v):
        return v.reshape(batch, 1, d)

    w_up_b, w_down_b = w_up.astype(_BF16), w_down.astype(_BF16)
    w_a_in_b, w_a_out_b = w_a_in.astype(_BF16), w_a_out.astype(_BF16)
    w_b_q_b, w_b_out_b, w_kv_b = w_b_q.astype(_BF16), w_b_out.astype(_BF16), w_kv.astype(_BF16)

    ones_d = jnp.ones((1, d), _F32)
    for l in range(depth):
        sh1, sc1, gt1, sh2, sc2, gt2 = (vec(m) for m in jnp.split(mod[l], 6, axis=-1))
        if l < n_a:
            qk_w, v_w = m_heads * m_dk, m_heads * m_dv
            main_w = 2 * qk_w + 2 * v_w
            col_scale = jnp.concatenate([jnp.ones((qk_w,), _F32), jnp.full((qk_w,), m_dk ** -0.5, _F32),
                                         jnp.ones((2 * v_w,), _F32)]).reshape(1, main_w)
            w_gate = jnp.zeros((d, _LANES), _BF16).at[:, :2 * m_heads].set(w_a_in_b[l, :, main_w:])
            proj, gates = _norm_mod_matmul(xt, g_norm[l, 0][None], sc1, sh1, w_a_in_b, l, col_scale, seq,
                                           w_narrow=w_gate)
            b_gate = jnp.zeros((1, _LANES), _F32).at[0, :2 * m_heads].set(b_a_gate[l])
            mix_in = _mlstm(proj, gates, b_gate, g_a_hnorm[l][None], batch, seq, m_heads, m_dk, m_dv)
            xt = _matmul_residual(mix_in, w_a_out_b, l, xt, gt1, seq)
        else:
            j = l - n_a
            lam_init = 0.8 - 0.6 * math.exp(-0.3 * l)
            q_scale = d_dh ** -0.5 * _LOG2E
            k_w = w_b_q.shape[2]
            if l == n_a:
                kv_shift, kv_scale = (vec(m) for m in jnp.split(kv_mod, 2, axis=-1))
                w_qk = jnp.concatenate([w_b_q_b[j], w_kv_b[:, :k_w]], axis=1)
                q, vt_sh = _qkv_proj(xt, g_norm[l, 0][None], sc1, sh1, g_kv[None], kv_scale, kv_shift,
                                     w_qk, k_w, w_kv_b[:, k_w:].T, q_scale, batch, seq, 2 * d_dh)
                k_sh = q
            else:
                q = _norm_mod_matmul(xt, g_norm[l, 0][None], sc1, sh1, w_b_q_b, j,
                                     jnp.full((1, k_w), q_scale, _F32), seq)
                q = q.reshape(batch, seq, d_heads, 2 * d_dh).transpose(0, 2, 1, 3)
            mix_in = _diff_attention(q, k_sh, d_heads, vt_sh, b_lambda[j], g_b_subln[j][None], batch, seq,
                                     d_heads, d_dh, lam_init)
            xt = _matmul_residual(mix_in, w_b_out_b, j, xt, gt1, seq)
        last = l == depth - 1
        xt = _mlp(xt, g_norm[l, 1][None], sc2, sh2, gt2, w_up_b, w_down_b, l,
                  g_final[None] if last else ones_d, seq, last)
    return xt.reshape(batch, seq, d)
```

```python
import functools
import math

import jax
import jax.numpy as jnp
from jax import lax
from jax.experimental import pallas as pl
from jax.experimental.pallas import tpu as pltpu

_EPS = 1e-6
_F32 = jnp.float32
_BF16 = jnp.bfloat16

_LANES = 128
_SUBLANES = 8
_BF16_ROWS = 16
_VMEM_LIMIT_BYTES = 59 * 1024 * 1024

_MODVEC_TN = 1024
_PROJ_TM = 1024
_IN_PROJ_TN = 2048
_OUT_PROJ_TM = 512
_OUT_PROJ_TN = 2048
_MLP_TM = 1024
_MLP_TF = 1024
_NORM_ROWS = 16
_NORM_UNROLL = 8
_MLSTM_CHUNK = 128
_MLSTM_CHUNKS_PER_STEP = 2
_ATTN_BLOCK = 512
_QKV_TN = 1024

_LOG2E = math.log2(math.e)

_NT_DIMS = (((1,), (1,)), ((), ()))
_TN_DIMS = (((0,), (0,)), ((), ()))


def _params(*semantics):
    return pltpu.CompilerParams(dimension_semantics=semantics, vmem_limit_bytes=_VMEM_LIMIT_BYTES)


def _modvec_kernel(c_ref, w_ref, b_ref, o_ref):
    o_ref[...] = jnp.dot(c_ref[...].astype(_BF16), w_ref[...].astype(_BF16),
                         preferred_element_type=_F32) + b_ref[...]


def _modvec(c_rows, w, b):
    n_l, d, n = w.shape
    tn = _MODVEC_TN
    return pl.pallas_call(
        _modvec_kernel,
        grid=(n_l, n // tn),
        in_specs=[pl.BlockSpec((_SUBLANES, d), lambda l, j: (0, 0)),
                  pl.BlockSpec((None, d, tn), lambda l, j: (l, 0, j)),
                  pl.BlockSpec((None, 1, tn), lambda l, j: (l, 0, j))],
        out_specs=pl.BlockSpec((None, _SUBLANES, tn), lambda l, j: (l, 0, j)),
        out_shape=jax.ShapeDtypeStruct((n_l, _SUBLANES, n), _F32),
        compiler_params=_params("parallel", "parallel"),
        name="modvec",
    )(c_rows, w, b)


def _norm_mod_tile(x_ref, gain, shift, h_ref):
    n_chunks = x_ref.shape[0] // _NORM_ROWS

    def body(r, carry):
        rows = pl.ds(pl.multiple_of(r * _NORM_ROWS, _NORM_ROWS), _NORM_ROWS)
        x = x_ref[rows, :]
        inv = lax.rsqrt(jnp.mean(x * x, axis=-1, keepdims=True) + _EPS)
        h_ref[rows, :] = (x * inv * gain + shift).astype(h_ref.dtype)
        return carry

    lax.fori_loop(0, n_chunks, body, 0, unroll=_NORM_UNROLL)


def _norm_mod_matmul_kernel(x_ref, g_ref, sc_ref, sh_ref, w_ref, cs_ref, *rest, narrow):
    if narrow:
        wn_ref, o_ref, on_ref, h_ref = rest
    else:
        o_ref, h_ref = rest

    @pl.when(pl.program_id(1) == 0)
    def _():
        _norm_mod_tile(x_ref, g_ref[...] * (1.0 + sc_ref[...]), sh_ref[...], h_ref)
        if narrow:
            on_ref[...] = jnp.dot(h_ref[...], wn_ref[...], preferred_element_type=_F32)

    acc = jnp.dot(h_ref[...], w_ref[...], preferred_element_type=_F32)
    o_ref[...] = (acc * cs_ref[...]).astype(o_ref.dtype)


def _norm_mod_matmul(x, g, scale, shift, w, layer, col_scale, seq, w_narrow=None):
    t, d = x.shape
    n = col_scale.shape[1]
    tm = min(_PROJ_TM, seq)
    tn = min(_IN_PROJ_TN, n)
    tiles_per_seq = seq // tm
    narrow = w_narrow is not None
    batch_vec = pl.BlockSpec((None, 1, d), lambda i, j: (i // tiles_per_seq, 0, 0))
    in_specs = [pl.BlockSpec((tm, d), lambda i, j: (i, 0)),
                pl.BlockSpec((1, d), lambda i, j: (0, 0)),
                batch_vec, batch_vec,
                pl.BlockSpec((None, d, tn), lambda i, j: (layer, 0, j)),
                pl.BlockSpec((1, tn), lambda i, j: (0, j))]
    out_specs = [pl.BlockSpec((tm, tn), lambda i, j: (i, j))]
    out_shape = [jax.ShapeDtypeStruct((t, n), _BF16)]
    operands = [x, g, scale, shift, w, col_scale]
    if narrow:
        in_specs.append(pl.BlockSpec((d, _LANES), lambda i, j: (0, 0)))
        out_specs.append(pl.BlockSpec((tm, _LANES), lambda i, j: (i, 0)))
        out_shape.append(jax.ShapeDtypeStruct((t, _LANES), _F32))
        operands.append(w_narrow)
    outs = pl.pallas_call(
        functools.partial(_norm_mod_matmul_kernel, narrow=narrow),
        grid=(t // tm, n // tn),
        in_specs=in_specs,
        out_specs=out_specs,
        out_shape=out_shape,
        scratch_shapes=[pltpu.VMEM((tm, d), _BF16)],
        compiler_params=_params("parallel", "arbitrary"),
        name="norm_mod_matmul",
    )(*operands)
    return outs if narrow else outs[0]


def _matmul_residual_kernel(a_ref, w_ref, x_ref, gate_ref, o_ref):
    acc = jnp.dot(a_ref[...], w_ref[...], preferred_element_type=_F32)
    o_ref[...] = x_ref[...] + gate_ref[...] * acc


def _matmul_residual(a, w, layer, x, gate, seq):
    t, k = a.shape
    d = w.shape[2]
    tm = min(_OUT_PROJ_TM, seq)
    tn = min(_OUT_PROJ_TN, d)
    tiles_per_seq = seq // tm
    return pl.pallas_call(
        _matmul_residual_kernel,
        grid=(t // tm, d // tn),
        in_specs=[pl.BlockSpec((tm, k), lambda i, j: (i, 0)),
                  pl.BlockSpec((None, k, tn), lambda i, j: (layer, 0, j)),
                  pl.BlockSpec((tm, tn), lambda i, j: (i, j)),
                  pl.BlockSpec((None, 1, tn), lambda i, j: (i // tiles_per_seq, 0, j))],
        out_specs=pl.BlockSpec((tm, tn), lambda i, j: (i, j)),
        out_shape=jax.ShapeDtypeStruct((t, d), _F32),
        compiler_params=_params("parallel", "parallel"),
        name="matmul_residual",
    )(a, w, x, gate)


def _mlp_kernel(x_ref, g_ref, sc_ref, sh_ref, gate_ref, wu_ref, wd_ref, gf_ref, o_ref, h_ref, *, final_norm):
    f = pl.program_id(1)

    @pl.when(f == 0)
    def _():
        _norm_mod_tile(x_ref, g_ref[...] * (1.0 + sc_ref[...]), sh_ref[...], h_ref)
        o_ref[...] = jnp.zeros_like(o_ref)

    u = jnp.dot(h_ref[...], wu_ref[...], preferred_element_type=_F32)
    u = jnp.square(jnp.maximum(u, 0.0)).astype(_BF16)
    o_ref[...] += jnp.dot(u, wd_ref[...], preferred_element_type=_F32)

    @pl.when(f == pl.num_programs(1) - 1)
    def _():
        gate = gate_ref[...]
        gain = gf_ref[...]
        slab = _NORM_ROWS * _NORM_UNROLL

        def residual(rows):
            return x_ref[rows, :] + gate * o_ref[rows, :]

        def body(r, carry):
            base = pl.multiple_of(r * slab, slab)
            chunks = [pl.ds(base + c * _NORM_ROWS, _NORM_ROWS) for c in range(_NORM_UNROLL)]
            if final_norm:
                inv = [lax.rsqrt(jnp.mean(jnp.square(residual(rows)), axis=-1, keepdims=True) + _EPS)
                       for rows in chunks]
                for rows, scale in zip(chunks, inv):
                    o_ref[rows, :] = residual(rows) * scale * gain
            else:
                for rows in chunks:
                    o_ref[rows, :] = residual(rows)
            return carry

        lax.fori_loop(0, x_ref.shape[0] // slab, body, 0)


def _mlp(x, g, scale, shift, gate, w_up, w_down, layer, g_final, seq, final_norm):
    t, d = x.shape
    d_ff = w_up.shape[2]
    tm = min(_MLP_TM, seq)
    tf = min(_MLP_TF, d_ff)
    tiles_per_seq = seq // tm
    batch_vec = pl.BlockSpec((None, 1, d), lambda i, f: (i // tiles_per_seq, 0, 0))
    return pl.pallas_call(
        functools.partial(_mlp_kernel, final_norm=final_norm),
        grid=(t // tm, d_ff // tf),
        in_specs=[pl.BlockSpec((tm, d), lambda i, f: (i, 0)),
                  pl.BlockSpec((1, d), lambda i, f: (0, 0)),
                  batch_vec, batch_vec, batch_vec,
                  pl.BlockSpec((None, d, tf), lambda i, f: (layer, 0, f)),
                  pl.BlockSpec((None, tf, d), lambda i, f: (layer, f, 0)),
                  pl.BlockSpec((1, d), lambda i, f: (0, 0))],
        out_specs=pl.BlockSpec((tm, d), lambda i, f: (i, 0)),
        out_shape=jax.ShapeDtypeStruct((t, d), _F32),
        scratch_shapes=[pltpu.VMEM((tm, d), _BF16)],
        compiler_params=_params("parallel", "arbitrary"),
        name="mlp",
    )(x, g, scale, shift, gate, w_up, w_down, g_final)


def _log_sigmoid(x):
    return jnp.minimum(x, 0.0) - jnp.log1p(jnp.exp(-jnp.abs(x)))


def _mlstm_kernel(q_ref, k_ref, v_ref, og_ref, gt_ref, bg_ref, gh_ref, out_ref, c_ref, m_ref,
                  *, heads, dk, dv):
    @pl.when(pl.program_id(1) == 0)
    def _():
        c_ref[...] = jnp.zeros_like(c_ref)
        m_ref[...] = jnp.zeros_like(m_ref)

    chunk = _MLSTM_CHUNK
    src = lax.broadcasted_iota(jnp.int32, (chunk, chunk), 0)
    dst = lax.broadcasted_iota(jnp.int32, (chunk, chunk), 1)
    visible = src <= dst
    for sub in range(q_ref.shape[0] // chunk):
        rows = slice(sub * chunk, (sub + 1) * chunk)
        gates = gt_ref[rows, :] + bg_ref[...]
        b_all = jnp.dot((dst <= src).astype(_F32), _log_sigmoid(gates), preferred_element_type=_F32,
                        precision=lax.Precision.HIGHEST)
        _mlstm_chunk(q_ref, k_ref, v_ref, og_ref, gh_ref, out_ref, c_ref, m_ref, rows, gates, b_all, visible,
                     heads, dk, dv)


def _mlstm_chunk(q_ref, k_ref, v_ref, og_ref, gh_ref, out_ref, c_ref, m_ref, rows, gates, b_all, visible,
                 heads, dk, dv):
    chunk = _MLSTM_CHUNK
    gates_t = gates.T
    b_all_t = b_all.T
    for h in range(heads):
        b_row = b_all_t[heads + h:heads + h + 1, :]
        src_col = gates[:, h:h + 1] - b_all[:, heads + h:heads + h + 1]
        b_last = b_row[:, chunk - 1:chunk]
        m_prev = m_ref[h]
        c_aug = c_ref[h]
        qh = q_ref[rows, h * dk:(h + 1) * dk]
        kh = k_ref[rows, h * dk:(h + 1) * dk]
        vh = v_ref[rows, h * dv:(h + 1) * dv]

        dmat_t = jnp.where(visible, b_row + src_col, -jnp.inf)
        inter = b_row + m_prev
        m_j = jnp.maximum(inter, jnp.max(dmat_t, axis=0, keepdims=True))
        w_inter = jnp.exp(inter - m_j)
        s_t = lax.dot_general(kh, qh, _NT_DIMS, preferred_element_type=_F32) * jnp.exp(dmat_t - m_j)
        state_q = lax.dot_general(c_aug.astype(_BF16), qh, _NT_DIMS, preferred_element_type=_F32)
        num_t = (w_inter * state_q[:dv, :]
                 + lax.dot_general(vh, s_t.astype(_BF16), _TN_DIMS, preferred_element_type=_F32))
        den = w_inter * state_q[dv:dv + 1, :] + jnp.sum(s_t, axis=0, keepdims=True)
        h_t = num_t * (1.0 / jnp.maximum(jnp.abs(den), jnp.exp(-m_j)))

        dec = b_last + src_col
        m_new = jnp.maximum(b_last + m_prev, jnp.max(dec, axis=0, keepdims=True))
        kw = kh.astype(_F32) * jnp.exp(dec - m_new)
        carry_scale = jnp.exp(b_last + m_prev - m_new)
        c_ref[h, :dv, :] = carry_scale * c_aug[:dv, :] + lax.dot_general(
            vh, kw.astype(_BF16), _TN_DIMS, preferred_element_type=_F32)
        c_ref[h, dv:dv + 1, :] = carry_scale * c_aug[dv:dv + 1, :] + jnp.sum(kw, axis=0, keepdims=True)
        m_ref[h] = m_new

        y_t = h_t * lax.rsqrt(jnp.mean(h_t * h_t, axis=0, keepdims=True) + _EPS)
        og = og_ref[rows, h * dv:(h + 1) * dv].astype(_F32)
        out_ref[rows, h * dv:(h + 1) * dv] = (y_t.T * gh_ref[:, h * dv:(h + 1) * dv]
                                              * jax.nn.sigmoid(og)).astype(out_ref.dtype)


def _mlstm(proj, gates, b_gate, g_hnorm, batch, seq, heads, dk, dv):
    t = proj.shape[0]
    chunk = _MLSTM_CHUNK * _MLSTM_CHUNKS_PER_STEP
    n_chunks = seq // chunk
    qk_w, v_w = heads * dk, heads * dv
    assert v_w == 2 * qk_w

    def rows(b, c):
        return b * n_chunks + c

    return pl.pallas_call(
        functools.partial(_mlstm_kernel, heads=heads, dk=dk, dv=dv),
        grid=(batch, n_chunks),
        in_specs=[pl.BlockSpec((chunk, qk_w), lambda b, c: (rows(b, c), 0)),
                  pl.BlockSpec((chunk, qk_w), lambda b, c: (rows(b, c), 1)),
                  pl.BlockSpec((chunk, v_w), lambda b, c: (rows(b, c), 1)),
                  pl.BlockSpec((chunk, v_w), lambda b, c: (rows(b, c), 2)),
                  pl.BlockSpec((chunk, _LANES), lambda b, c: (rows(b, c), 0)),
                  pl.BlockSpec((1, _LANES), lambda b, c: (0, 0)),
                  pl.BlockSpec((1, v_w), lambda b, c: (0, 0))],
        out_specs=pl.BlockSpec((chunk, v_w), lambda b, c: (rows(b, c), 0)),
        out_shape=jax.ShapeDtypeStruct((t, v_w), _BF16),
        scratch_shapes=[pltpu.VMEM((heads, dv + _BF16_ROWS, dk), _F32),
                        pltpu.VMEM((heads, 1, 1), _F32)],
        compiler_params=_params("parallel", "arbitrary"),
        name="mlstm",
    )(proj, proj, proj, proj, gates, b_gate, g_hnorm)


def _qkv_proj_kernel(x_ref, g1_ref, sc1_ref, sh1_ref, gkv_ref, sckv_ref, shkv_ref, wqk_ref, wvt_ref,
                     qk_ref, vt_ref, h1_ref, hk_ref, *, n_q, n_k, q_scale):
    j = pl.program_id(1)

    @pl.when(j == 0)
    def _():
        gain1 = g1_ref[...] * (1.0 + sc1_ref[...])
        shift1 = sh1_ref[...]
        gain_kv = gkv_ref[...] * (1.0 + sckv_ref[...])
        shift_kv = shkv_ref[...]
        n_chunks = x_ref.shape[0] // _NORM_ROWS

        def body(r, carry):
            rows = pl.ds(pl.multiple_of(r * _NORM_ROWS, _NORM_ROWS), _NORM_ROWS)
            x = x_ref[rows, :]
            xn = x * lax.rsqrt(jnp.mean(x * x, axis=-1, keepdims=True) + _EPS)
            h1_ref[rows, :] = (xn * gain1 + shift1).astype(h1_ref.dtype)
            hk_ref[rows, :] = (xn * gain_kv + shift_kv).astype(hk_ref.dtype)
            return carry

        lax.fori_loop(0, n_chunks, body, 0, unroll=_NORM_UNROLL)

    @pl.when(j < n_q)
    def _():
        acc = jnp.dot(h1_ref[...], wqk_ref[...], preferred_element_type=_F32)
        qk_ref[...] = (acc * q_scale).astype(qk_ref.dtype)

    @pl.when(jnp.logical_and(j >= n_q, j < n_q + n_k))
    def _():
        qk_ref[...] = jnp.dot(hk_ref[...], wqk_ref[...], preferred_element_type=_F32).astype(qk_ref.dtype)

    @pl.when(j >= n_q + n_k)
    def _():
        vt_ref[...] = lax.dot_general(wvt_ref[...], hk_ref[...], _NT_DIMS,
                                      preferred_element_type=_F32).astype(vt_ref.dtype)


def _qkv_proj(x, g1, sc1, sh1, gkv, sckv, shkv, wqk, n_qc, wvt, q_scale, batch, seq):
    t, d = x.shape
    n_qkc, n_vc = wqk.shape[1], wvt.shape[0]
    tm = min(_PROJ_TM, seq)
    tn = _QKV_TN
    tiles_per_seq = seq // tm
    n_q, n_qk, n_v = n_qc // tn, n_qkc // tn, n_vc // tn

    def qk_col(j):
        return jnp.minimum(j, n_qk - 1)

    def vt_row(j):
        return jnp.maximum(j - n_qk, 0)

    batch_vec = pl.BlockSpec((None, 1, d), lambda i, j: (i // tiles_per_seq, 0, 0))
    row_vec = pl.BlockSpec((1, d), lambda i, j: (0, 0))
    return pl.pallas_call(
        functools.partial(_qkv_proj_kernel, n_q=n_q, n_k=n_qk - n_q, q_scale=q_scale),
        grid=(t // tm, n_qk + n_v),
        in_specs=[pl.BlockSpec((tm, d), lambda i, j: (i, 0)),
                  row_vec, batch_vec, batch_vec, row_vec, batch_vec, batch_vec,
                  pl.BlockSpec((d, tn), lambda i, j: (0, qk_col(j))),
                  pl.BlockSpec((tn, d), lambda i, j: (vt_row(j), 0))],
        out_specs=[pl.BlockSpec((tm, tn), lambda i, j: (i, qk_col(j))),
                   pl.BlockSpec((None, tn, tm), lambda i, j: (i // tiles_per_seq, vt_row(j), i % tiles_per_seq))],
        out_shape=[jax.ShapeDtypeStruct((t, n_qkc), _BF16),
                   jax.ShapeDtypeStruct((batch, n_vc, seq), _BF16)],
        scratch_shapes=[pltpu.VMEM((tm, d), _BF16), pltpu.VMEM((tm, d), _BF16)],
        compiler_params=_params("parallel", "arbitrary"),
        name="qkv_proj",
    )(x, g1, sc1, sh1, gkv, sckv, shkv, wqk, wvt)


def _attn_kernel(q_ref, k_ref, vt_ref, slope_ref, lam_ref, g_ref, o_ref, qa_ref, kpos_ref, s_ref, bmax_ref, m_ref,
                 acc_ref, *, dh, lam_init):
    qi = pl.program_id(2)
    blk = q_ref.shape[0]
    slope2 = slope_ref[:, 0:1] * _LOG2E

    dv = vt_ref.shape[0]

    @pl.when(qi == 0)
    def _():
        lane = lax.broadcasted_iota(jnp.int32, (blk, dh), 1)
        key_bias = lax.broadcasted_iota(jnp.int32, (blk, dh), 0).astype(_F32) * slope2
        part_hi = key_bias.astype(_BF16).astype(_F32)
        part_mid = (key_bias - part_hi).astype(_BF16).astype(_F32)
        part_lo = key_bias - part_hi - part_mid
        kpos_ref[...] = jnp.where(lane == 0, part_hi, jnp.where(lane == 1, part_mid,
                                                                jnp.where(lane == 2, part_lo, 0.0))).astype(_BF16)
        for s_idx in range(2):
            qa_ref[s_idx, :, dh:] = jnp.where(lane < 3, 1.0, 0.0).astype(_BF16)

    for s_idx in range(2):
        qa_ref[s_idx, :, :dh] = q_ref[:, s_idx * dh:(s_idx + 1) * dh]

    acc_ref[...] = jnp.zeros_like(acc_ref)
    m_ref[...] = jnp.full_like(m_ref, -jnp.inf)

    def scores(kj, masked, buf):
        keys = pl.ds(pl.multiple_of(kj * blk, blk), blk)
        for s_idx in range(2):
            k_cat = jnp.concatenate([k_ref[keys, s_idx * dh:(s_idx + 1) * dh], kpos_ref[...]], axis=1)
            st = lax.dot_general(k_cat, qa_ref[s_idx], _NT_DIMS, preferred_element_type=_F32)
            if masked:
                key = lax.broadcasted_iota(jnp.int32, (blk, blk), 0)
                qry = lax.broadcasted_iota(jnp.int32, (blk, blk), 1)
                st = jnp.where(key <= qry, st, -jnp.inf)
            s_ref[buf, s_idx] = st
            bmax_ref[buf, s_idx] = jnp.max(st, axis=0, keepdims=True)

    def accumulate(kj, buf):
        keys = pl.ds(pl.multiple_of(kj * blk, blk), blk)
        vt_aug = jnp.concatenate([vt_ref[:, keys], jnp.ones((_BF16_ROWS, blk), _BF16)], axis=0)
        block_bias = slope2 * jnp.full((1, 1), (kj - qi) * blk, jnp.int32).astype(_F32)
        for s_idx in range(2):
            m_old = m_ref[s_idx]
            m_new = jnp.maximum(m_old, bmax_ref[buf, s_idx] + block_bias)
            pt = jnp.exp2(s_ref[buf, s_idx] - (m_new - block_bias))
            acc_ref[s_idx] = acc_ref[s_idx] * jnp.exp2(m_old - m_new) + jnp.dot(
                vt_aug, pt.astype(_BF16), preferred_element_type=_F32)
            m_ref[s_idx] = m_new

    def block_of(t):
        return jnp.where(t == 0, qi, t - 1)

    def pair(u):
        t = 2 * u
        scores(t, False, 1)
        accumulate(block_of(t), 0)
        scores(t + 1, False, 0)
        accumulate(t, 1)

    def four_pairs(u, carry):
        for p in range(4):
            pair(4 * u + p)
        return carry

    scores(qi, True, 0)
    n_pairs = lax.shift_right_logical(qi, 1)
    n_fours = lax.shift_right_logical(n_pairs, 2)
    lax.fori_loop(0, n_fours, four_pairs, 0)

    @pl.when((n_pairs & 2) == 2)
    def _():
        pair(4 * n_fours)
        pair(4 * n_fours + 1)

    @pl.when((n_pairs & 1) == 1)
    def _():
        pair(n_pairs - 1)

    t_rest = 2 * n_pairs
    one_more = (qi & 1) == 1

    @pl.when(one_more)
    def _():
        scores(t_rest, False, 1)
        accumulate(block_of(t_rest), 0)
        accumulate(t_rest, 1)

    @pl.when(jnp.logical_not(one_more))
    def _():
        accumulate(block_of(t_rest), 0)

    lp = lam_ref[...]
    lam = (jnp.exp(jnp.sum(lp[0:1, :] * lp[1:2, :], axis=-1, keepdims=True))
           - jnp.exp(jnp.sum(lp[2:3, :] * lp[3:4, :], axis=-1, keepdims=True)) + lam_init)
    ot = (acc_ref[0, :dv, :] * (1.0 / acc_ref[0, dv:dv + 1, :])
          - lam * (acc_ref[1, :dv, :] * (1.0 / acc_ref[1, dv:dv + 1, :])))
    ot = ot * lax.rsqrt(jnp.mean(ot * ot, axis=0, keepdims=True) + _EPS)
    o_ref[...] = (ot.T * g_ref[...] * (1.0 - lam_init)).astype(o_ref.dtype)


def _diff_attention(q, k, k_head0, vt, lam_p, g_subln, batch, seq, heads, dh, lam_init):
    t = q.shape[0]
    dv = 2 * dh
    blk = min(_ATTN_BLOCK, seq)
    n_blk = seq // blk
    slopes = jnp.asarray([2.0 ** (-8.0 * (h + 1) / heads) for h in range(heads)], _F32)
    slopes = jnp.broadcast_to(slopes[:, None, None], (heads, 1, _LANES))
    return pl.pallas_call(
        functools.partial(_attn_kernel, dh=dh, lam_init=lam_init),
        grid=(batch, heads, n_blk),
        in_specs=[pl.BlockSpec((blk, dv), lambda b, h, i: (b * n_blk + i, h)),
                  pl.BlockSpec((seq, dv), lambda b, h, i: (b, k_head0 + h)),
                  pl.BlockSpec((None, dv, seq), lambda b, h, i: (b, h, 0)),
                  pl.BlockSpec((None, 1, _LANES), lambda b, h, i: (h, 0, 0)),
                  pl.BlockSpec((4, dh), lambda b, h, i: (0, 0)),
                  pl.BlockSpec((1, dv), lambda b, h, i: (0, h))],
        out_specs=pl.BlockSpec((blk, dv), lambda b, h, i: (b * n_blk + i, h)),
        out_shape=jax.ShapeDtypeStruct((t, heads * dv), _BF16),
        scratch_shapes=[pltpu.VMEM((2, blk, 2 * dh), _BF16),
                        pltpu.VMEM((blk, dh), _BF16),
                        pltpu.VMEM((2, 2, blk, blk), _F32),
                        pltpu.VMEM((2, 2, 1, blk), _F32),
                        pltpu.VMEM((2, 1, blk), _F32),
                        pltpu.VMEM((2, dv + _BF16_ROWS, blk), _F32)],
        compiler_params=_params("parallel", "parallel", "arbitrary"),
        name="diff_attention",
    )(q, k, vt, slopes, lam_p, g_subln)


def kernel(x, c, w_mod, b_mod, g_norm, w_up, w_down, w_a_in, b_a_gate, g_a_hnorm, w_a_out, g_kv, w_kv_mod,
           b_kv_mod, w_kv, w_b_q, b_lambda, g_b_subln, w_b_out, g_final):
    batch, seq, d = x.shape
    depth = w_mod.shape[0]
    n_a = w_a_in.shape[0]
    m_heads = b_a_gate.shape[1] // 2
    m_dv = g_a_hnorm.shape[1] // m_heads
    m_dk = (w_a_in.shape[2] - 2 * m_heads * m_dv - 2 * m_heads) // (2 * m_heads)
    d_dh = b_lambda.shape[2]
    d_heads = w_b_q.shape[2] // (2 * d_dh)
    t = batch * seq

    xt = x.reshape(t, d)
    c_rows = jnp.zeros((_SUBLANES, d), _F32).at[:batch].set(c)

    mod = _modvec(c_rows, w_mod, b_mod.reshape(depth, 1, 6 * d))[:, :batch]
    kv_mod = _modvec(c_rows, w_kv_mod[None], b_kv_mod.reshape(1, 1, 2 * d))[0, :batch]

    def vec(v):
        return v.reshape(batch, 1, d)

    w_up_b, w_down_b = w_up.astype(_BF16), w_down.astype(_BF16)
    w_a_in_b, w_a_out_b = w_a_in.astype(_BF16), w_a_out.astype(_BF16)
    w_b_q_b, w_b_out_b = w_b_q.astype(_BF16), w_b_out.astype(_BF16)

    ones_d = jnp.ones((1, d), _F32)
    for l in range(depth):
        sh1, sc1, gt1, sh2, sc2, gt2 = (vec(m) for m in jnp.split(mod[l], 6, axis=-1))
        if l < n_a:
            qk_w, v_w = m_heads * m_dk, m_heads * m_dv
            main_w = 2 * qk_w + 2 * v_w
            col_scale = jnp.concatenate([jnp.ones((qk_w,), _F32), jnp.full((qk_w,), m_dk ** -0.5, _F32),
                                         jnp.ones((2 * v_w,), _F32)]).reshape(1, main_w)
            w_gate = jnp.zeros((d, _LANES), _BF16).at[:, :2 * m_heads].set(w_a_in_b[l, :, main_w:])
            proj, gates = _norm_mod_matmul(xt, g_norm[l, 0][None], sc1, sh1, w_a_in_b, l, col_scale, seq,
                                           w_narrow=w_gate)
            b_gate = jnp.zeros((1, _LANES), _F32).at[0, :2 * m_heads].set(b_a_gate[l])
            mix_in = _mlstm(proj, gates, b_gate, g_a_hnorm[l][None], batch, seq, m_heads, m_dk, m_dv)
            xt = _matmul_residual(mix_in, w_a_out_b, l, xt, gt1, seq)
        else:
            j = l - n_a
            lam_init = 0.8 - 0.6 * math.exp(-0.3 * l)
            q_scale = d_dh ** -0.5 * _LOG2E
            k_w = w_b_q.shape[2]
            if l == n_a:
                kv_shift, kv_scale = (vec(m) for m in jnp.split(kv_mod, 2, axis=-1))
                w_qk = jnp.concatenate([w_b_q[j], w_kv[:, :k_w]], axis=1).astype(_BF16)
                q, vt_sh = _qkv_proj(xt, g_norm[l, 0][None], sc1, sh1, g_kv[None], kv_scale, kv_shift,
                                     w_qk, k_w, w_kv[:, k_w:].T.astype(_BF16), q_scale, batch, seq)
                k_sh = q
            else:
                q = _norm_mod_matmul(xt, g_norm[l, 0][None], sc1, sh1, w_b_q_b, j,
                                     jnp.full((1, k_w), q_scale, _F32), seq)
            mix_in = _diff_attention(q, k_sh, d_heads, vt_sh, b_lambda[j], g_b_subln[j][None], batch, seq,
                                     d_heads, d_dh, lam_init)
            xt = _matmul_residual(mix_in, w_b_out_b, j, xt, gt1, seq)
        last = l == depth - 1
        xt = _mlp(xt, g_norm[l, 1][None], sc2, sh2, gt2, w_up_b, w_down_b, l,
                  g_final[None] if last else ones_d, seq, last)
    return xt.reshape(batch, seq, d)
```

```python
import functools
import math

import jax
import jax.numpy as jnp
from jax import lax
from jax.experimental import pallas as pl
from jax.experimental.pallas import tpu as pltpu

_EPS = 1e-6
_F32 = jnp.float32
_BF16 = jnp.bfloat16

_LANES = 128
_SUBLANES = 8
_BF16_ROWS = 16
_VMEM_LIMIT_BYTES = 59 * 1024 * 1024

_MODVEC_TN = 1024
_PROJ_TM = 1024
_IN_PROJ_TN = 2048
_OUT_PROJ_TM = 512
_OUT_PROJ_TN = 2048
_MLP_TM = 1024
_MLP_TF = 1024
_NORM_ROWS = 16
_NORM_UNROLL = 8
_MLSTM_CHUNK = 128
_MLSTM_CHUNKS_PER_STEP = 2
_ATTN_BLOCK = 512
_QKV_TN = 1024

_LOG2E = math.log2(math.e)

_NT_DIMS = (((1,), (1,)), ((), ()))
_TN_DIMS = (((0,), (0,)), ((), ()))


def _params(*semantics):
    return pltpu.CompilerParams(dimension_semantics=semantics, vmem_limit_bytes=_VMEM_LIMIT_BYTES)


def _modvec_kernel(c_ref, w_ref, b_ref, o_ref):
    o_ref[...] = jnp.dot(c_ref[...].astype(_BF16), w_ref[...].astype(_BF16),
                         preferred_element_type=_F32) + b_ref[...]


def _modvec(c_rows, w, b):
    n_l, d, n = w.shape
    tn = _MODVEC_TN
    return pl.pallas_call(
        _modvec_kernel,
        grid=(n_l, n // tn),
        in_specs=[pl.BlockSpec((_SUBLANES, d), lambda l, j: (0, 0)),
                  pl.BlockSpec((None, d, tn), lambda l, j: (l, 0, j)),
                  pl.BlockSpec((None, 1, tn), lambda l, j: (l, 0, j))],
        out_specs=pl.BlockSpec((None, _SUBLANES, tn), lambda l, j: (l, 0, j)),
        out_shape=jax.ShapeDtypeStruct((n_l, _SUBLANES, n), _F32),
        compiler_params=_params("parallel", "parallel"),
        name="modvec",
    )(c_rows, w, b)


def _norm_mod_tile(x_ref, gain, shift, h_ref):
    n_chunks = x_ref.shape[0] // _NORM_ROWS

    def body(r, carry):
        rows = pl.ds(pl.multiple_of(r * _NORM_ROWS, _NORM_ROWS), _NORM_ROWS)
        x = x_ref[rows, :]
        inv = lax.rsqrt(jnp.mean(x * x, axis=-1, keepdims=True) + _EPS)
        h_ref[rows, :] = (x * inv * gain + shift).astype(h_ref.dtype)
        return carry

    lax.fori_loop(0, n_chunks, body, 0, unroll=True)


def _norm_mod_matmul_kernel(x_ref, g_ref, sc_ref, sh_ref, w_ref, cs_ref, *rest, narrow):
    if narrow:
        wn_ref, o_ref, on_ref, h_ref = rest
    else:
        o_ref, h_ref = rest

    @pl.when(pl.program_id(1) == 0)
    def _():
        _norm_mod_tile(x_ref, g_ref[...] * (1.0 + sc_ref[...]), sh_ref[...], h_ref)
        if narrow:
            on_ref[...] = jnp.dot(h_ref[...], wn_ref[...], preferred_element_type=_F32)

    acc = jnp.dot(h_ref[...], w_ref[...], preferred_element_type=_F32)
    o_ref[...] = (acc * cs_ref[...]).astype(o_ref.dtype)


def _norm_mod_matmul(x, g, scale, shift, w, layer, col_scale, seq, w_narrow=None):
    t, d = x.shape
    n = col_scale.shape[1]
    tm = min(_PROJ_TM, seq)
    tn = min(_IN_PROJ_TN, n)
    tiles_per_seq = seq // tm
    narrow = w_narrow is not None
    batch_vec = pl.BlockSpec((None, 1, d), lambda i, j: (i // tiles_per_seq, 0, 0))
    in_specs = [pl.BlockSpec((tm, d), lambda i, j: (i, 0)),
                pl.BlockSpec((1, d), lambda i, j: (0, 0)),
                batch_vec, batch_vec,
                pl.BlockSpec((None, d, tn), lambda i, j: (layer, 0, j)),
                pl.BlockSpec((1, tn), lambda i, j: (0, j))]
    out_specs = [pl.BlockSpec((tm, tn), lambda i, j: (i, j))]
    out_shape = [jax.ShapeDtypeStruct((t, n), _BF16)]
    operands = [x, g, scale, shift, w, col_scale]
    if narrow:
        in_specs.append(pl.BlockSpec((d, _LANES), lambda i, j: (0, 0)))
        out_specs.append(pl.BlockSpec((tm, _LANES), lambda i, j: (i, 0)))
        out_shape.append(jax.ShapeDtypeStruct((t, _LANES), _F32))
        operands.append(w_narrow)
    outs = pl.pallas_call(
        functools.partial(_norm_mod_matmul_kernel, narrow=narrow),
        grid=(t // tm, n // tn),
        in_specs=in_specs,
        out_specs=out_specs,
        out_shape=out_shape,
        scratch_shapes=[pltpu.VMEM((tm, d), _BF16)],
        compiler_params=_params("parallel", "arbitrary"),
        name="norm_mod_matmul",
    )(*operands)
    return outs if narrow else outs[0]


def _matmul_residual_kernel(a_ref, w_ref, x_ref, gate_ref, o_ref):
    acc = jnp.dot(a_ref[...], w_ref[...], preferred_element_type=_F32)
    o_ref[...] = x_ref[...] + gate_ref[...] * acc


def _matmul_residual(a, w, layer, x, gate, seq):
    t, k = a.shape
    d = w.shape[2]
    tm = min(_OUT_PROJ_TM, seq)
    tn = min(_OUT_PROJ_TN, d)
    tiles_per_seq = seq // tm
    return pl.pallas_call(
        _matmul_residual_kernel,
        grid=(t // tm, d // tn),
        in_specs=[pl.BlockSpec((tm, k), lambda i, j: (i, 0)),
                  pl.BlockSpec((None, k, tn), lambda i, j: (layer, 0, j)),
                  pl.BlockSpec((tm, tn), lambda i, j: (i, j)),
                  pl.BlockSpec((None, 1, tn), lambda i, j: (i // tiles_per_seq, 0, j))],
        out_specs=pl.BlockSpec((tm, tn), lambda i, j: (i, j)),
        out_shape=jax.ShapeDtypeStruct((t, d), _F32),
        compiler_params=_params("parallel", "parallel"),
        name="matmul_residual",
    )(a, w, x, gate)


def _mlp_kernel(x_ref, g_ref, sc_ref, sh_ref, gate_ref, wu_ref, wd_ref, gf_ref, o_ref, h_ref, *, final_norm):
    f = pl.program_id(1)

    @pl.when(f == 0)
    def _():
        _norm_mod_tile(x_ref, g_ref[...] * (1.0 + sc_ref[...]), sh_ref[...], h_ref)
        o_ref[...] = jnp.zeros_like(o_ref)

    u = jnp.dot(h_ref[...], wu_ref[...], preferred_element_type=_F32)
    u = jnp.square(jnp.maximum(u, 0.0)).astype(_BF16)
    o_ref[...] += jnp.dot(u, wd_ref[...], preferred_element_type=_F32)

    @pl.when(f == pl.num_programs(1) - 1)
    def _():
        gate = gate_ref[...]
        gain = gf_ref[...]
        slab = _NORM_ROWS * _NORM_UNROLL

        def residual(rows):
            return x_ref[rows, :] + gate * o_ref[rows, :]

        def body(r, carry):
            base = pl.multiple_of(r * slab, slab)
            chunks = [pl.ds(base + c * _NORM_ROWS, _NORM_ROWS) for c in range(_NORM_UNROLL)]
            if final_norm:
                inv = [lax.rsqrt(jnp.mean(jnp.square(residual(rows)), axis=-1, keepdims=True) + _EPS)
                       for rows in chunks]
                for rows, scale in zip(chunks, inv):
                    o_ref[rows, :] = residual(rows) * scale * gain
            else:
                for rows in chunks:
                    o_ref[rows, :] = residual(rows)
            return carry

        lax.fori_loop(0, x_ref.shape[0] // slab, body, 0, unroll=True)


def _mlp(x, g, scale, shift, gate, w_up, w_down, layer, g_final, seq, final_norm):
    t, d = x.shape
    d_ff = w_up.shape[2]
    tm = min(_MLP_TM, seq)
    tf = min(_MLP_TF, d_ff)
    tiles_per_seq = seq // tm
    batch_vec = pl.BlockSpec((None, 1, d), lambda i, f: (i // tiles_per_seq, 0, 0))
    return pl.pallas_call(
        functools.partial(_mlp_kernel, final_norm=final_norm),
        grid=(t // tm, d_ff // tf),
        in_specs=[pl.BlockSpec((tm, d), lambda i, f: (i, 0)),
                  pl.BlockSpec((1, d), lambda i, f: (0, 0)),
                  batch_vec, batch_vec, batch_vec,
                  pl.BlockSpec((None, d, tf), lambda i, f: (layer, 0, f)),
                  pl.BlockSpec((None, tf, d), lambda i, f: (layer, f, 0)),
                  pl.BlockSpec((1, d), lambda i, f: (0, 0))],
        out_specs=pl.BlockSpec((tm, d), lambda i, f: (i, 0)),
        out_shape=jax.ShapeDtypeStruct((t, d), _F32),
        scratch_shapes=[pltpu.VMEM((tm, d), _BF16)],
        compiler_params=_params("parallel", "arbitrary"),
        name="mlp",
    )(x, g, scale, shift, gate, w_up, w_down, g_final)


def _log_sigmoid(x):
    return jnp.minimum(x, 0.0) - jnp.log1p(jnp.exp(-jnp.abs(x)))


def _mlstm_kernel(q_ref, k_ref, v_ref, og_ref, gt_ref, bg_ref, gh_ref, out_ref, c_ref, m_ref,
                  *, heads, dk, dv):
    @pl.when(pl.program_id(1) == 0)
    def _():
        c_ref[...] = jnp.zeros_like(c_ref)
        m_ref[...] = jnp.zeros_like(m_ref)

    chunk = _MLSTM_CHUNK
    src = lax.broadcasted_iota(jnp.int32, (chunk, chunk), 0)
    dst = lax.broadcasted_iota(jnp.int32, (chunk, chunk), 1)
    visible = src <= dst
    for sub in range(q_ref.shape[0] // chunk):
        rows = slice(sub * chunk, (sub + 1) * chunk)
        gates = gt_ref[rows, :] + bg_ref[...]
        b_all = jnp.dot((dst <= src).astype(_F32), _log_sigmoid(gates), preferred_element_type=_F32,
                        precision=lax.Precision.HIGHEST)
        _mlstm_chunk(q_ref, k_ref, v_ref, og_ref, gh_ref, out_ref, c_ref, m_ref, rows, gates, b_all, visible,
                     heads, dk, dv)


def _mlstm_chunk(q_ref, k_ref, v_ref, og_ref, gh_ref, out_ref, c_ref, m_ref, rows, gates, b_all, visible,
                 heads, dk, dv):
    chunk = _MLSTM_CHUNK
    gates_t = gates.T
    b_all_t = b_all.T
    for h in range(heads):
        b_row = b_all_t[heads + h:heads + h + 1, :]
        src_col = gates[:, h:h + 1] - b_all[:, heads + h:heads + h + 1]
        b_last = b_row[:, chunk - 1:chunk]
        m_prev = m_ref[h]
        c_aug = c_ref[h]
        qh = q_ref[rows, h * dk:(h + 1) * dk]
        kh = k_ref[rows, h * dk:(h + 1) * dk]
        vh = v_ref[rows, h * dv:(h + 1) * dv]

        dmat_t = jnp.where(visible, b_row + src_col, -jnp.inf)
        inter = b_row + m_prev
        m_j = jnp.maximum(inter, jnp.max(dmat_t, axis=0, keepdims=True))
        w_inter = jnp.exp(inter - m_j)
        s_t = lax.dot_general(kh, qh, _NT_DIMS, preferred_element_type=_F32) * jnp.exp(dmat_t - m_j)
        state_q = lax.dot_general(c_aug.astype(_BF16), qh, _NT_DIMS, preferred_element_type=_F32)
        num_t = (w_inter * state_q[:dv, :]
                 + lax.dot_general(vh, s_t.astype(_BF16), _TN_DIMS, preferred_element_type=_F32))
        den = w_inter * state_q[dv:dv + 1, :] + jnp.sum(s_t, axis=0, keepdims=True)
        h_t = num_t * (1.0 / jnp.maximum(jnp.abs(den), jnp.exp(-m_j)))

        dec = b_last + src_col
        m_new = jnp.maximum(b_last + m_prev, jnp.max(dec, axis=0, keepdims=True))
        kw = kh.astype(_F32) * jnp.exp(dec - m_new)
        carry_scale = jnp.exp(b_last + m_prev - m_new)
        c_ref[h, :dv, :] = carry_scale * c_aug[:dv, :] + lax.dot_general(
            vh, kw.astype(_BF16), _TN_DIMS, preferred_element_type=_F32)
        c_ref[h, dv:dv + 1, :] = carry_scale * c_aug[dv:dv + 1, :] + jnp.sum(kw, axis=0, keepdims=True)
        m_ref[h] = m_new

        y_t = h_t * lax.rsqrt(jnp.mean(h_t * h_t, axis=0, keepdims=True) + _EPS)
        og = og_ref[rows, h * dv:(h + 1) * dv].astype(_F32)
        out_ref[rows, h * dv:(h + 1) * dv] = (y_t.T * gh_ref[:, h * dv:(h + 1) * dv]
                                              * jax.nn.sigmoid(og)).astype(out_ref.dtype)


def _mlstm(proj, gates, b_gate, g_hnorm, batch, seq, heads, dk, dv):
    t = proj.shape[0]
    chunk = _MLSTM_CHUNK * _MLSTM_CHUNKS_PER_STEP
    n_chunks = seq // chunk
    qk_w, v_w = heads * dk, heads * dv
    assert v_w == 2 * qk_w

    def rows(b, c):
        return b * n_chunks + c

    return pl.pallas_call(
        functools.partial(_mlstm_kernel, heads=heads, dk=dk, dv=dv),
        grid=(batch, n_chunks),
        in_specs=[pl.BlockSpec((chunk, qk_w), lambda b, c: (rows(b, c), 0)),
                  pl.BlockSpec((chunk, qk_w), lambda b, c: (rows(b, c), 1)),
                  pl.BlockSpec((chunk, v_w), lambda b, c: (rows(b, c), 1)),
                  pl.BlockSpec((chunk, v_w), lambda b, c: (rows(b, c), 2)),
                  pl.BlockSpec((chunk, _LANES), lambda b, c: (rows(b, c), 0)),
                  pl.BlockSpec((1, _LANES), lambda b, c: (0, 0)),
                  pl.BlockSpec((1, v_w), lambda b, c: (0, 0))],
        out_specs=pl.BlockSpec((chunk, v_w), lambda b, c: (rows(b, c), 0)),
        out_shape=jax.ShapeDtypeStruct((t, v_w), _BF16),
        scratch_shapes=[pltpu.VMEM((heads, dv + _BF16_ROWS, dk), _F32),
                        pltpu.VMEM((heads, 1, 1), _F32)],
        compiler_params=_params("parallel", "arbitrary"),
        name="mlstm",
    )(proj, proj, proj, proj, gates, b_gate, g_hnorm)


def _qkv_proj_kernel(x_ref, g1_ref, sc1_ref, sh1_ref, gkv_ref, sckv_ref, shkv_ref, wqk_ref, wvt_ref,
                     qk_ref, vt_ref, h1_ref, hk_ref, *, n_q, n_k, q_scale):
    j = pl.program_id(1)

    @pl.when(j == 0)
    def _():
        gain1 = g1_ref[...] * (1.0 + sc1_ref[...])
        shift1 = sh1_ref[...]
        gain_kv = gkv_ref[...] * (1.0 + sckv_ref[...])
        shift_kv = shkv_ref[...]
        n_chunks = x_ref.shape[0] // _NORM_ROWS

        def body(r, carry):
            rows = pl.ds(pl.multiple_of(r * _NORM_ROWS, _NORM_ROWS), _NORM_ROWS)
            x = x_ref[rows, :]
            xn = x * lax.rsqrt(jnp.mean(x * x, axis=-1, keepdims=True) + _EPS)
            h1_ref[rows, :] = (xn * gain1 + shift1).astype(h1_ref.dtype)
            hk_ref[rows, :] = (xn * gain_kv + shift_kv).astype(hk_ref.dtype)
            return carry

        lax.fori_loop(0, n_chunks, body, 0, unroll=True)

    @pl.when(j < n_q)
    def _():
        acc = jnp.dot(h1_ref[...], wqk_ref[...], preferred_element_type=_F32)
        qk_ref[...] = (acc * q_scale).astype(qk_ref.dtype)

    @pl.when(jnp.logical_and(j >= n_q, j < n_q + n_k))
    def _():
        qk_ref[...] = jnp.dot(hk_ref[...], wqk_ref[...], preferred_element_type=_F32).astype(qk_ref.dtype)

    @pl.when(j >= n_q + n_k)
    def _():
        vt_ref[...] = lax.dot_general(wvt_ref[...], hk_ref[...], _NT_DIMS,
                                      preferred_element_type=_F32).astype(vt_ref.dtype)


def _qkv_proj(x, g1, sc1, sh1, gkv, sckv, shkv, wqk, n_qc, wvt, q_scale, batch, seq):
    t, d = x.shape
    n_qkc, n_vc = wqk.shape[1], wvt.shape[0]
    tm = min(_PROJ_TM, seq)
    tn = _QKV_TN
    tiles_per_seq = seq // tm
    n_q, n_qk, n_v = n_qc // tn, n_qkc // tn, n_vc // tn

    def qk_col(j):
        return jnp.minimum(j, n_qk - 1)

    def vt_row(j):
        return jnp.maximum(j - n_qk, 0)

    batch_vec = pl.BlockSpec((None, 1, d), lambda i, j: (i // tiles_per_seq, 0, 0))
    row_vec = pl.BlockSpec((1, d), lambda i, j: (0, 0))
    return pl.pallas_call(
        functools.partial(_qkv_proj_kernel, n_q=n_q, n_k=n_qk - n_q, q_scale=q_scale),
        grid=(t // tm, n_qk + n_v),
        in_specs=[pl.BlockSpec((tm, d), lambda i, j: (i, 0)),
                  row_vec, batch_vec, batch_vec, row_vec, batch_vec, batch_vec,
                  pl.BlockSpec((d, tn), lambda i, j: (0, qk_col(j))),
                  pl.BlockSpec((tn, d), lambda i, j: (vt_row(j), 0))],
        out_specs=[pl.BlockSpec((tm, tn), lambda i, j: (i, qk_col(j))),
                   pl.BlockSpec((None, tn, tm), lambda i, j: (i // tiles_per_seq, vt_row(j), i % tiles_per_seq))],
        out_shape=[jax.ShapeDtypeStruct((t, n_qkc), _BF16),
                   jax.ShapeDtypeStruct((batch, n_vc, seq), _BF16)],
        scratch_shapes=[pltpu.VMEM((tm, d), _BF16), pltpu.VMEM((tm, d), _BF16)],
        compiler_params=_params("parallel", "arbitrary"),
        name="qkv_proj",
    )(x, g1, sc1, sh1, gkv, sckv, shkv, wqk, wvt)


def _attn_kernel(q_ref, k_ref, vt_ref, slope_ref, lam_ref, g_ref, o_ref, qa_ref, kpos_ref, s_ref, bmax_ref, m_ref,
                 acc_ref, *, dh, lam_init):
    qi = pl.program_id(2)
    blk = q_ref.shape[0]
    slope2 = slope_ref[:, 0:1] * _LOG2E

    dv = vt_ref.shape[0]

    @pl.when(qi == 0)
    def _():
        lane = lax.broadcasted_iota(jnp.int32, (blk, dh), 1)
        key_bias = lax.broadcasted_iota(jnp.int32, (blk, dh), 0).astype(_F32) * slope2
        part_hi = key_bias.astype(_BF16).astype(_F32)
        part_mid = (key_bias - part_hi).astype(_BF16).astype(_F32)
        part_lo = key_bias - part_hi - part_mid
        kpos_ref[...] = jnp.where(lane == 0, part_hi, jnp.where(lane == 1, part_mid,
                                                                jnp.where(lane == 2, part_lo, 0.0))).astype(_BF16)
        for s_idx in range(2):
            qa_ref[s_idx, :, dh:] = jnp.where(lane < 3, 1.0, 0.0).astype(_BF16)

    for s_idx in range(2):
        qa_ref[s_idx, :, :dh] = q_ref[:, s_idx * dh:(s_idx + 1) * dh]

    acc_ref[...] = jnp.zeros_like(acc_ref)
    m_ref[...] = jnp.full_like(m_ref, -jnp.inf)

    def scores(kj, masked, buf):
        keys = pl.ds(pl.multiple_of(kj * blk, blk), blk)
        for s_idx in range(2):
            k_cat = jnp.concatenate([k_ref[keys, s_idx * dh:(s_idx + 1) * dh], kpos_ref[...]], axis=1)
            st = lax.dot_general(k_cat, qa_ref[s_idx], _NT_DIMS, preferred_element_type=_F32)
            if masked:
                key = lax.broadcasted_iota(jnp.int32, (blk, blk), 0)
                qry = lax.broadcasted_iota(jnp.int32, (blk, blk), 1)
                st = jnp.where(key <= qry, st, -jnp.inf)
            s_ref[buf, s_idx] = st
            bmax_ref[buf, s_idx] = jnp.max(st, axis=0, keepdims=True)

    def accumulate(kj, buf):
        keys = pl.ds(pl.multiple_of(kj * blk, blk), blk)
        vt_aug = jnp.concatenate([vt_ref[:, keys], jnp.ones((_BF16_ROWS, blk), _BF16)], axis=0)
        block_bias = slope2 * jnp.full((1, 1), (kj - qi) * blk, jnp.int32).astype(_F32)
        for s_idx in range(2):
            m_old = m_ref[s_idx]
            m_new = jnp.maximum(m_old, bmax_ref[buf, s_idx] + block_bias)
            pt = jnp.exp2(s_ref[buf, s_idx] - (m_new - block_bias))
            acc_ref[s_idx] = acc_ref[s_idx] * jnp.exp2(m_old - m_new) + jnp.dot(
                vt_aug, pt.astype(_BF16), preferred_element_type=_F32)
            m_ref[s_idx] = m_new

    def block_of(t):
        return jnp.where(t == 0, qi, t - 1)

    def pair(u):
        t = 2 * u
        scores(t, False, 1)
        accumulate(block_of(t), 0)
        scores(t + 1, False, 0)
        accumulate(t, 1)

    def four_pairs(u, carry):
        for p in range(4):
            pair(4 * u + p)
        return carry

    scores(qi, True, 0)
    n_pairs = lax.shift_right_logical(qi, 1)
    n_fours = lax.shift_right_logical(n_pairs, 2)
    lax.fori_loop(0, n_fours, four_pairs, 0)

    @pl.when((n_pairs & 2) == 2)
    def _():
        pair(4 * n_fours)
        pair(4 * n_fours + 1)

    @pl.when((n_pairs & 1) == 1)
    def _():
        pair(n_pairs - 1)

    t_rest = 2 * n_pairs
    one_more = (qi & 1) == 1

    def finish():
        lp = lam_ref[...]
        lam = (jnp.exp(jnp.sum(lp[0:1, :] * lp[1:2, :], axis=-1, keepdims=True))
               - jnp.exp(jnp.sum(lp[2:3, :] * lp[3:4, :], axis=-1, keepdims=True)) + lam_init)
        ot = (acc_ref[0, :dv, :] * (1.0 / acc_ref[0, dv:dv + 1, :])
              - lam * (acc_ref[1, :dv, :] * (1.0 / acc_ref[1, dv:dv + 1, :])))
        ot = ot * lax.rsqrt(jnp.mean(ot * ot, axis=0, keepdims=True) + _EPS)
        o_ref[...] = (ot.T * g_ref[...] * (1.0 - lam_init)).astype(o_ref.dtype)

    @pl.when(one_more)
    def _():
        scores(t_rest, False, 1)
        accumulate(block_of(t_rest), 0)
        accumulate(t_rest, 1)
        finish()

    @pl.when(jnp.logical_not(one_more))
    def _():
        accumulate(block_of(t_rest), 0)
        finish()


def _diff_attention(q, k, k_head0, vt, lam_p, g_subln, batch, seq, heads, dh, lam_init):
    t = q.shape[0]
    dv = 2 * dh
    blk = min(_ATTN_BLOCK, seq)
    n_blk = seq // blk
    slopes = jnp.asarray([2.0 ** (-8.0 * (h + 1) / heads) for h in range(heads)], _F32)
    slopes = jnp.broadcast_to(slopes[:, None, None], (heads, 1, _LANES))
    return pl.pallas_call(
        functools.partial(_attn_kernel, dh=dh, lam_init=lam_init),
        grid=(batch, heads, n_blk),
        in_specs=[pl.BlockSpec((blk, dv), lambda b, h, i: (b * n_blk + i, h)),
                  pl.BlockSpec((seq, dv), lambda b, h, i: (b, k_head0 + h)),
                  pl.BlockSpec((None, dv, seq), lambda b, h, i: (b, h, 0)),
                  pl.BlockSpec((None, 1, _LANES), lambda b, h, i: (h, 0, 0)),
                  pl.BlockSpec((4, dh), lambda b, h, i: (0, 0)),
                  pl.BlockSpec((1, dv), lambda b, h, i: (0, h))],
        out_specs=pl.BlockSpec((blk, dv), lambda b, h, i: (b * n_blk + i, h)),
        out_shape=jax.ShapeDtypeStruct((t, heads * dv), _BF16),
        scratch_shapes=[pltpu.VMEM((2, blk, 2 * dh), _BF16),
                        pltpu.VMEM((blk, dh), _BF16),
                        pltpu.VMEM((2, 2, blk, blk), _F32),
                        pltpu.VMEM((2, 2, 1, blk), _F32),
                        pltpu.VMEM((2, 1, blk), _F32),
                        pltpu.VMEM((2, dv + _BF16_ROWS, blk), _F32)],
        compiler_params=_params("parallel", "parallel", "arbitrary"),
        name="diff_attention",
    )(q, k, vt, slopes, lam_p, g_subln)


def kernel(x, c, w_mod, b_mod, g_norm, w_up, w_down, w_a_in, b_a_gate, g_a_hnorm, w_a_out, g_kv, w_kv_mod,
           b_kv_mod, w_kv, w_b_q, b_lambda, g_b_subln, w_b_out, g_final):
    batch, seq, d = x.shape
    depth = w_mod.shape[0]
    n_a = w_a_in.shape[0]
    m_heads = b_a_gate.shape[1] // 2
    m_dv = g_a_hnorm.shape[1] // m_heads
    m_dk = (w_a_in.shape[2] - 2 * m_heads * m_dv - 2 * m_heads) // (2 * m_heads)
    d_dh = b_lambda.shape[2]
    d_heads = w_b_q.shape[2] // (2 * d_dh)
    t = batch * seq

    xt = x.reshape(t, d)
    c_rows = jnp.zeros((_SUBLANES, d), _F32).at[:batch].set(c)

    mod = _modvec(c_rows, w_mod, b_mod.reshape(depth, 1, 6 * d))[:, :batch]
    kv_mod = _modvec(c_rows, w_kv_mod[None], b_kv_mod.reshape(1, 1, 2 * d))[0, :batch]

    def vec(v):
        return v.reshape(batch, 1, d)

    w_up_b, w_down_b = w_up.astype(_BF16), w_down.astype(_BF16)
    w_a_in_b, w_a_out_b = w_a_in.astype(_BF16), w_a_out.astype(_BF16)
    w_b_q_b, w_b_out_b = w_b_q.astype(_BF16), w_b_out.astype(_BF16)

    ones_d = jnp.ones((1, d), _F32)
    for l in range(depth):
        sh1, sc1, gt1, sh2, sc2, gt2 = (vec(m) for m in jnp.split(mod[l], 6, axis=-1))
        if l < n_a:
            qk_w, v_w = m_heads * m_dk, m_heads * m_dv
            main_w = 2 * qk_w + 2 * v_w
            col_scale = jnp.concatenate([jnp.ones((qk_w,), _F32), jnp.full((qk_w,), m_dk ** -0.5, _F32),
                                         jnp.ones((2 * v_w,), _F32)]).reshape(1, main_w)
            w_gate = jnp.zeros((d, _LANES), _BF16).at[:, :2 * m_heads].set(w_a_in_b[l, :, main_w:])
            proj, gates = _norm_mod_matmul(xt, g_norm[l, 0][None], sc1, sh1, w_a_in_b, l, col_scale, seq,
                                           w_narrow=w_gate)
            b_gate = jnp.zeros((1, _LANES), _F32).at[0, :2 * m_heads].set(b_a_gate[l])
            mix_in = _mlstm(proj, gates, b_gate, g_a_hnorm[l][None], batch, seq, m_heads, m_dk, m_dv)
            xt = _matmul_residual(mix_in, w_a_out_b, l, xt, gt1, seq)
        else:
            j = l - n_a
            lam_init = 0.8 - 0.6 * math.exp(-0.3 * l)
            q_scale = d_dh ** -0.5 * _LOG2E
            k_w = w_b_q.shape[2]
            if l == n_a:
                kv_shift, kv_scale = (vec(m) for m in jnp.split(kv_mod, 2, axis=-1))
                w_qk = jnp.concatenate([w_b_q[j], w_kv[:, :k_w]], axis=1).astype(_BF16)
                q, vt_sh = _qkv_proj(xt, g_norm[l, 0][None], sc1, sh1, g_kv[None], kv_scale, kv_shift,
                                     w_qk, k_w, w_kv[:, k_w:].T.astype(_BF16), q_scale, batch, seq)
                k_sh = q
            else:
                q = _norm_mod_matmul(xt, g_norm[l, 0][None], sc1, sh1, w_b_q_b, j,
                                     jnp.full((1, k_w), q_scale, _F32), seq)
            mix_in = _diff_attention(q, k_sh, d_heads, vt_sh, b_lambda[j], g_b_subln[j][None], batch, seq,
                                     d_heads, d_dh, lam_init)
            xt = _matmul_residual(mix_in, w_b_out_b, j, xt, gt1, seq)
        last = l == depth - 1
        xt = _mlp(xt, g_norm[l, 1][None], sc2, sh2, gt2, w_up_b, w_down_b, l,
                  g_final[None] if last else ones_d, seq, last)
    return xt.reshape(batch, seq, d)
```

```python
import functools
import math

import jax
import jax.numpy as jnp
from jax import lax
from jax.experimental import pallas as pl
from jax.experimental.pallas import tpu as pltpu

_EPS = 1e-6
_F32 = jnp.float32
_BF16 = jnp.bfloat16

_LANES = 128
_SUBLANES = 8
_BF16_ROWS = 16
_VMEM_LIMIT_BYTES = 59 * 1024 * 1024

_MODVEC_TN = 1024
_PROJ_TM = 1024
_IN_PROJ_TN = 2048
_OUT_PROJ_TM = 512
_OUT_PROJ_TN = 2048
_MLP_TM = 1024
_MLP_TF = 1024
_NORM_ROWS = 16
_NORM_UNROLL = 8
_MLSTM_CHUNK = 128
_MLSTM_CHUNKS_PER_STEP = 2
_ATTN_BLOCK = 512
_QKV_TN = 1024

_LOG2E = math.log2(math.e)

_NT_DIMS = (((1,), (1,)), ((), ()))
_TN_DIMS = (((0,), (0,)), ((), ()))


def _params(*semantics):
    return pltpu.CompilerParams(dimension_semantics=semantics, vmem_limit_bytes=_VMEM_LIMIT_BYTES)


def _modvec_kernel(c_ref, w_ref, b_ref, o_ref):
    o_ref[...] = jnp.dot(c_ref[...].astype(_BF16), w_ref[...].astype(_BF16),
                         preferred_element_type=_F32) + b_ref[...]


def _modvec(c_rows, w, b):
    n_l, d, n = w.shape
    tn = _MODVEC_TN
    return pl.pallas_call(
        _modvec_kernel,
        grid=(n_l, n // tn),
        in_specs=[pl.BlockSpec((_SUBLANES, d), lambda l, j: (0, 0)),
                  pl.BlockSpec((None, d, tn), lambda l, j: (l, 0, j)),
                  pl.BlockSpec((None, 1, tn), lambda l, j: (l, 0, j))],
        out_specs=pl.BlockSpec((None, _SUBLANES, tn), lambda l, j: (l, 0, j)),
        out_shape=jax.ShapeDtypeStruct((n_l, _SUBLANES, n), _F32),
        compiler_params=_params("parallel", "parallel"),
        name="modvec",
    )(c_rows, w, b)


def _norm_mod_tile(x_ref, gain, shift, h_ref):
    n_chunks = x_ref.shape[0] // _NORM_ROWS

    def body(r, carry):
        rows = pl.ds(pl.multiple_of(r * _NORM_ROWS, _NORM_ROWS), _NORM_ROWS)
        x = x_ref[rows, :]
        inv = lax.rsqrt(jnp.mean(x * x, axis=-1, keepdims=True) + _EPS)
        h_ref[rows, :] = (x * inv * gain + shift).astype(h_ref.dtype)
        return carry

    lax.fori_loop(0, n_chunks, body, 0, unroll=True)


def _norm_mod_matmul_kernel(x_ref, g_ref, sc_ref, sh_ref, w_ref, cs_ref, *rest, narrow):
    if narrow:
        wn_ref, o_ref, on_ref, h_ref = rest
    else:
        o_ref, h_ref = rest

    @pl.when(pl.program_id(1) == 0)
    def _():
        _norm_mod_tile(x_ref, g_ref[...] * (1.0 + sc_ref[...]), sh_ref[...], h_ref)
        if narrow:
            on_ref[...] = jnp.dot(h_ref[...], wn_ref[...], preferred_element_type=_F32)

    acc = jnp.dot(h_ref[...], w_ref[...], preferred_element_type=_F32)
    o_ref[...] = (acc * cs_ref[...]).astype(o_ref.dtype)


def _norm_mod_matmul(x, g, scale, shift, w, layer, col_scale, seq, w_narrow=None):
    t, d = x.shape
    n = col_scale.shape[1]
    tm = min(_PROJ_TM, seq)
    tn = min(_IN_PROJ_TN, n)
    tiles_per_seq = seq // tm
    narrow = w_narrow is not None
    batch_vec = pl.BlockSpec((None, 1, d), lambda i, j: (i // tiles_per_seq, 0, 0))
    in_specs = [pl.BlockSpec((tm, d), lambda i, j: (i, 0)),
                pl.BlockSpec((1, d), lambda i, j: (0, 0)),
                batch_vec, batch_vec,
                pl.BlockSpec((None, d, tn), lambda i, j: (layer, 0, j)),
                pl.BlockSpec((1, tn), lambda i, j: (0, j))]
    out_specs = [pl.BlockSpec((tm, tn), lambda i, j: (i, j))]
    out_shape = [jax.ShapeDtypeStruct((t, n), _BF16)]
    operands = [x, g, scale, shift, w, col_scale]
    if narrow:
        in_specs.append(pl.BlockSpec((d, _LANES), lambda i, j: (0, 0)))
        out_specs.append(pl.BlockSpec((tm, _LANES), lambda i, j: (i, 0)))
        out_shape.append(jax.ShapeDtypeStruct((t, _LANES), _F32))
        operands.append(w_narrow)
    outs = pl.pallas_call(
        functools.partial(_norm_mod_matmul_kernel, narrow=narrow),
        grid=(t // tm, n // tn),
        in_specs=in_specs,
        out_specs=out_specs,
        out_shape=out_shape,
        scratch_shapes=[pltpu.VMEM((tm, d), _BF16)],
        compiler_params=_params("parallel", "arbitrary"),
        name="norm_mod_matmul",
    )(*operands)
    return outs if narrow else outs[0]


def _matmul_residual_kernel(a_ref, w_ref, x_ref, gate_ref, o_ref):
    acc = jnp.dot(a_ref[...], w_ref[...], preferred_element_type=_F32)
    o_ref[...] = x_ref[...] + gate_ref[...] * acc


def _matmul_residual(a, w, layer, x, gate, seq):
    t, k = a.shape
    d = w.shape[2]
    tm = min(_OUT_PROJ_TM, seq)
    tn = min(_OUT_PROJ_TN, d)
    tiles_per_seq = seq // tm
    return pl.pallas_call(
        _matmul_residual_kernel,
        grid=(t // tm, d // tn),
        in_specs=[pl.BlockSpec((tm, k), lambda i, j: (i, 0)),
                  pl.BlockSpec((None, k, tn), lambda i, j: (layer, 0, j)),
                  pl.BlockSpec((tm, tn), lambda i, j: (i, j)),
                  pl.BlockSpec((None, 1, tn), lambda i, j: (i // tiles_per_seq, 0, j))],
        out_specs=pl.BlockSpec((tm, tn), lambda i, j: (i, j)),
        out_shape=jax.ShapeDtypeStruct((t, d), _F32),
        compiler_params=_params("parallel", "parallel"),
        name="matmul_residual",
    )(a, w, x, gate)


def _mlp_kernel(x_ref, g_ref, sc_ref, sh_ref, gate_ref, wu_ref, wd_ref, gf_ref, o_ref, h_ref, *, final_norm):
    f = pl.program_id(1)

    @pl.when(f == 0)
    def _():
        _norm_mod_tile(x_ref, g_ref[...] * (1.0 + sc_ref[...]), sh_ref[...], h_ref)
        o_ref[...] = jnp.zeros_like(o_ref)

    u = jnp.dot(h_ref[...], wu_ref[...], preferred_element_type=_F32)
    u = jnp.square(jnp.maximum(u, 0.0)).astype(_BF16)
    o_ref[...] += jnp.dot(u, wd_ref[...], preferred_element_type=_F32)

    @pl.when(f == pl.num_programs(1) - 1)
    def _():
        gate = gate_ref[...]
        gain = gf_ref[...]
        slab = _NORM_ROWS * _NORM_UNROLL

        def residual(rows):
            return x_ref[rows, :] + gate * o_ref[rows, :]

        def body(r, carry):
            base = pl.multiple_of(r * slab, slab)
            chunks = [pl.ds(base + c * _NORM_ROWS, _NORM_ROWS) for c in range(_NORM_UNROLL)]
            if final_norm:
                inv = [lax.rsqrt(jnp.mean(jnp.square(residual(rows)), axis=-1, keepdims=True) + _EPS)
                       for rows in chunks]
                for rows, scale in zip(chunks, inv):
                    o_ref[rows, :] = residual(rows) * scale * gain
            else:
                for rows in chunks:
                    o_ref[rows, :] = residual(rows)
            return carry

        lax.fori_loop(0, x_ref.shape[0] // slab, body, 0, unroll=True)


def _mlp(x, g, scale, shift, gate, w_up, w_down, layer, g_final, seq, final_norm):
    t, d = x.shape
    d_ff = w_up.shape[2]
    tm = min(_MLP_TM, seq)
    tf = min(_MLP_TF, d_ff)
    tiles_per_seq = seq // tm
    batch_vec = pl.BlockSpec((None, 1, d), lambda i, f: (i // tiles_per_seq, 0, 0))
    return pl.pallas_call(
        functools.partial(_mlp_kernel, final_norm=final_norm),
        grid=(t // tm, d_ff // tf),
        in_specs=[pl.BlockSpec((tm, d), lambda i, f: (i, 0)),
                  pl.BlockSpec((1, d), lambda i, f: (0, 0)),
                  batch_vec, batch_vec, batch_vec,
                  pl.BlockSpec((None, d, tf), lambda i, f: (layer, 0, f)),
                  pl.BlockSpec((None, tf, d), lambda i, f: (layer, f, 0)),
                  pl.BlockSpec((1, d), lambda i, f: (0, 0))],
        out_specs=pl.BlockSpec((tm, d), lambda i, f: (i, 0)),
        out_shape=jax.ShapeDtypeStruct((t, d), _F32),
        scratch_shapes=[pltpu.VMEM((tm, d), _BF16)],
        compiler_params=_params("parallel", "arbitrary"),
        name="mlp",
    )(x, g, scale, shift, gate, w_up, w_down, g_final)


def _log_sigmoid(x):
    return jnp.minimum(x, 0.0) - jnp.log1p(jnp.exp(-jnp.abs(x)))


def _mlstm_kernel(q_ref, k_ref, v_ref, og_ref, gt_ref, bg_ref, gh_ref, out_ref, c_ref, m_ref,
                  *, heads, dk, dv):
    @pl.when(pl.program_id(1) == 0)
    def _():
        c_ref[...] = jnp.zeros_like(c_ref)
        m_ref[...] = jnp.zeros_like(m_ref)

    chunk = _MLSTM_CHUNK
    src = lax.broadcasted_iota(jnp.int32, (chunk, chunk), 0)
    dst = lax.broadcasted_iota(jnp.int32, (chunk, chunk), 1)
    visible = src <= dst
    for sub in range(q_ref.shape[0] // chunk):
        rows = slice(sub * chunk, (sub + 1) * chunk)
        gates = gt_ref[rows, :] + bg_ref[...]
        b_all = jnp.dot((dst <= src).astype(_F32), _log_sigmoid(gates), preferred_element_type=_F32,
                        precision=lax.Precision.HIGHEST)
        _mlstm_chunk(q_ref, k_ref, v_ref, og_ref, gh_ref, out_ref, c_ref, m_ref, rows, gates, b_all, visible,
                     heads, dk, dv)


def _mlstm_chunk(q_ref, k_ref, v_ref, og_ref, gh_ref, out_ref, c_ref, m_ref, rows, gates, b_all, visible,
                 heads, dk, dv):
    chunk = _MLSTM_CHUNK
    gates_t = gates.T
    b_all_t = b_all.T
    for h in range(heads):
        b_row = b_all_t[heads + h:heads + h + 1, :]
        src_col = gates[:, h:h + 1] - b_all[:, heads + h:heads + h + 1]
        b_last = b_row[:, chunk - 1:chunk]
        m_prev = m_ref[h]
        c_aug = c_ref[h]
        qh = q_ref[rows, h * dk:(h + 1) * dk]
        kh = k_ref[rows, h * dk:(h + 1) * dk]
        vh = v_ref[rows, h * dv:(h + 1) * dv]

        dmat_t = jnp.where(visible, b_row + src_col, -jnp.inf)
        inter = b_row + m_prev
        m_j = jnp.maximum(inter, jnp.max(dmat_t, axis=0, keepdims=True))
        w_inter = jnp.exp(inter - m_j)
        s_t = lax.dot_general(kh, qh, _NT_DIMS, preferred_element_type=_F32) * jnp.exp(dmat_t - m_j)
        state_q = lax.dot_general(c_aug.astype(_BF16), qh, _NT_DIMS, preferred_element_type=_F32)
        num_t = (w_inter * state_q[:dv, :]
                 + lax.dot_general(vh, s_t.astype(_BF16), _TN_DIMS, preferred_element_type=_F32))
        den = w_inter * state_q[dv:dv + 1, :] + jnp.sum(s_t, axis=0, keepdims=True)
        h_t = num_t * (1.0 / jnp.maximum(jnp.abs(den), jnp.exp(-m_j)))

        dec = b_last + src_col
        m_new = jnp.maximum(b_last + m_prev, jnp.max(dec, axis=0, keepdims=True))
        kw = kh.astype(_F32) * jnp.exp(dec - m_new)
        carry_scale = jnp.exp(b_last + m_prev - m_new)
        c_ref[h, :dv, :] = carry_scale * c_aug[:dv, :] + lax.dot_general(
            vh, kw.astype(_BF16), _TN_DIMS, preferred_element_type=_F32)
        c_ref[h, dv:dv + 1, :] = carry_scale * c_aug[dv:dv + 1, :] + jnp.sum(kw, axis=0, keepdims=True)
        m_ref[h] = m_new

        y_t = h_t * lax.rsqrt(jnp.mean(h_t * h_t, axis=0, keepdims=True) + _EPS)
        og = og_ref[rows, h * dv:(h + 1) * dv].astype(_F32)
        out_ref[rows, h * dv:(h + 1) * dv] = (y_t.T * gh_ref[:, h * dv:(h + 1) * dv]
                                              * jax.nn.sigmoid(og)).astype(out_ref.dtype)


def _mlstm(proj, gates, b_gate, g_hnorm, batch, seq, heads, dk, dv):
    t = proj.shape[0]
    chunk = _MLSTM_CHUNK * _MLSTM_CHUNKS_PER_STEP
    n_chunks = seq // chunk
    qk_w, v_w = heads * dk, heads * dv
    assert v_w == 2 * qk_w

    def rows(b, c):
        return b * n_chunks + c

    return pl.pallas_call(
        functools.partial(_mlstm_kernel, heads=heads, dk=dk, dv=dv),
        grid=(batch, n_chunks),
        in_specs=[pl.BlockSpec((chunk, qk_w), lambda b, c: (rows(b, c), 0)),
                  pl.BlockSpec((chunk, qk_w), lambda b, c: (rows(b, c), 1)),
                  pl.BlockSpec((chunk, v_w), lambda b, c: (rows(b, c), 1)),
                  pl.BlockSpec((chunk, v_w), lambda b, c: (rows(b, c), 2)),
                  pl.BlockSpec((chunk, _LANES), lambda b, c: (rows(b, c), 0)),
                  pl.BlockSpec((1, _LANES), lambda b, c: (0, 0)),
                  pl.BlockSpec((1, v_w), lambda b, c: (0, 0))],
        out_specs=pl.BlockSpec((chunk, v_w), lambda b, c: (rows(b, c), 0)),
        out_shape=jax.ShapeDtypeStruct((t, v_w), _BF16),
        scratch_shapes=[pltpu.VMEM((heads, dv + _BF16_ROWS, dk), _F32),
                        pltpu.VMEM((heads, 1, 1), _F32)],
        compiler_params=_params("parallel", "arbitrary"),
        name="mlstm",
    )(proj, proj, proj, proj, gates, b_gate, g_hnorm)


def _qkv_proj_kernel(x_ref, g1_ref, sc1_ref, sh1_ref, gkv_ref, sckv_ref, shkv_ref, wqk_ref, wvt_ref,
                     qk_ref, vt_ref, h1_ref, hk_ref, *, n_q, n_k, q_scale):
    j = pl.program_id(1)

    @pl.when(j == 0)
    def _():
        gain1 = g1_ref[...] * (1.0 + sc1_ref[...])
        shift1 = sh1_ref[...]
        gain_kv = gkv_ref[...] * (1.0 + sckv_ref[...])
        shift_kv = shkv_ref[...]
        n_chunks = x_ref.shape[0] // _NORM_ROWS

        def body(r, carry):
            rows = pl.ds(pl.multiple_of(r * _NORM_ROWS, _NORM_ROWS), _NORM_ROWS)
            x = x_ref[rows, :]
            xn = x * lax.rsqrt(jnp.mean(x * x, axis=-1, keepdims=True) + _EPS)
            h1_ref[rows, :] = (xn * gain1 + shift1).astype(h1_ref.dtype)
            hk_ref[rows, :] = (xn * gain_kv + shift_kv).astype(hk_ref.dtype)
            return carry

        lax.fori_loop(0, n_chunks, body, 0, unroll=True)

    @pl.when(j < n_q)
    def _():
        acc = jnp.dot(h1_ref[...], wqk_ref[...], preferred_element_type=_F32)
        qk_ref[...] = (acc * q_scale).astype(qk_ref.dtype)

    @pl.when(jnp.logical_and(j >= n_q, j < n_q + n_k))
    def _():
        qk_ref[...] = jnp.dot(hk_ref[...], wqk_ref[...], preferred_element_type=_F32).astype(qk_ref.dtype)

    @pl.when(j >= n_q + n_k)
    def _():
        vt_ref[...] = lax.dot_general(wvt_ref[...], hk_ref[...], _NT_DIMS,
                                      preferred_element_type=_F32).astype(vt_ref.dtype)


def _qkv_proj(x, g1, sc1, sh1, gkv, sckv, shkv, wqk, n_qc, wvt, q_scale, batch, seq):
    t, d = x.shape
    n_qkc, n_vc = wqk.shape[1], wvt.shape[0]
    tm = min(_PROJ_TM, seq)
    tn = _QKV_TN
    tiles_per_seq = seq // tm
    n_q, n_qk, n_v = n_qc // tn, n_qkc // tn, n_vc // tn

    def qk_col(j):
        return jnp.minimum(j, n_qk - 1)

    def vt_row(j):
        return jnp.maximum(j - n_qk, 0)

    batch_vec = pl.BlockSpec((None, 1, d), lambda i, j: (i // tiles_per_seq, 0, 0))
    row_vec = pl.BlockSpec((1, d), lambda i, j: (0, 0))
    return pl.pallas_call(
        functools.partial(_qkv_proj_kernel, n_q=n_q, n_k=n_qk - n_q, q_scale=q_scale),
        grid=(t // tm, n_qk + n_v),
        in_specs=[pl.BlockSpec((tm, d), lambda i, j: (i, 0)),
                  row_vec, batch_vec, batch_vec, row_vec, batch_vec, batch_vec,
                  pl.BlockSpec((d, tn), lambda i, j: (0, qk_col(j))),
                  pl.BlockSpec((tn, d), lambda i, j: (vt_row(j), 0))],
        out_specs=[pl.BlockSpec((tm, tn), lambda i, j: (i, qk_col(j))),
                   pl.BlockSpec((None, tn, tm), lambda i, j: (i // tiles_per_seq, vt_row(j), i % tiles_per_seq))],
        out_shape=[jax.ShapeDtypeStruct((t, n_qkc), _BF16),
                   jax.ShapeDtypeStruct((batch, n_vc, seq), _BF16)],
        scratch_shapes=[pltpu.VMEM((tm, d), _BF16), pltpu.VMEM((tm, d), _BF16)],
        compiler_params=_params("parallel", "arbitrary"),
        name="qkv_proj",
    )(x, g1, sc1, sh1, gkv, sckv, shkv, wqk, wvt)


def _attn_kernel(q_ref, k_ref, vt_ref, slope_ref, lam_ref, g_ref, o_ref, qa_ref, kpos_ref, s_ref, bmax_ref, m_ref,
                 acc_ref, *, dh, lam_init):
    qi = pl.program_id(2)
    blk = q_ref.shape[0]
    slope2 = slope_ref[:, 0:1] * _LOG2E

    dv = vt_ref.shape[0]

    @pl.when(qi == 0)
    def _():
        lane = lax.broadcasted_iota(jnp.int32, (blk, dh), 1)
        key_bias = lax.broadcasted_iota(jnp.int32, (blk, dh), 0).astype(_F32) * slope2
        part_hi = key_bias.astype(_BF16).astype(_F32)
        part_mid = (key_bias - part_hi).astype(_BF16).astype(_F32)
        part_lo = key_bias - part_hi - part_mid
        kpos_ref[...] = jnp.where(lane == 0, part_hi, jnp.where(lane == 1, part_mid,
                                                                jnp.where(lane == 2, part_lo, 0.0))).astype(_BF16)
        for s_idx in range(2):
            qa_ref[s_idx, :, dh:] = jnp.where(lane < 3, 1.0, 0.0).astype(_BF16)

    for s_idx in range(2):
        qa_ref[s_idx, :, :dh] = q_ref[:, s_idx * dh:(s_idx + 1) * dh]

    acc_ref[...] = jnp.zeros_like(acc_ref)
    m_ref[...] = jnp.full_like(m_ref, -jnp.inf)

    def scores(kj, masked, buf):
        keys = pl.ds(pl.multiple_of(kj * blk, blk), blk)
        for s_idx in range(2):
            k_cat = jnp.concatenate([k_ref[keys, s_idx * dh:(s_idx + 1) * dh], kpos_ref[...]], axis=1)
            st = lax.dot_general(k_cat, qa_ref[s_idx], _NT_DIMS, preferred_element_type=_F32)
            if masked:
                key = lax.broadcasted_iota(jnp.int32, (blk, blk), 0)
                qry = lax.broadcasted_iota(jnp.int32, (blk, blk), 1)
                st = jnp.where(key <= qry, st, -jnp.inf)
            s_ref[buf, s_idx] = st
            bmax_ref[buf, s_idx] = jnp.max(st, axis=0, keepdims=True)

    def accumulate(kj, buf):
        keys = pl.ds(pl.multiple_of(kj * blk, blk), blk)
        vt_aug = jnp.concatenate([vt_ref[:, keys], jnp.ones((_BF16_ROWS, blk), _BF16)], axis=0)
        block_bias = slope2 * jnp.full((1, 1), (kj - qi) * blk, jnp.int32).astype(_F32)
        for s_idx in range(2):
            m_old = m_ref[s_idx]
            m_new = jnp.maximum(m_old, bmax_ref[buf, s_idx] + block_bias)
            pt = jnp.exp2(s_ref[buf, s_idx] - (m_new - block_bias))
            acc_ref[s_idx] = acc_ref[s_idx] * jnp.exp2(m_old - m_new) + jnp.dot(
                vt_aug, pt.astype(_BF16), preferred_element_type=_F32)
            m_ref[s_idx] = m_new

    def block_of(t):
        return jnp.where(t == 0, qi, t - 1)

    def pair(u):
        t = 2 * u
        scores(t, False, 1)
        accumulate(block_of(t), 0)
        scores(t + 1, False, 0)
        accumulate(t, 1)

    def four_pairs(u, carry):
        for p in range(4):
            pair(4 * u + p)
        return carry

    scores(qi, True, 0)
    n_pairs = lax.shift_right_logical(qi, 1)
    n_fours = lax.shift_right_logical(n_pairs, 2)
    lax.fori_loop(0, n_fours, four_pairs, 0)

    @pl.when((n_pairs & 2) == 2)
    def _():
        pair(4 * n_fours)
        pair(4 * n_fours + 1)

    @pl.when((n_pairs & 1) == 1)
    def _():
        pair(n_pairs - 1)

    t_rest = 2 * n_pairs
    one_more = (qi & 1) == 1

    def finish():
        lp = lam_ref[...]
        lam = (jnp.exp(jnp.sum(lp[0:1, :] * lp[1:2, :], axis=-1, keepdims=True))
               - jnp.exp(jnp.sum(lp[2:3, :] * lp[3:4, :], axis=-1, keepdims=True)) + lam_init)
        ot = (acc_ref[0, :dv, :] * (1.0 / acc_ref[0, dv:dv + 1, :])
              - lam * (acc_ref[1, :dv, :] * (1.0 / acc_ref[1, dv:dv + 1, :])))
        ot = ot * lax.rsqrt(jnp.mean(ot * ot, axis=0, keepdims=True) + _EPS)
        o_ref[...] = (ot.T * g_ref[...] * (1.0 - lam_init)).astype(o_ref.dtype)

    @pl.when(one_more)
    def _():
        scores(t_rest, False, 1)
        accumulate(block_of(t_rest), 0)
        accumulate(t_rest, 1)
        finish()

    @pl.when(jnp.logical_not(one_more))
    def _():
        accumulate(block_of(t_rest), 0)
        finish()


def _diff_attention(q, k, k_head0, vt, lam_p, g_subln, batch, seq, heads, dh, lam_init):
    t = q.shape[0]
    dv = 2 * dh
    blk = min(_ATTN_BLOCK, seq)
    n_blk = seq // blk
    slopes = jnp.asarray([2.0 ** (-8.0 * (h + 1) / heads) for h in range(heads)], _F32)
    slopes = jnp.broadcast_to(slopes[:, None, None], (heads, 1, _LANES))
    return pl.pallas_call(
        functools.partial(_attn_kernel, dh=dh, lam_init=lam_init),
        grid=(batch, heads, n_blk),
        in_specs=[pl.BlockSpec((blk, dv), lambda b, h, i: (b * n_blk + i, h)),
                  pl.BlockSpec((seq, dv), lambda b, h, i: (b, k_head0 + h)),
                  pl.BlockSpec((None, dv, seq), lambda b, h, i: (b, h, 0)),
                  pl.BlockSpec((None, 1, _LANES), lambda b, h, i: (h, 0, 0)),
                  pl.BlockSpec((4, dh), lambda b, h, i: (0, 0)),
                  pl.BlockSpec((1, dv), lambda b, h, i: (0, h))],
        out_specs=pl.BlockSpec((blk, dv), lambda b, h, i: (b * n_blk + i, h)),
        out_shape=jax.ShapeDtypeStruct((t, heads * dv), _BF16),
        scratch_shapes=[pltpu.VMEM((2, blk, 2 * dh), _BF16),
                        pltpu.VMEM((blk, dh), _BF16),
                        pltpu.VMEM((2, 2, blk, blk), _F32),
                        pltpu.VMEM((2, 2, 1, blk), _F32),
                        pltpu.VMEM((2, 1, blk), _F32),
                        pltpu.VMEM((2, dv + _BF16_ROWS, blk), _F32)],
        compiler_params=_params("parallel", "parallel", "arbitrary"),
        name="diff_attention",
    )(q, k, vt, slopes, lam_p, g_subln)


def kernel(x, c, w_mod, b_mod, g_norm, w_up, w_down, w_a_in, b_a_gate, g_a_hnorm, w_a_out, g_kv, w_kv_mod,
           b_kv_mod, w_kv, w_b_q, b_lambda, g_b_subln, w_b_out, g_final):
    batch, seq, d = x.shape
    depth = w_mod.shape[0]
    n_a = w_a_in.shape[0]
    m_heads = b_a_gate.shape[1] // 2
    m_dv = g_a_hnorm.shape[1] // m_heads
    m_dk = (w_a_in.shape[2] - 2 * m_heads * m_dv - 2 * m_heads) // (2 * m_heads)
    d_dh = b_lambda.shape[2]
    d_heads = w_b_q.shape[2] // (2 * d_dh)
    t = batch * seq

    xt = x.reshape(t, d)
    c_rows = jnp.zeros((_SUBLANES, d), _F32).at[:batch].set(c)

    mod = _modvec(c_rows, w_mod, b_mod.reshape(depth, 1, 6 * d))[:, :batch]
    kv_mod = _modvec(c_rows, w_kv_mod[None], b_kv_mod.reshape(1, 1, 2 * d))[0, :batch]

    def vec(v):
        return v.reshape(batch, 1, d)

    qk_w, v_w = m_heads * m_dk, m_heads * m_dv
    main_w = 2 * qk_w + 2 * v_w
    w_up_b, w_down_b = w_up.astype(_BF16), w_down.astype(_BF16)
    w_a_in_b, w_a_out_b = w_a_in[:, :, :main_w].astype(_BF16), w_a_out.astype(_BF16)
    w_b_q_b, w_b_out_b = w_b_q.astype(_BF16), w_b_out.astype(_BF16)

    ones_d = jnp.ones((1, d), _F32)
    for l in range(depth):
        sh1, sc1, gt1, sh2, sc2, gt2 = (vec(m) for m in jnp.split(mod[l], 6, axis=-1))
        if l < n_a:
            col_scale = jnp.concatenate([jnp.ones((qk_w,), _F32), jnp.full((qk_w,), m_dk ** -0.5, _F32),
                                         jnp.ones((2 * v_w,), _F32)]).reshape(1, main_w)
            w_gate = jnp.zeros((d, _LANES), _BF16).at[:, :2 * m_heads].set(w_a_in[l, :, main_w:].astype(_BF16))
            proj, gates = _norm_mod_matmul(xt, g_norm[l, 0][None], sc1, sh1, w_a_in_b, l, col_scale, seq,
                                           w_narrow=w_gate)
            b_gate = jnp.zeros((1, _LANES), _F32).at[0, :2 * m_heads].set(b_a_gate[l])
            mix_in = _mlstm(proj, gates, b_gate, g_a_hnorm[l][None], batch, seq, m_heads, m_dk, m_dv)
            xt = _matmul_residual(mix_in, w_a_out_b, l, xt, gt1, seq)
        else:
            j = l - n_a
            lam_init = 0.8 - 0.6 * math.exp(-0.3 * l)
            q_scale = d_dh ** -0.5 * _LOG2E
            k_w = w_b_q.shape[2]
            if l == n_a:
                kv_shift, kv_scale = (vec(m) for m in jnp.split(kv_mod, 2, axis=-1))
                w_qk = jnp.concatenate([w_b_q[j], w_kv[:, :k_w]], axis=1).astype(_BF16)
                q, vt_sh = _qkv_proj(xt, g_norm[l, 0][None], sc1, sh1, g_kv[None], kv_scale, kv_shift,
                                     w_qk, k_w, w_kv[:, k_w:].T.astype(_BF16), q_scale, batch, seq)
                k_sh = q
            else:
                q = _norm_mod_matmul(xt, g_norm[l, 0][None], sc1, sh1, w_b_q_b, j,
                                     jnp.full((1, k_w), q_scale, _F32), seq)
            mix_in = _diff_attention(q, k_sh, d_heads, vt_sh, b_lambda[j], g_b_subln[j][None], batch, seq,
                                     d_heads, d_dh, lam_init)
            xt = _matmul_residual(mix_in, w_b_out_b, j, xt, gt1, seq)
        last = l == depth - 1
        xt = _mlp(xt, g_norm[l, 1][None], sc2, sh2, gt2, w_up_b, w_down_b, l,
                  g_final[None] if last else ones_d, seq, last)
    return xt.reshape(batch, seq, d)
```

```python
import functools
import math

import jax
import jax.numpy as jnp
from jax import lax
from jax.experimental import pallas as pl
from jax.experimental.pallas import tpu as pltpu

_EPS = 1e-6
_F32 = jnp.float32
_BF16 = jnp.bfloat16

_LANES = 128
_SUBLANES = 8
_BF16_ROWS = 16
_VMEM_LIMIT_BYTES = 59 * 1024 * 1024

_MODVEC_TN = 1024
_PROJ_TM = 1024
_IN_PROJ_TN = 2048
_OUT_PROJ_TM = 512
_OUT_PROJ_TN = 2048
_MLP_TM = 1024
_MLP_TF = 1024
_NORM_ROWS = 16
_NORM_UNROLL = 8
_MLSTM_CHUNK = 128
_MLSTM_CHUNKS_PER_STEP = 2
_ATTN_BLOCK = 512
_QKV_TN = 1024

_LOG2E = math.log2(math.e)

_NT_DIMS = (((1,), (1,)), ((), ()))
_TN_DIMS = (((0,), (0,)), ((), ()))


def _params(*semantics):
    return pltpu.CompilerParams(dimension_semantics=semantics, vmem_limit_bytes=_VMEM_LIMIT_BYTES)


def _modvec_kernel(c_ref, w_ref, b_ref, o_ref):
    o_ref[...] = jnp.dot(c_ref[...].astype(_BF16), w_ref[...].astype(_BF16),
                         preferred_element_type=_F32) + b_ref[...]


def _modvec(c_rows, w, b):
    n_l, d, n = w.shape
    tn = _MODVEC_TN
    return pl.pallas_call(
        _modvec_kernel,
        grid=(n_l, n // tn),
        in_specs=[pl.BlockSpec((_SUBLANES, d), lambda l, j: (0, 0)),
                  pl.BlockSpec((None, d, tn), lambda l, j: (l, 0, j)),
                  pl.BlockSpec((None, 1, tn), lambda l, j: (l, 0, j))],
        out_specs=pl.BlockSpec((None, _SUBLANES, tn), lambda l, j: (l, 0, j)),
        out_shape=jax.ShapeDtypeStruct((n_l, _SUBLANES, n), _F32),
        compiler_params=_params("parallel", "parallel"),
        name="modvec",
    )(c_rows, w, b)


def _norm_mod_tile(x_ref, gain, shift, h_ref):
    n_chunks = x_ref.shape[0] // _NORM_ROWS

    def body(r, carry):
        rows = pl.ds(pl.multiple_of(r * _NORM_ROWS, _NORM_ROWS), _NORM_ROWS)
        x = x_ref[rows, :]
        inv = lax.rsqrt(jnp.mean(x * x, axis=-1, keepdims=True) + _EPS)
        h_ref[rows, :] = (x * inv * gain + shift).astype(h_ref.dtype)
        return carry

    lax.fori_loop(0, n_chunks, body, 0, unroll=True)


def _norm_mod_matmul_kernel(x_ref, g_ref, sc_ref, sh_ref, w_ref, cs_ref, *rest, narrow):
    if narrow:
        wn_ref, o_ref, on_ref, h_ref = rest
    else:
        o_ref, h_ref = rest

    @pl.when(pl.program_id(1) == 0)
    def _():
        _norm_mod_tile(x_ref, g_ref[...] * (1.0 + sc_ref[...]), sh_ref[...], h_ref)
        if narrow:
            on_ref[...] = jnp.dot(h_ref[...], wn_ref[...], preferred_element_type=_F32)

    acc = jnp.dot(h_ref[...], w_ref[...], preferred_element_type=_F32)
    o_ref[...] = (acc * cs_ref[...]).astype(o_ref.dtype)


def _norm_mod_matmul(x, g, scale, shift, w, layer, col_scale, seq, w_narrow=None):
    t, d = x.shape
    n = col_scale.shape[1]
    tm = min(_PROJ_TM, seq)
    tn = min(_IN_PROJ_TN, n)
    tiles_per_seq = seq // tm
    narrow = w_narrow is not None
    batch_vec = pl.BlockSpec((None, 1, d), lambda i, j: (i // tiles_per_seq, 0, 0))
    in_specs = [pl.BlockSpec((tm, d), lambda i, j: (i, 0)),
                pl.BlockSpec((1, d), lambda i, j: (0, 0)),
                batch_vec, batch_vec,
                pl.BlockSpec((None, d, tn), lambda i, j: (layer, 0, j)),
                pl.BlockSpec((1, tn), lambda i, j: (0, j))]
    out_specs = [pl.BlockSpec((tm, tn), lambda i, j: (i, j))]
    out_shape = [jax.ShapeDtypeStruct((t, n), _BF16)]
    operands = [x, g, scale, shift, w, col_scale]
    if narrow:
        in_specs.append(pl.BlockSpec((d, _LANES), lambda i, j: (0, 0)))
        out_specs.append(pl.BlockSpec((tm, _LANES), lambda i, j: (i, 0)))
        out_shape.append(jax.ShapeDtypeStruct((t, _LANES), _F32))
        operands.append(w_narrow)
    outs = pl.pallas_call(
        functools.partial(_norm_mod_matmul_kernel, narrow=narrow),
        grid=(t // tm, n // tn),
        in_specs=in_specs,
        out_specs=out_specs,
        out_shape=out_shape,
        scratch_shapes=[pltpu.VMEM((tm, d), _BF16)],
        compiler_params=_params("parallel", "arbitrary"),
        name="norm_mod_matmul",
    )(*operands)
    return outs if narrow else outs[0]


def _matmul_residual_kernel(a_ref, w_ref, x_ref, gate_ref, o_ref):
    acc = jnp.dot(a_ref[...], w_ref[...], preferred_element_type=_F32)
    o_ref[...] = x_ref[...] + gate_ref[...] * acc


def _matmul_residual(a, w, layer, x, gate, seq):
    t, k = a.shape
    d = w.shape[2]
    tm = min(_OUT_PROJ_TM, seq)
    tn = min(_OUT_PROJ_TN, d)
    tiles_per_seq = seq // tm
    return pl.pallas_call(
        _matmul_residual_kernel,
        grid=(t // tm, d // tn),
        in_specs=[pl.BlockSpec((tm, k), lambda i, j: (i, 0)),
                  pl.BlockSpec((None, k, tn), lambda i, j: (layer, 0, j)),
                  pl.BlockSpec((tm, tn), lambda i, j: (i, j)),
                  pl.BlockSpec((None, 1, tn), lambda i, j: (i // tiles_per_seq, 0, j))],
        out_specs=pl.BlockSpec((tm, tn), lambda i, j: (i, j)),
        out_shape=jax.ShapeDtypeStruct((t, d), _F32),
        compiler_params=_params("parallel", "parallel"),
        name="matmul_residual",
    )(a, w, x, gate)


def _mlp_kernel(x_ref, g_ref, sc_ref, sh_ref, gate_ref, wu_ref, wd_ref, gf_ref, o_ref, h_ref, *, final_norm):
    f = pl.program_id(1)

    @pl.when(f == 0)
    def _():
        _norm_mod_tile(x_ref, g_ref[...] * (1.0 + sc_ref[...]), sh_ref[...], h_ref)
        o_ref[...] = jnp.zeros_like(o_ref)

    u = jnp.dot(h_ref[...], wu_ref[...], preferred_element_type=_F32)
    u = jnp.square(jnp.maximum(u, 0.0)).astype(_BF16)
    o_ref[...] += jnp.dot(u, wd_ref[...], preferred_element_type=_F32)

    @pl.when(f == pl.num_programs(1) - 1)
    def _():
        gate = gate_ref[...]
        gain = gf_ref[...]
        slab = _NORM_ROWS * _NORM_UNROLL

        def residual(rows):
            return x_ref[rows, :] + gate * o_ref[rows, :]

        def body(r, carry):
            base = pl.multiple_of(r * slab, slab)
            chunks = [pl.ds(base + c * _NORM_ROWS, _NORM_ROWS) for c in range(_NORM_UNROLL)]
            if final_norm:
                inv = [lax.rsqrt(jnp.mean(jnp.square(residual(rows)), axis=-1, keepdims=True) + _EPS)
                       for rows in chunks]
                for rows, scale in zip(chunks, inv):
                    o_ref[rows, :] = residual(rows) * scale * gain
            else:
                for rows in chunks:
                    o_ref[rows, :] = residual(rows)
            return carry

        lax.fori_loop(0, x_ref.shape[0] // slab, body, 0, unroll=True)


def _mlp(x, g, scale, shift, gate, w_up, w_down, layer, g_final, seq, final_norm):
    t, d = x.shape
    d_ff = w_up.shape[2]
    tm = min(_MLP_TM, seq)
    tf = min(_MLP_TF, d_ff)
    tiles_per_seq = seq // tm
    batch_vec = pl.BlockSpec((None, 1, d), lambda i, f: (i // tiles_per_seq, 0, 0))
    return pl.pallas_call(
        functools.partial(_mlp_kernel, final_norm=final_norm),
        grid=(t // tm, d_ff // tf),
        in_specs=[pl.BlockSpec((tm, d), lambda i, f: (i, 0)),
                  pl.BlockSpec((1, d), lambda i, f: (0, 0)),
                  batch_vec, batch_vec, batch_vec,
                  pl.BlockSpec((None, d, tf), lambda i, f: (layer, 0, f)),
                  pl.BlockSpec((None, tf, d), lambda i, f: (layer, f, 0)),
                  pl.BlockSpec((1, d), lambda i, f: (0, 0))],
        out_specs=pl.BlockSpec((tm, d), lambda i, f: (i, 0)),
        out_shape=jax.ShapeDtypeStruct((t, d), _F32),
        scratch_shapes=[pltpu.VMEM((tm, d), _BF16)],
        compiler_params=_params("parallel", "arbitrary"),
        name="mlp",
    )(x, g, scale, shift, gate, w_up, w_down, g_final)


def _log_sigmoid(x):
    return jnp.minimum(x, 0.0) - jnp.log1p(jnp.exp(-jnp.abs(x)))


def _mlstm_kernel(q_ref, k_ref, v_ref, og_ref, gt_ref, bg_ref, gh_ref, out_ref, c_ref, m_ref,
                  *, heads, dk, dv):
    @pl.when(pl.program_id(1) == 0)
    def _():
        c_ref[...] = jnp.zeros_like(c_ref)
        m_ref[...] = jnp.zeros_like(m_ref)

    chunk = _MLSTM_CHUNK
    src = lax.broadcasted_iota(jnp.int32, (chunk, chunk), 0)
    dst = lax.broadcasted_iota(jnp.int32, (chunk, chunk), 1)
    visible = src <= dst
    for sub in range(q_ref.shape[0] // chunk):
        rows = slice(sub * chunk, (sub + 1) * chunk)
        gates = gt_ref[rows, :] + bg_ref[...]
        b_all = jnp.dot((dst <= src).astype(_F32), _log_sigmoid(gates), preferred_element_type=_F32,
                        precision=lax.Precision.HIGHEST)
        _mlstm_chunk(q_ref, k_ref, v_ref, og_ref, gh_ref, out_ref, c_ref, m_ref, rows, gates, b_all, visible,
                     heads, dk, dv)


def _mlstm_chunk(q_ref, k_ref, v_ref, og_ref, gh_ref, out_ref, c_ref, m_ref, rows, gates, b_all, visible,
                 heads, dk, dv):
    chunk = _MLSTM_CHUNK
    gates_t = gates.T
    b_all_t = b_all.T
    for h in range(heads):
        b_row = b_all_t[heads + h:heads + h + 1, :]
        src_col = gates[:, h:h + 1] - b_all[:, heads + h:heads + h + 1]
        b_last = b_row[:, chunk - 1:chunk]
        m_prev = m_ref[h]
        c_aug = c_ref[h]
        qh = q_ref[rows, h * dk:(h + 1) * dk]
        kh = k_ref[rows, h * dk:(h + 1) * dk]
        vh = v_ref[rows, h * dv:(h + 1) * dv]

        dmat_t = jnp.where(visible, b_row + src_col, -jnp.inf)
        inter = b_row + m_prev
        m_j = jnp.maximum(inter, jnp.max(dmat_t, axis=0, keepdims=True))
        w_inter = jnp.exp(inter - m_j)
        s_t = lax.dot_general(kh, qh, _NT_DIMS, preferred_element_type=_F32) * jnp.exp(dmat_t - m_j)
        state_q = lax.dot_general(c_aug.astype(_BF16), qh, _NT_DIMS, preferred_element_type=_F32)
        num_t = (w_inter * state_q[:dv, :]
                 + lax.dot_general(vh, s_t.astype(_BF16), _TN_DIMS, preferred_element_type=_F32))
        den = w_inter * state_q[dv:dv + 1, :] + jnp.sum(s_t, axis=0, keepdims=True)
        h_t = num_t * (1.0 / jnp.maximum(jnp.abs(den), jnp.exp(-m_j)))

        dec = b_last + src_col
        m_new = jnp.maximum(b_last + m_prev, jnp.max(dec, axis=0, keepdims=True))
        kw = kh.astype(_F32) * jnp.exp(dec - m_new)
        carry_scale = jnp.exp(b_last + m_prev - m_new)
        c_ref[h, :dv, :] = carry_scale * c_aug[:dv, :] + lax.dot_general(
            vh, kw.astype(_BF16), _TN_DIMS, preferred_element_type=_F32)
        c_ref[h, dv:dv + 1, :] = carry_scale * c_aug[dv:dv + 1, :] + jnp.sum(kw, axis=0, keepdims=True)
        m_ref[h] = m_new

        y_t = h_t * lax.rsqrt(jnp.mean(h_t * h_t, axis=0, keepdims=True) + _EPS)
        og = og_ref[rows, h * dv:(h + 1) * dv].astype(_F32)
        out_ref[rows, h * dv:(h + 1) * dv] = (y_t.T * gh_ref[:, h * dv:(h + 1) * dv]
                                              * jax.nn.sigmoid(og)).astype(out_ref.dtype)


def _mlstm(proj, gates, b_gate, g_hnorm, batch, seq, heads, dk, dv):
    t = proj.shape[0]
    chunk = _MLSTM_CHUNK * _MLSTM_CHUNKS_PER_STEP
    n_chunks = seq // chunk
    qk_w, v_w = heads * dk, heads * dv
    assert v_w == 2 * qk_w

    def rows(b, c):
        return b * n_chunks + c

    return pl.pallas_call(
        functools.partial(_mlstm_kernel, heads=heads, dk=dk, dv=dv),
        grid=(batch, n_chunks),
        in_specs=[pl.BlockSpec((chunk, qk_w), lambda b, c: (rows(b, c), 0)),
                  pl.BlockSpec((chunk, qk_w), lambda b, c: (rows(b, c), 1)),
                  pl.BlockSpec((chunk, v_w), lambda b, c: (rows(b, c), 1)),
                  pl.BlockSpec((chunk, v_w), lambda b, c: (rows(b, c), 2)),
                  pl.BlockSpec((chunk, _LANES), lambda b, c: (rows(b, c), 0)),
                  pl.BlockSpec((1, _LANES), lambda b, c: (0, 0)),
                  pl.BlockSpec((1, v_w), lambda b, c: (0, 0))],
        out_specs=pl.BlockSpec((chunk, v_w), lambda b, c: (rows(b, c), 0)),
        out_shape=jax.ShapeDtypeStruct((t, v_w), _BF16),
        scratch_shapes=[pltpu.VMEM((heads, dv + _BF16_ROWS, dk), _F32),
                        pltpu.VMEM((heads, 1, 1), _F32)],
        compiler_params=_params("parallel", "arbitrary"),
        name="mlstm",
    )(proj, proj, proj, proj, gates, b_gate, g_hnorm)


def _qkv_proj_kernel(x_ref, g1_ref, sc1_ref, sh1_ref, gkv_ref, sckv_ref, shkv_ref, wqk_ref, wvt_ref,
                     qk_ref, vt_ref, h1_ref, hk_ref, *, n_q, n_k, q_scale):
    j = pl.program_id(1)

    @pl.when(j == 0)
    def _():
        gain1 = g1_ref[...] * (1.0 + sc1_ref[...])
        shift1 = sh1_ref[...]
        gain_kv = gkv_ref[...] * (1.0 + sckv_ref[...])
        shift_kv = shkv_ref[...]
        n_chunks = x_ref.shape[0] // _NORM_ROWS

        def body(r, carry):
            rows = pl.ds(pl.multiple_of(r * _NORM_ROWS, _NORM_ROWS), _NORM_ROWS)
            x = x_ref[rows, :]
            xn = x * lax.rsqrt(jnp.mean(x * x, axis=-1, keepdims=True) + _EPS)
            h1_ref[rows, :] = (xn * gain1 + shift1).astype(h1_ref.dtype)
            hk_ref[rows, :] = (xn * gain_kv + shift_kv).astype(hk_ref.dtype)
            return carry

        lax.fori_loop(0, n_chunks, body, 0, unroll=True)

    @pl.when(j < n_q)
    def _():
        acc = jnp.dot(h1_ref[...], wqk_ref[...], preferred_element_type=_F32)
        qk_ref[...] = (acc * q_scale).astype(qk_ref.dtype)

    @pl.when(jnp.logical_and(j >= n_q, j < n_q + n_k))
    def _():
        qk_ref[...] = jnp.dot(hk_ref[...], wqk_ref[...], preferred_element_type=_F32).astype(qk_ref.dtype)

    @pl.when(j >= n_q + n_k)
    def _():
        vt_ref[...] = lax.dot_general(wvt_ref[...], hk_ref[...], _NT_DIMS,
                                      preferred_element_type=_F32).astype(vt_ref.dtype)


def _qkv_proj(x, g1, sc1, sh1, gkv, sckv, shkv, wqk, n_qc, wvt, q_scale, batch, seq):
    t, d = x.shape
    n_qkc, n_vc = wqk.shape[1], wvt.shape[0]
    tm = min(_PROJ_TM, seq)
    tn = _QKV_TN
    tiles_per_seq = seq // tm
    n_q, n_qk, n_v = n_qc // tn, n_qkc // tn, n_vc // tn

    def qk_col(j):
        return jnp.minimum(j, n_qk - 1)

    def vt_row(j):
        return jnp.maximum(j - n_qk, 0)

    batch_vec = pl.BlockSpec((None, 1, d), lambda i, j: (i // tiles_per_seq, 0, 0))
    row_vec = pl.BlockSpec((1, d), lambda i, j: (0, 0))
    return pl.pallas_call(
        functools.partial(_qkv_proj_kernel, n_q=n_q, n_k=n_qk - n_q, q_scale=q_scale),
        grid=(t // tm, n_qk + n_v),
        in_specs=[pl.BlockSpec((tm, d), lambda i, j: (i, 0)),
                  row_vec, batch_vec, batch_vec, row_vec, batch_vec, batch_vec,
                  pl.BlockSpec((d, tn), lambda i, j: (0, qk_col(j))),
                  pl.BlockSpec((tn, d), lambda i, j: (vt_row(j), 0))],
        out_specs=[pl.BlockSpec((tm, tn), lambda i, j: (i, qk_col(j))),
                   pl.BlockSpec((None, tn, tm), lambda i, j: (i // tiles_per_seq, vt_row(j), i % tiles_per_seq))],
        out_shape=[jax.ShapeDtypeStruct((t, n_qkc), _BF16),
                   jax.ShapeDtypeStruct((batch, n_vc, seq), _BF16)],
        scratch_shapes=[pltpu.VMEM((tm, d), _BF16), pltpu.VMEM((tm, d), _BF16)],
        compiler_params=_params("parallel", "arbitrary"),
        name="qkv_proj",
    )(x, g1, sc1, sh1, gkv, sckv, shkv, wqk, wvt)


def _attn_kernel(q_ref, k_ref, vt_ref, slope_ref, lam_ref, g_ref, o_ref, qa_ref, kpos_ref, s_ref, bmax_ref, m_ref,
                 acc_ref, *, dh, lam_init):
    qi = pl.program_id(2)
    blk = q_ref.shape[0]
    slope2 = slope_ref[:, 0:1] * _LOG2E

    dv = vt_ref.shape[0]

    @pl.when(qi == 0)
    def _():
        lane = lax.broadcasted_iota(jnp.int32, (blk, dh), 1)
        key_bias = lax.broadcasted_iota(jnp.int32, (blk, dh), 0).astype(_F32) * slope2
        part_hi = key_bias.astype(_BF16).astype(_F32)
        part_mid = (key_bias - part_hi).astype(_BF16).astype(_F32)
        part_lo = key_bias - part_hi - part_mid
        kpos_ref[...] = jnp.where(lane == 0, part_hi, jnp.where(lane == 1, part_mid,
                                                                jnp.where(lane == 2, part_lo, 0.0))).astype(_BF16)
        for s_idx in range(2):
            qa_ref[s_idx, :, dh:] = jnp.where(lane < 3, 1.0, 0.0).astype(_BF16)

    for s_idx in range(2):
        qa_ref[s_idx, :, :dh] = q_ref[:, s_idx * dh:(s_idx + 1) * dh]

    acc_ref[...] = jnp.zeros_like(acc_ref)
    m_ref[...] = jnp.full_like(m_ref, -jnp.inf)

    def scores(kj, masked, buf):
        keys = pl.ds(pl.multiple_of(kj * blk, blk), blk)
        for s_idx in range(2):
            k_cat = jnp.concatenate([k_ref[keys, s_idx * dh:(s_idx + 1) * dh], kpos_ref[...]], axis=1)
            st = lax.dot_general(k_cat, qa_ref[s_idx], _NT_DIMS, preferred_element_type=_F32)
            if masked:
                key = lax.broadcasted_iota(jnp.int32, (blk, blk), 0)
                qry = lax.broadcasted_iota(jnp.int32, (blk, blk), 1)
                st = jnp.where(key <= qry, st, -jnp.inf)
            s_ref[buf, s_idx] = st
            bmax_ref[buf, s_idx] = jnp.max(st, axis=0, keepdims=True)

    def accumulate(kj, buf):
        keys = pl.ds(pl.multiple_of(kj * blk, blk), blk)
        vt_aug = jnp.concatenate([vt_ref[:, keys], jnp.ones((_BF16_ROWS, blk), _BF16)], axis=0)
        block_bias = slope2 * jnp.full((1, 1), (kj - qi) * blk, jnp.int32).astype(_F32)
        for s_idx in range(2):
            m_old = m_ref[s_idx]
            m_new = jnp.maximum(m_old, bmax_ref[buf, s_idx] + block_bias)
            pt = jnp.exp2(s_ref[buf, s_idx] - (m_new - block_bias))
            acc_ref[s_idx] = acc_ref[s_idx] * jnp.exp2(m_old - m_new) + jnp.dot(
                vt_aug, pt.astype(_BF16), preferred_element_type=_F32)
            m_ref[s_idx] = m_new

    def block_of(t):
        return jnp.where(t == 0, qi, t - 1)

    def pair(u):
        t = 2 * u
        scores(t, False, 1)
        accumulate(block_of(t), 0)
        scores(t + 1, False, 0)
        accumulate(t, 1)

    def four_pairs(u, carry):
        for p in range(4):
            pair(4 * u + p)
        return carry

    scores(qi, True, 0)
    n_pairs = lax.shift_right_logical(qi, 1)
    n_fours = lax.shift_right_logical(n_pairs, 2)
    lax.fori_loop(0, n_fours, four_pairs, 0)

    @pl.when((n_pairs & 2) == 2)
    def _():
        pair(4 * n_fours)
        pair(4 * n_fours + 1)

    @pl.when((n_pairs & 1) == 1)
    def _():
        pair(n_pairs - 1)

    t_rest = 2 * n_pairs
    one_more = (qi & 1) == 1

    def finish():
        lp = lam_ref[...]
        lam = (jnp.exp(jnp.sum(lp[0:1, :] * lp[1:2, :], axis=-1, keepdims=True))
               - jnp.exp(jnp.sum(lp[2:3, :] * lp[3:4, :], axis=-1, keepdims=True)) + lam_init)
        ot = (acc_ref[0, :dv, :] * (1.0 / acc_ref[0, dv:dv + 1, :])
              - lam * (acc_ref[1, :dv, :] * (1.0 / acc_ref[1, dv:dv + 1, :])))
        ot = ot * lax.rsqrt(jnp.mean(ot * ot, axis=0, keepdims=True) + _EPS)
        o_ref[...] = (ot.T * g_ref[...] * (1.0 - lam_init)).astype(o_ref.dtype)

    @pl.when(one_more)
    def _():
        scores(t_rest, False, 1)
        accumulate(block_of(t_rest), 0)
        accumulate(t_rest, 1)
        finish()

    @pl.when(jnp.logical_not(one_more))
    def _():
        accumulate(block_of(t_rest), 0)
        finish()


def _diff_attention(q, k, k_head0, vt, lam_p, g_subln, batch, seq, heads, dh, lam_init):
    t = q.shape[0]
    dv = 2 * dh
    blk = min(_ATTN_BLOCK, seq)
    n_blk = seq // blk
    slopes = jnp.asarray([2.0 ** (-8.0 * (h + 1) / heads) for h in range(heads)], _F32)
    slopes = jnp.broadcast_to(slopes[:, None, None], (heads, 1, _LANES))
    return pl.pallas_call(
        functools.partial(_attn_kernel, dh=dh, lam_init=lam_init),
        grid=(batch, heads, n_blk),
        in_specs=[pl.BlockSpec((blk, dv), lambda b, h, i: (b * n_blk + i, h)),
                  pl.BlockSpec((seq, dv), lambda b, h, i: (b, k_head0 + h)),
                  pl.BlockSpec((None, dv, seq), lambda b, h, i: (b, h, 0)),
                  pl.BlockSpec((None, 1, _LANES), lambda b, h, i: (h, 0, 0)),
                  pl.BlockSpec((4, dh), lambda b, h, i: (0, 0)),
                  pl.BlockSpec((1, dv), lambda b, h, i: (0, h))],
        out_specs=pl.BlockSpec((blk, dv), lambda b, h, i: (b * n_blk + i, h)),
        out_shape=jax.ShapeDtypeStruct((t, heads * dv), _BF16),
        scratch_shapes=[pltpu.VMEM((2, blk, 2 * dh), _BF16),
                        pltpu.VMEM((blk, dh), _BF16),
                        pltpu.VMEM((2, 2, blk, blk), _F32),
                        pltpu.VMEM((2, 2, 1, blk), _F32),
                        pltpu.VMEM((2, 1, blk), _F32),
                        pltpu.VMEM((2, dv + _BF16_ROWS, blk), _F32)],
        compiler_params=_params("parallel", "parallel", "arbitrary"),
        name="diff_attention",
    )(q, k, vt, slopes, lam_p, g_subln)


def kernel(x, c, w_mod, b_mod, g_norm, w_up, w_down, w_a_in, b_a_gate, g_a_hnorm, w_a_out, g_kv, w_kv_mod,
           b_kv_mod, w_kv, w_b_q, b_lambda, g_b_subln, w_b_out, g_final):
    batch, seq, d = x.shape
    depth = w_mod.shape[0]
    n_a = w_a_in.shape[0]
    m_heads = b_a_gate.shape[1] // 2
    m_dv = g_a_hnorm.shape[1] // m_heads
    m_dk = (w_a_in.shape[2] - 2 * m_heads * m_dv - 2 * m_heads) // (2 * m_heads)
    d_dh = b_lambda.shape[2]
    d_heads = w_b_q.shape[2] // (2 * d_dh)
    t = batch * seq

    xt = x.reshape(t, d)
    c_rows = jnp.zeros((_SUBLANES, d), _F32).at[:batch].set(c)

    mod = _modvec(c_rows, w_mod, b_mod.reshape(depth, 1, 6 * d))[:, :batch]
    kv_mod = _modvec(c_rows, w_kv_mod[None], b_kv_mod.reshape(1, 1, 2 * d))[0, :batch]

    def vec(v):
        return v.reshape(batch, 1, d)

    qk_w, v_w = m_heads * m_dk, m_heads * m_dv
    main_w = 2 * qk_w + 2 * v_w
    w_up_b, w_down_b = w_up.astype(_BF16), w_down.astype(_BF16)
    w_a_in_b, w_a_out_b = w_a_in.astype(_BF16), w_a_out.astype(_BF16)
    w_b_q_b, w_b_out_b = w_b_q.astype(_BF16), w_b_out.astype(_BF16)

    ones_d = jnp.ones((1, d), _F32)
    for l in range(depth):
        sh1, sc1, gt1, sh2, sc2, gt2 = (vec(m) for m in jnp.split(mod[l], 6, axis=-1))
        if l < n_a:
            col_scale = jnp.concatenate([jnp.ones((qk_w,), _F32), jnp.full((qk_w,), m_dk ** -0.5, _F32),
                                         jnp.ones((2 * v_w,), _F32)]).reshape(1, main_w)
            w_gate = jnp.zeros((d, _LANES), _BF16).at[:, :2 * m_heads].set(w_a_in_b[l, :, main_w:])
            proj, gates = _norm_mod_matmul(xt, g_norm[l, 0][None], sc1, sh1, w_a_in_b, l, col_scale, seq,
                                           w_narrow=w_gate)
            b_gate = jnp.zeros((1, _LANES), _F32).at[0, :2 * m_heads].set(b_a_gate[l])
            mix_in = _mlstm(proj, gates, b_gate, g_a_hnorm[l][None], batch, seq, m_heads, m_dk, m_dv)
            xt = _matmul_residual(mix_in, w_a_out_b, l, xt, gt1, seq)
        else:
            j = l - n_a
            lam_init = 0.8 - 0.6 * math.exp(-0.3 * l)
            q_scale = d_dh ** -0.5 * _LOG2E
            k_w = w_b_q.shape[2]
            if l == n_a:
                kv_shift, kv_scale = (vec(m) for m in jnp.split(kv_mod, 2, axis=-1))
                w_qk = jnp.concatenate([w_b_q[j], w_kv[:, :k_w]], axis=1).astype(_BF16)
                q, vt_sh = _qkv_proj(xt, g_norm[l, 0][None], sc1, sh1, g_kv[None], kv_scale, kv_shift,
                                     w_qk, k_w, w_kv[:, k_w:].T.astype(_BF16), q_scale, batch, seq)
                k_sh = q
            else:
                q = _norm_mod_matmul(xt, g_norm[l, 0][None], sc1, sh1, w_b_q_b, j,
                                     jnp.full((1, k_w), q_scale, _F32), seq)
            mix_in = _diff_attention(q, k_sh, d_heads, vt_sh, b_lambda[j], g_b_subln[j][None], batch, seq,
                                     d_heads, d_dh, lam_init)
            xt = _matmul_residual(mix_in, w_b_out_b, j, xt, gt1, seq)
        last = l == depth - 1
        xt = _mlp(xt, g_norm[l, 1][None], sc2, sh2, gt2, w_up_b, w_down_b, l,
                  g_final[None] if last else ones_d, seq, last)
    return xt.reshape(batch, seq, d)
```

```python
import functools
import math

import jax
import jax.numpy as jnp
from jax import lax
from jax.experimental import pallas as pl
from jax.experimental.pallas import tpu as pltpu

_EPS = 1e-6
_F32 = jnp.float32
_BF16 = jnp.bfloat16

_LANES = 128
_SUBLANES = 8
_BF16_ROWS = 16
_VMEM_LIMIT_BYTES = 59 * 1024 * 1024

_MODVEC_TN = 1024
_PROJ_TM = 1024
_IN_PROJ_TN = 2048
_OUT_PROJ_TM = 512
_OUT_PROJ_TN = 2048
_MLP_TM = 1024
_MLP_TF = 1024
_NORM_ROWS = 16
_NORM_UNROLL = 8
_MLSTM_CHUNK = 128
_MLSTM_CHUNKS_PER_STEP = 1
_ATTN_BLOCK = 512
_QKV_TN = 1024

_LOG2E = math.log2(math.e)

_NT_DIMS = (((1,), (1,)), ((), ()))
_TN_DIMS = (((0,), (0,)), ((), ()))


def _params(*semantics):
    return pltpu.CompilerParams(dimension_semantics=semantics, vmem_limit_bytes=_VMEM_LIMIT_BYTES)


def _modvec_kernel(c_ref, w_ref, b_ref, o_ref):
    o_ref[...] = jnp.dot(c_ref[...].astype(_BF16), w_ref[...].astype(_BF16),
                         preferred_element_type=_F32) + b_ref[...]


def _modvec(c_rows, w, b):
    n_l, d, n = w.shape
    tn = _MODVEC_TN
    return pl.pallas_call(
        _modvec_kernel,
        grid=(n_l, n // tn),
        in_specs=[pl.BlockSpec((_SUBLANES, d), lambda l, j: (0, 0)),
                  pl.BlockSpec((None, d, tn), lambda l, j: (l, 0, j)),
                  pl.BlockSpec((None, 1, tn), lambda l, j: (l, 0, j))],
        out_specs=pl.BlockSpec((None, _SUBLANES, tn), lambda l, j: (l, 0, j)),
        out_shape=jax.ShapeDtypeStruct((n_l, _SUBLANES, n), _F32),
        compiler_params=_params("parallel", "parallel"),
        name="modvec",
    )(c_rows, w, b)


def _norm_mod_tile(x_ref, gain, shift, h_ref):
    n_chunks = x_ref.shape[0] // _NORM_ROWS

    def body(r, carry):
        rows = pl.ds(pl.multiple_of(r * _NORM_ROWS, _NORM_ROWS), _NORM_ROWS)
        x = x_ref[rows, :]
        inv = lax.rsqrt(jnp.mean(x * x, axis=-1, keepdims=True) + _EPS)
        h_ref[rows, :] = (x * inv * gain + shift).astype(h_ref.dtype)
        return carry

    lax.fori_loop(0, n_chunks, body, 0, unroll=True)


def _norm_mod_matmul_kernel(x_ref, g_ref, sc_ref, sh_ref, w_ref, cs_ref, *rest, narrow):
    if narrow:
        wn_ref, o_ref, on_ref, h_ref = rest
    else:
        o_ref, h_ref = rest

    @pl.when(pl.program_id(1) == 0)
    def _():
        _norm_mod_tile(x_ref, g_ref[...] * (1.0 + sc_ref[...]), sh_ref[...], h_ref)
        if narrow:
            on_ref[...] = jnp.dot(h_ref[...], wn_ref[...], preferred_element_type=_F32)

    acc = jnp.dot(h_ref[...], w_ref[...], preferred_element_type=_F32)
    o_ref[...] = (acc * cs_ref[...]).astype(o_ref.dtype)


def _norm_mod_matmul(x, g, scale, shift, w, layer, col_scale, seq, w_narrow=None):
    t, d = x.shape
    n = col_scale.shape[1]
    tm = min(_PROJ_TM, seq)
    tn = min(_IN_PROJ_TN, n)
    tiles_per_seq = seq // tm
    narrow = w_narrow is not None
    batch_vec = pl.BlockSpec((None, 1, d), lambda i, j: (i // tiles_per_seq, 0, 0))
    in_specs = [pl.BlockSpec((tm, d), lambda i, j: (i, 0)),
                pl.BlockSpec((1, d), lambda i, j: (0, 0)),
                batch_vec, batch_vec,
                pl.BlockSpec((None, d, tn), lambda i, j: (layer, 0, j)),
                pl.BlockSpec((1, tn), lambda i, j: (0, j))]
    out_specs = [pl.BlockSpec((tm, tn), lambda i, j: (i, j))]
    out_shape = [jax.ShapeDtypeStruct((t, n), _BF16)]
    operands = [x, g, scale, shift, w, col_scale]
    if narrow:
        in_specs.append(pl.BlockSpec((d, _LANES), lambda i, j: (0, 0)))
        out_specs.append(pl.BlockSpec((tm, _LANES), lambda i, j: (i, 0)))
        out_shape.append(jax.ShapeDtypeStruct((t, _LANES), _F32))
        operands.append(w_narrow)
    outs = pl.pallas_call(
        functools.partial(_norm_mod_matmul_kernel, narrow=narrow),
        grid=(t // tm, n // tn),
        in_specs=in_specs,
        out_specs=out_specs,
        out_shape=out_shape,
        scratch_shapes=[pltpu.VMEM((tm, d), _BF16)],
        compiler_params=_params("parallel", "arbitrary"),
        name="norm_mod_matmul",
    )(*operands)
    return outs if narrow else outs[0]


def _matmul_residual_kernel(a_ref, w_ref, x_ref, gate_ref, o_ref):
    acc = jnp.dot(a_ref[...], w_ref[...], preferred_element_type=_F32)
    o_ref[...] = x_ref[...] + gate_ref[...] * acc


def _matmul_residual(a, w, layer, x, gate, seq):
    t, k = a.shape
    d = w.shape[2]
    tm = min(_OUT_PROJ_TM, seq)
    tn = min(_OUT_PROJ_TN, d)
    tiles_per_seq = seq // tm
    return pl.pallas_call(
        _matmul_residual_kernel,
        grid=(t // tm, d // tn),
        in_specs=[pl.BlockSpec((tm, k), lambda i, j: (i, 0)),
                  pl.BlockSpec((None, k, tn), lambda i, j: (layer, 0, j)),
                  pl.BlockSpec((tm, tn), lambda i, j: (i, j)),
                  pl.BlockSpec((None, 1, tn), lambda i, j: (i // tiles_per_seq, 0, j))],
        out_specs=pl.BlockSpec((tm, tn), lambda i, j: (i, j)),
        out_shape=jax.ShapeDtypeStruct((t, d), _F32),
        compiler_params=_params("parallel", "parallel"),
        name="matmul_residual",
    )(a, w, x, gate)


def _mlp_kernel(x_ref, g_ref, sc_ref, sh_ref, gate_ref, wu_ref, wd_ref, gf_ref, o_ref, h_ref, *, final_norm):
    f = pl.program_id(1)

    @pl.when(f == 0)
    def _():
        _norm_mod_tile(x_ref, g_ref[...] * (1.0 + sc_ref[...]), sh_ref[...], h_ref)
        o_ref[...] = jnp.zeros_like(o_ref)

    u = jnp.dot(h_ref[...], wu_ref[...], preferred_element_type=_F32)
    u = jnp.square(jnp.maximum(u, 0.0)).astype(_BF16)
    o_ref[...] += jnp.dot(u, wd_ref[...], preferred_element_type=_F32)

    @pl.when(f == pl.num_programs(1) - 1)
    def _():
        gate = gate_ref[...]
        gain = gf_ref[...]
        slab = _NORM_ROWS * _NORM_UNROLL

        def residual(rows):
            return x_ref[rows, :] + gate * o_ref[rows, :]

        def body(r, carry):
            base = pl.multiple_of(r * slab, slab)
            chunks = [pl.ds(base + c * _NORM_ROWS, _NORM_ROWS) for c in range(_NORM_UNROLL)]
            if final_norm:
                inv = [lax.rsqrt(jnp.mean(jnp.square(residual(rows)), axis=-1, keepdims=True) + _EPS)
                       for rows in chunks]
                for rows, scale in zip(chunks, inv):
                    o_ref[rows, :] = residual(rows) * scale * gain
            else:
                for rows in chunks:
                    o_ref[rows, :] = residual(rows)
            return carry

        lax.fori_loop(0, x_ref.shape[0] // slab, body, 0, unroll=True)


def _mlp(x, g, scale, shift, gate, w_up, w_down, layer, g_final, seq, final_norm):
    t, d = x.shape
    d_ff = w_up.shape[2]
    tm = min(_MLP_TM, seq)
    tf = min(_MLP_TF, d_ff)
    tiles_per_seq = seq // tm
    batch_vec = pl.BlockSpec((None, 1, d), lambda i, f: (i // tiles_per_seq, 0, 0))
    return pl.pallas_call(
        functools.partial(_mlp_kernel, final_norm=final_norm),
        grid=(t // tm, d_ff // tf),
        in_specs=[pl.BlockSpec((tm, d), lambda i, f: (i, 0)),
                  pl.BlockSpec((1, d), lambda i, f: (0, 0)),
                  batch_vec, batch_vec, batch_vec,
                  pl.BlockSpec((None, d, tf), lambda i, f: (layer, 0, f)),
                  pl.BlockSpec((None, tf, d), lambda i, f: (layer, f, 0)),
                  pl.BlockSpec((1, d), lambda i, f: (0, 0))],
        out_specs=pl.BlockSpec((tm, d), lambda i, f: (i, 0)),
        out_shape=jax.ShapeDtypeStruct((t, d), _F32),
        scratch_shapes=[pltpu.VMEM((tm, d), _BF16)],
        compiler_params=_params("parallel", "arbitrary"),
        name="mlp",
    )(x, g, scale, shift, gate, w_up, w_down, g_final)


def _log_sigmoid(x):
    return jnp.minimum(x, 0.0) - jnp.log1p(jnp.exp(-jnp.abs(x)))


def _mlstm_kernel(q_ref, k_ref, v_ref, og_ref, gt_ref, bg_ref, gh_ref, out_ref, c_ref, m_ref,
                  *, heads, dk, dv):
    @pl.when(pl.program_id(0) == 0)
    def _():
        c_ref[...] = jnp.zeros_like(c_ref)
        m_ref[...] = jnp.zeros_like(m_ref)

    chunk = _MLSTM_CHUNK
    src = lax.broadcasted_iota(jnp.int32, (chunk, chunk), 0)
    dst = lax.broadcasted_iota(jnp.int32, (chunk, chunk), 1)
    visible = src <= dst
    for sub in range(q_ref.shape[1] // chunk):
        rows = slice(sub * chunk, (sub + 1) * chunk)
        for b in range(q_ref.shape[0]):
            gates = gt_ref[b, rows, :] + bg_ref[...]
            b_all = jnp.dot((dst <= src).astype(_F32), _log_sigmoid(gates), preferred_element_type=_F32,
                            precision=lax.Precision.HIGHEST)
            _mlstm_chunk(q_ref.at[b], k_ref.at[b], v_ref.at[b], og_ref.at[b], gh_ref, out_ref.at[b], c_ref.at[b],
                         m_ref.at[b], rows, gates, b_all, visible, heads, dk, dv)


def _mlstm_chunk(q_ref, k_ref, v_ref, og_ref, gh_ref, out_ref, c_ref, m_ref, rows, gates, b_all, visible,
                 heads, dk, dv):
    chunk = _MLSTM_CHUNK
    gates_t = gates.T
    b_all_t = b_all.T
    for h in range(heads):
        b_row = b_all_t[heads + h:heads + h + 1, :]
        src_col = gates[:, h:h + 1] - b_all[:, heads + h:heads + h + 1]
        b_last = b_row[:, chunk - 1:chunk]
        m_prev = m_ref[h]
        c_aug = c_ref[h]
        qh = q_ref[rows, h * dk:(h + 1) * dk]
        kh = k_ref[rows, h * dk:(h + 1) * dk]
        vh = v_ref[rows, h * dv:(h + 1) * dv]

        dmat_t = jnp.where(visible, b_row + src_col, -jnp.inf)
        inter = b_row + m_prev
        m_j = jnp.maximum(inter, jnp.max(dmat_t, axis=0, keepdims=True))
        w_inter = jnp.exp(inter - m_j)
        s_t = lax.dot_general(kh, qh, _NT_DIMS, preferred_element_type=_F32) * jnp.exp(dmat_t - m_j)
        state_q = lax.dot_general(c_aug.astype(_BF16), qh, _NT_DIMS, preferred_element_type=_F32)
        num_t = (w_inter * state_q[:dv, :]
                 + lax.dot_general(vh, s_t.astype(_BF16), _TN_DIMS, preferred_element_type=_F32))
        den = w_inter * state_q[dv:dv + 1, :] + jnp.sum(s_t, axis=0, keepdims=True)
        h_t = num_t * (1.0 / jnp.maximum(jnp.abs(den), jnp.exp(-m_j)))

        dec = b_last + src_col
        m_new = jnp.maximum(b_last + m_prev, jnp.max(dec, axis=0, keepdims=True))
        kw = kh.astype(_F32) * jnp.exp(dec - m_new)
        carry_scale = jnp.exp(b_last + m_prev - m_new)
        c_ref[h, :dv, :] = carry_scale * c_aug[:dv, :] + lax.dot_general(
            vh, kw.astype(_BF16), _TN_DIMS, preferred_element_type=_F32)
        c_ref[h, dv:dv + 1, :] = carry_scale * c_aug[dv:dv + 1, :] + jnp.sum(kw, axis=0, keepdims=True)
        m_ref[h] = m_new

        y_t = h_t * lax.rsqrt(jnp.mean(h_t * h_t, axis=0, keepdims=True) + _EPS)
        og = og_ref[rows, h * dv:(h + 1) * dv].astype(_F32)
        out_ref[rows, h * dv:(h + 1) * dv] = (y_t.T * gh_ref[:, h * dv:(h + 1) * dv]
                                              * jax.nn.sigmoid(og)).astype(out_ref.dtype)


def _mlstm(proj, gates, b_gate, g_hnorm, batch, seq, heads, dk, dv):
    t = proj.shape[0]
    chunk = _MLSTM_CHUNK * _MLSTM_CHUNKS_PER_STEP
    qk_w, v_w = heads * dk, heads * dv
    assert v_w == 2 * qk_w
    proj = proj.reshape(batch, seq, proj.shape[1])
    gates = gates.reshape(batch, seq, _LANES)

    out = pl.pallas_call(
        functools.partial(_mlstm_kernel, heads=heads, dk=dk, dv=dv),
        grid=(seq // chunk,),
        in_specs=[pl.BlockSpec((batch, chunk, qk_w), lambda c: (0, c, 0)),
                  pl.BlockSpec((batch, chunk, qk_w), lambda c: (0, c, 1)),
                  pl.BlockSpec((batch, chunk, v_w), lambda c: (0, c, 1)),
                  pl.BlockSpec((batch, chunk, v_w), lambda c: (0, c, 2)),
                  pl.BlockSpec((batch, chunk, _LANES), lambda c: (0, c, 0)),
                  pl.BlockSpec((1, _LANES), lambda c: (0, 0)),
                  pl.BlockSpec((1, v_w), lambda c: (0, 0))],
        out_specs=pl.BlockSpec((batch, chunk, v_w), lambda c: (0, c, 0)),
        out_shape=jax.ShapeDtypeStruct((batch, seq, v_w), _BF16),
        scratch_shapes=[pltpu.VMEM((batch, heads, dv + _BF16_ROWS, dk), _F32),
                        pltpu.VMEM((batch, heads, 1, 1), _F32)],
        compiler_params=_params("arbitrary"),
        name="mlstm",
    )(proj, proj, proj, proj, gates, b_gate, g_hnorm)
    return out.reshape(t, v_w)


def _qkv_proj_kernel(x_ref, g1_ref, sc1_ref, sh1_ref, gkv_ref, sckv_ref, shkv_ref, wqk_ref, wvt_ref,
                     qk_ref, vt_ref, h1_ref, hk_ref, *, n_q, n_k, q_scale):
    j = pl.program_id(1)

    @pl.when(j == 0)
    def _():
        gain1 = g1_ref[...] * (1.0 + sc1_ref[...])
        shift1 = sh1_ref[...]
        gain_kv = gkv_ref[...] * (1.0 + sckv_ref[...])
        shift_kv = shkv_ref[...]
        n_chunks = x_ref.shape[0] // _NORM_ROWS

        def body(r, carry):
            rows = pl.ds(pl.multiple_of(r * _NORM_ROWS, _NORM_ROWS), _NORM_ROWS)
            x = x_ref[rows, :]
            xn = x * lax.rsqrt(jnp.mean(x * x, axis=-1, keepdims=True) + _EPS)
            h1_ref[rows, :] = (xn * gain1 + shift1).astype(h1_ref.dtype)
            hk_ref[rows, :] = (xn * gain_kv + shift_kv).astype(hk_ref.dtype)
            return carry

        lax.fori_loop(0, n_chunks, body, 0, unroll=True)

    @pl.when(j < n_q)
    def _():
        acc = jnp.dot(h1_ref[...], wqk_ref[...], preferred_element_type=_F32)
        qk_ref[...] = (acc * q_scale).astype(qk_ref.dtype)

    @pl.when(jnp.logical_and(j >= n_q, j < n_q + n_k))
    def _():
        qk_ref[...] = jnp.dot(hk_ref[...], wqk_ref[...], preferred_element_type=_F32).astype(qk_ref.dtype)

    @pl.when(j >= n_q + n_k)
    def _():
        vt_ref[...] = lax.dot_general(wvt_ref[...], hk_ref[...], _NT_DIMS,
                                      preferred_element_type=_F32).astype(vt_ref.dtype)


def _qkv_proj(x, g1, sc1, sh1, gkv, sckv, shkv, wqk, n_qc, wvt, q_scale, batch, seq):
    t, d = x.shape
    n_qkc, n_vc = wqk.shape[1], wvt.shape[0]
    tm = min(_PROJ_TM, seq)
    tn = _QKV_TN
    tiles_per_seq = seq // tm
    n_q, n_qk, n_v = n_qc // tn, n_qkc // tn, n_vc // tn

    def qk_col(j):
        return jnp.minimum(j, n_qk - 1)

    def vt_row(j):
        return jnp.maximum(j - n_qk, 0)

    batch_vec = pl.BlockSpec((None, 1, d), lambda i, j: (i // tiles_per_seq, 0, 0))
    row_vec = pl.BlockSpec((1, d), lambda i, j: (0, 0))
    return pl.pallas_call(
        functools.partial(_qkv_proj_kernel, n_q=n_q, n_k=n_qk - n_q, q_scale=q_scale),
        grid=(t // tm, n_qk + n_v),
        in_specs=[pl.BlockSpec((tm, d), lambda i, j: (i, 0)),
                  row_vec, batch_vec, batch_vec, row_vec, batch_vec, batch_vec,
                  pl.BlockSpec((d, tn), lambda i, j: (0, qk_col(j))),
                  pl.BlockSpec((tn, d), lambda i, j: (vt_row(j), 0))],
        out_specs=[pl.BlockSpec((tm, tn), lambda i, j: (i, qk_col(j))),
                   pl.BlockSpec((None, tn, tm), lambda i, j: (i // tiles_per_seq, vt_row(j), i % tiles_per_seq))],
        out_shape=[jax.ShapeDtypeStruct((t, n_qkc), _BF16),
                   jax.ShapeDtypeStruct((batch, n_vc, seq), _BF16)],
        scratch_shapes=[pltpu.VMEM((tm, d), _BF16), pltpu.VMEM((tm, d), _BF16)],
        compiler_params=_params("parallel", "arbitrary"),
        name="qkv_proj",
    )(x, g1, sc1, sh1, gkv, sckv, shkv, wqk, wvt)


def _attn_kernel(q_ref, k_ref, vt_ref, slope_ref, lam_ref, g_ref, o_ref, qa_ref, kpos_ref, s_ref, bmax_ref, m_ref,
                 acc_ref, *, dh, lam_init):
    qi = pl.program_id(2)
    blk = q_ref.shape[0]
    slope2 = slope_ref[:, 0:1] * _LOG2E

    dv = vt_ref.shape[0]

    @pl.when(qi == 0)
    def _():
        lane = lax.broadcasted_iota(jnp.int32, (blk, dh), 1)
        key_bias = lax.broadcasted_iota(jnp.int32, (blk, dh), 0).astype(_F32) * slope2
        part_hi = key_bias.astype(_BF16).astype(_F32)
        part_mid = (key_bias - part_hi).astype(_BF16).astype(_F32)
        part_lo = key_bias - part_hi - part_mid
        kpos_ref[...] = jnp.where(lane == 0, part_hi, jnp.where(lane == 1, part_mid,
                                                                jnp.where(lane == 2, part_lo, 0.0))).astype(_BF16)
        for s_idx in range(2):
            qa_ref[s_idx, :, dh:] = jnp.where(lane < 3, 1.0, 0.0).astype(_BF16)

    for s_idx in range(2):
        qa_ref[s_idx, :, :dh] = q_ref[:, s_idx * dh:(s_idx + 1) * dh]

    acc_ref[...] = jnp.zeros_like(acc_ref)
    m_ref[...] = jnp.full_like(m_ref, -jnp.inf)

    def scores(kj, masked, buf):
        keys = pl.ds(pl.multiple_of(kj * blk, blk), blk)
        for s_idx in range(2):
            k_cat = jnp.concatenate([k_ref[keys, s_idx * dh:(s_idx + 1) * dh], kpos_ref[...]], axis=1)
            st = lax.dot_general(k_cat, qa_ref[s_idx], _NT_DIMS, preferred_element_type=_F32)
            if masked:
                key = lax.broadcasted_iota(jnp.int32, (blk, blk), 0)
                qry = lax.broadcasted_iota(jnp.int32, (blk, blk), 1)
                st = jnp.where(key <= qry, st, -jnp.inf)
            s_ref[buf, s_idx] = st
            bmax_ref[buf, s_idx] = jnp.max(st, axis=0, keepdims=True)

    def accumulate(kj, buf):
        keys = pl.ds(pl.multiple_of(kj * blk, blk), blk)
        vt_aug = jnp.concatenate([vt_ref[:, keys], jnp.ones((_BF16_ROWS, blk), _BF16)], axis=0)
        block_bias = slope2 * jnp.full((1, 1), (kj - qi) * blk, jnp.int32).astype(_F32)
        for s_idx in range(2):
            m_old = m_ref[s_idx]
            m_new = jnp.maximum(m_old, bmax_ref[buf, s_idx] + block_bias)
            pt = jnp.exp2(s_ref[buf, s_idx] - (m_new - block_bias))
            acc_ref[s_idx] = acc_ref[s_idx] * jnp.exp2(m_old - m_new) + jnp.dot(
                vt_aug, pt.astype(_BF16), preferred_element_type=_F32)
            m_ref[s_idx] = m_new

    def block_of(t):
        return jnp.where(t == 0, qi, t - 1)

    def pair(u):
        t = 2 * u
        scores(t, False, 1)
        accumulate(block_of(t), 0)
        scores(t + 1, False, 0)
        accumulate(t, 1)

    def four_pairs(u, carry):
        for p in range(4):
            pair(4 * u + p)
        return carry

    scores(qi, True, 0)
    n_pairs = lax.shift_right_logical(qi, 1)
    n_fours = lax.shift_right_logical(n_pairs, 2)
    lax.fori_loop(0, n_fours, four_pairs, 0)

    @pl.when((n_pairs & 2) == 2)
    def _():
        pair(4 * n_fours)
        pair(4 * n_fours + 1)

    @pl.when((n_pairs & 1) == 1)
    def _():
        pair(n_pairs - 1)

    t_rest = 2 * n_pairs
    one_more = (qi & 1) == 1

    def finish():
        lp = lam_ref[...]
        lam = (jnp.exp(jnp.sum(lp[0:1, :] * lp[1:2, :], axis=-1, keepdims=True))
               - jnp.exp(jnp.sum(lp[2:3, :] * lp[3:4, :], axis=-1, keepdims=True)) + lam_init)
        ot = (acc_ref[0, :dv, :] * (1.0 / acc_ref[0, dv:dv + 1, :])
              - lam * (acc_ref[1, :dv, :] * (1.0 / acc_ref[1, dv:dv + 1, :])))
        ot = ot * lax.rsqrt(jnp.mean(ot * ot, axis=0, keepdims=True) + _EPS)
        o_ref[...] = (ot.T * g_ref[...] * (1.0 - lam_init)).astype(o_ref.dtype)

    @pl.when(one_more)
    def _():
        scores(t_rest, False, 1)
        accumulate(block_of(t_rest), 0)
        accumulate(t_rest, 1)
        finish()

    @pl.when(jnp.logical_not(one_more))
    def _():
        accumulate(block_of(t_rest), 0)
        finish()


def _diff_attention(q, k, k_head0, vt, lam_p, g_subln, batch, seq, heads, dh, lam_init):
    t = q.shape[0]
    dv = 2 * dh
    blk = min(_ATTN_BLOCK, seq)
    n_blk = seq // blk
    slopes = jnp.asarray([2.0 ** (-8.0 * (h + 1) / heads) for h in range(heads)], _F32)
    slopes = jnp.broadcast_to(slopes[:, None, None], (heads, 1, _LANES))
    return pl.pallas_call(
        functools.partial(_attn_kernel, dh=dh, lam_init=lam_init),
        grid=(batch, heads, n_blk),
        in_specs=[pl.BlockSpec((blk, dv), lambda b, h, i: (b * n_blk + i, h)),
                  pl.BlockSpec((seq, dv), lambda b, h, i: (b, k_head0 + h)),
                  pl.BlockSpec((None, dv, seq), lambda b, h, i: (b, h, 0)),
                  pl.BlockSpec((None, 1, _LANES), lambda b, h, i: (h, 0, 0)),
                  pl.BlockSpec((4, dh), lambda b, h, i: (0, 0)),
                  pl.BlockSpec((1, dv), lambda b, h, i: (0, h))],
        out_specs=pl.BlockSpec((blk, dv), lambda b, h, i: (b * n_blk + i, h)),
        out_shape=jax.ShapeDtypeStruct((t, heads * dv), _BF16),
        scratch_shapes=[pltpu.VMEM((2, blk, 2 * dh), _BF16),
                        pltpu.VMEM((blk, dh), _BF16),
                        pltpu.VMEM((2, 2, blk, blk), _F32),
                        pltpu.VMEM((2, 2, 1, blk), _F32),
                        pltpu.VMEM((2, 1, blk), _F32),
                        pltpu.VMEM((2, dv + _BF16_ROWS, blk), _F32)],
        compiler_params=_params("parallel", "parallel", "arbitrary"),
        name="diff_attention",
    )(q, k, vt, slopes, lam_p, g_subln)


def kernel(x, c, w_mod, b_mod, g_norm, w_up, w_down, w_a_in, b_a_gate, g_a_hnorm, w_a_out, g_kv, w_kv_mod,
           b_kv_mod, w_kv, w_b_q, b_lambda, g_b_subln, w_b_out, g_final):
    batch, seq, d = x.shape
    depth = w_mod.shape[0]
    n_a = w_a_in.shape[0]
    m_heads = b_a_gate.shape[1] // 2
    m_dv = g_a_hnorm.shape[1] // m_heads
    m_dk = (w_a_in.shape[2] - 2 * m_heads * m_dv - 2 * m_heads) // (2 * m_heads)
    d_dh = b_lambda.shape[2]
    d_heads = w_b_q.shape[2] // (2 * d_dh)
    t = batch * seq

    xt = x.reshape(t, d)
    c_rows = jnp.zeros((_SUBLANES, d), _F32).at[:batch].set(c)

    mod = _modvec(c_rows, w_mod, b_mod.reshape(depth, 1, 6 * d))[:, :batch]
    kv_mod = _modvec(c_rows, w_kv_mod[None], b_kv_mod.reshape(1, 1, 2 * d))[0, :batch]

    def vec(v):
        return v.reshape(batch, 1, d)

    qk_w, v_w = m_heads * m_dk, m_heads * m_dv
    main_w = 2 * qk_w + 2 * v_w
    w_up_b, w_down_b = w_up.astype(_BF16), w_down.astype(_BF16)
    w_a_in_b, w_a_out_b = w_a_in.astype(_BF16), w_a_out.astype(_BF16)
    w_b_q_b, w_b_out_b = w_b_q.astype(_BF16), w_b_out.astype(_BF16)

    ones_d = jnp.ones((1, d), _F32)
    for l in range(depth):
        sh1, sc1, gt1, sh2, sc2, gt2 = (vec(m) for m in jnp.split(mod[l], 6, axis=-1))
        if l < n_a:
            col_scale = jnp.concatenate([jnp.ones((qk_w,), _F32), jnp.full((qk_w,), m_dk ** -0.5, _F32),
                                         jnp.ones((2 * v_w,), _F32)]).reshape(1, main_w)
            w_gate = jnp.zeros((d, _LANES), _BF16).at[:, :2 * m_heads].set(w_a_in_b[l, :, main_w:])
            proj, gates = _norm_mod_matmul(xt, g_norm[l, 0][None], sc1, sh1, w_a_in_b, l, col_scale, seq,
                                           w_narrow=w_gate)
            b_gate = jnp.zeros((1, _LANES), _F32).at[0, :2 * m_heads].set(b_a_gate[l])
            mix_in = _mlstm(proj, gates, b_gate, g_a_hnorm[l][None], batch, seq, m_heads, m_dk, m_dv)
            xt = _matmul_residual(mix_in, w_a_out_b, l, xt, gt1, seq)
        else:
            j = l - n_a
            lam_init = 0.8 - 0.6 * math.exp(-0.3 * l)
            q_scale = d_dh ** -0.5 * _LOG2E
            k_w = w_b_q.shape[2]
            if l == n_a:
                kv_shift, kv_scale = (vec(m) for m in jnp.split(kv_mod, 2, axis=-1))
                w_qk = jnp.concatenate([w_b_q[j], w_kv[:, :k_w]], axis=1).astype(_BF16)
                q, vt_sh = _qkv_proj(xt, g_norm[l, 0][None], sc1, sh1, g_kv[None], kv_scale, kv_shift,
                                     w_qk, k_w, w_kv[:, k_w:].T.astype(_BF16), q_scale, batch, seq)
                k_sh = q
            else:
                q = _norm_mod_matmul(xt, g_norm[l, 0][None], sc1, sh1, w_b_q_b, j,
                                     jnp.full((1, k_w), q_scale, _F32), seq)
            mix_in = _diff_attention(q, k_sh, d_heads, vt_sh, b_lambda[j], g_b_subln[j][None], batch, seq,
                                     d_heads, d_dh, lam_init)
            xt = _matmul_residual(mix_in, w_b_out_b, j, xt, gt1, seq)
        last = l == depth - 1
        xt = _mlp(xt, g_norm[l, 1][None], sc2, sh2, gt2, w_up_b, w_down_b, l,
                  g_final[None] if last else ones_d, seq, last)
    return xt.reshape(batch, seq, d)
```
